```python
import math
import jax, jax.numpy as jnp
from jax import lax
import numpy as np

D_MODEL = 1024
BATCH = 2
SEQ = 8192
DEPTH = 1
DEC_BATCH = 128
DEC_SEQ = 1
PAST_LEN = 2048
PAGE_SIZE = 128

HEAD_DIM = 64
MIX_WIDTH = D_MODEL
ATTN_WIDTH = MIX_WIDTH // 2
SGU_WIDTH = MIX_WIDTH - ATTN_WIDTH
N_ATTN_HEADS = ATTN_WIDTH // HEAD_DIM
N_SGU_GROUPS = SGU_WIDTH // HEAD_DIM
SGU_GROUP_DIM = SGU_WIDTH // N_SGU_GROUPS
SGU_CHUNK = 128
MOBA_BLOCK = 256
MOBA_TOPK = 3
MOBA_Q_CHUNK = 64
GATHER_BUDGET = 1 << 26
D_FF = -(-8 * D_MODEL // (3 * 256)) * 256
IN_WIDTH = 3 * ATTN_WIDTH + 2 * SGU_WIDTH
RMS_EPS = 1e-6
LN_EPS = 1e-5
NEG_INF = -1e30

kernel_name = "hymba_moba_sgu_decoder_step"


def rmsnorm(x, g):
    xf = x.astype(jnp.float32)
    r = lax.rsqrt(jnp.mean(xf * xf, axis=-1, keepdims=True) + RMS_EPS)
    return (xf * r).astype(x.dtype) * g


def layernorm(x, g, b):
    xf = x.astype(jnp.float32)
    mu = jnp.mean(xf, axis=-1, keepdims=True)
    var = jnp.mean(jnp.square(xf - mu), axis=-1, keepdims=True)
    return ((xf - mu) * lax.rsqrt(var + LN_EPS)).astype(x.dtype) * g + b


def alibi_slopes(n):
    start = 2.0 ** (-8.0 / n)
    return jnp.asarray([start ** (i + 1) for i in range(n)], dtype=jnp.float32)


def moba_attention(q, q_pos, k, v):
    B, L, H, Dh = k.shape
    Q = q.shape[1]
    Lp = -(-L // MOBA_BLOCK) * MOBA_BLOCK
    nb = Lp // MOBA_BLOCK
    pad = ((0, 0), (0, Lp - L), (0, 0), (0, 0))
    kb = jnp.pad(k, pad).reshape(B, nb, MOBA_BLOCK, H, Dh).transpose(0, 3, 1, 2, 4)
    vb = jnp.pad(v, pad).reshape(B, nb, MOBA_BLOCK, H, Dh).transpose(0, 3, 1, 2, 4)
    k_mean = jnp.mean(kb.astype(jnp.float32), axis=3)
    topk = min(MOBA_TOPK, nb)
    nsel = topk + 1
    slopes = alibi_slopes(H)
    scale = 1.0 / math.sqrt(Dh)
    per_query = B * H * nsel * MOBA_BLOCK * Dh
    C = max(1, min(MOBA_Q_CHUNK, Q, GATHER_BUDGET // per_query))
    Qp = -(-Q // C) * C
    qp = jnp.pad(q, ((0, 0), (0, Qp - Q), (0, 0), (0, 0)))
    pos_p = jnp.pad(q_pos, (0, Qp - Q), mode="edge")
    qc = qp.reshape(B, Qp // C, C, H, Dh).transpose(1, 0, 3, 2, 4)
    pc = pos_p.reshape(Qp // C, C)
    b_idx = jnp.arange(B)[:, None, None, None]
    h_idx = jnp.arange(H)[None, :, None, None]
    blk_off = jnp.arange(MOBA_BLOCK)
    blk_ids = jnp.arange(nb)

    def chunk(args):
        qb, pb = args
        own = pb // MOBA_BLOCK
        gate = jnp.einsum("bhcd,bhnd->bhcn", qb.astype(jnp.float32), k_mean)
        fully_past = blk_ids[None, :] < own[:, None]
        gate = jnp.where(fully_past, gate, NEG_INF)
        _, top = lax.top_k(gate, topk)
        own_b = jnp.broadcast_to(own[None, None, :, None], (B, H, C, 1)).astype(top.dtype)
        sel = jnp.concatenate([top, own_b], axis=-1)
        sel_ok = jnp.concatenate([top < own[None, None, :, None], jnp.ones((B, H, C, 1), dtype=bool)], axis=-1)
        ks = kb[b_idx, h_idx, sel]
        vs = vb[b_idx, h_idx, sel]
        kpos = sel[..., None] * MOBA_BLOCK + blk_off
        qpos = pb[None, None, :, None, None]
        dist = jnp.abs(qpos - kpos).astype(jnp.float32)
        s = jnp.einsum("bhcd,bhcjkd->bhcjk", qb, ks).astype(jnp.float32) * scale \
            - slopes[None, :, None, None, None] * dist
        ok = sel_ok[..., None] & (kpos <= qpos)
        s = jnp.where(ok, s, NEG_INF).reshape(B, H, C, nsel * MOBA_BLOCK)
        p = jax.nn.softmax(s, axis=-1).astype(vs.dtype).reshape(B, H, C, nsel, MOBA_BLOCK)
        return jnp.einsum("bhcjk,bhcjkd->bhcd", p, vs)

    out = lax.map(chunk, (qc, pc))
    return out.transpose(1, 0, 3, 2, 4).reshape(B, Qp, H, Dh)[:, :Q]


def spatial_gating(u, gv, ln_g, ln_b, w_s, b_s):
    B, T, _ = u.shape
    vn = layernorm(gv, ln_g, ln_b)
    Tp = -(-T // SGU_CHUNK) * SGU_CHUNK
    vc = jnp.pad(vn, ((0, 0), (0, Tp - T), (0, 0))).reshape(B, Tp // SGU_CHUNK, SGU_CHUNK, N_SGU_GROUPS, SGU_GROUP_DIM)
    causal = jnp.tril(jnp.ones((SGU_CHUNK, SGU_CHUNK), dtype=bool))
    w = jnp.where(causal[None], w_s, 0)
    s = jnp.einsum("gts,bnsgd->bntgd", w, vc) + b_s.T[None, None, :, :, None]
    s = s.reshape(B, Tp, SGU_WIDTH)[:, :T]
    return u * s, vn


def decoder_layer(x, attend, norm1_g, w_in, attn_out_g, sgu_ln_g, sgu_ln_b, sgu_w, sgu_b,
                  sgu_out_g, w_out, norm2_g, w_gate, w_up, w_down):
    B, T, _ = x.shape
    h = rmsnorm(x, norm1_g)
    z = h @ w_in
    A, S = ATTN_WIDTH, SGU_WIDTH
    q, k, v, u, gv = jnp.split(z, [A, 2 * A, 3 * A, 3 * A + S], axis=-1)
    q = q.reshape(B, T, N_ATTN_HEADS, HEAD_DIM)
    k = k.reshape(B, T, N_ATTN_HEADS, HEAD_DIM)
    v = v.reshape(B, T, N_ATTN_HEADS, HEAD_DIM)
    a = attend(q, k, v).reshape(B, T, A)
    g, vn = spatial_gating(jax.nn.gelu(u), jax.nn.gelu(gv), sgu_ln_g, sgu_ln_b, sgu_w, sgu_b)
    mixed = jnp.concatenate([rmsnorm(a, attn_out_g), rmsnorm(g, sgu_out_g)], axis=-1) @ w_out
    x = x + mixed
    hf = rmsnorm(x, norm2_g)
    x = x + (jax.nn.silu(hf @ w_gate) * (hf @ w_up)) @ w_down
    return x, k, v, vn


def setup_inputs(seed: int = 0) -> dict:
    key = jax.random.key(seed)
    ks = jax.random.split(key, 24)
    f32 = jnp.float32
    n_pages = PAST_LEN // PAGE_SIZE
    n_used = DEC_BATCH * n_pages
    n_pool = n_used + n_used // 4

    def nrm(k, shape, scale=1.0):
        return jax.random.normal(k, shape, f32) * scale

    perm = jax.random.permutation(ks[2], n_pool)[:n_used]
    page_table = perm.reshape(DEC_BATCH, n_pages).astype(jnp.int32)
    return {
        "x_prompt": nrm(ks[0], (BATCH, SEQ, D_MODEL)),
        "x_sample": nrm(ks[1], (DEC_BATCH, DEC_SEQ, D_MODEL)),
        "cache_k": nrm(ks[3], (DEPTH, n_pool, PAGE_SIZE, N_ATTN_HEADS, HEAD_DIM)),
        "cache_v": nrm(ks[4], (DEPTH, n_pool, PAGE_SIZE, N_ATTN_HEADS, HEAD_DIM)),
        "page_table": page_table,
        "norm1_g": 1.0 + nrm(ks[5], (DEPTH, D_MODEL), 0.02),
        "w_in": nrm(ks[6], (DEPTH, D_MODEL, IN_WIDTH), D_MODEL ** -0.5),
        "attn_out_g": 1.0 + nrm(ks[7], (DEPTH, ATTN_WIDTH), 0.02),
        "sgu_ln_g": 1.0 + nrm(ks[8], (DEPTH, SGU_WIDTH), 0.02),
        "sgu_ln_b": nrm(ks[9], (DEPTH, SGU_WIDTH), 0.02),
        "sgu_w": nrm(ks[10], (DEPTH, N_SGU_GROUPS, SGU_CHUNK, SGU_CHUNK), SGU_CHUNK ** -0.5),
        "sgu_b": 1.0 + nrm(ks[11], (DEPTH, N_SGU_GROUPS, SGU_CHUNK), 0.02),
        "sgu_out_g": 1.0 + nrm(ks[12], (DEPTH, SGU_WIDTH), 0.02),
        "w_out": nrm(ks[13], (DEPTH, MIX_WIDTH, D_MODEL), MIX_WIDTH ** -0.5),
        "norm2_g": 1.0 + nrm(ks[14], (DEPTH, D_MODEL), 0.02),
        "w_gate": nrm(ks[15], (DEPTH, D_MODEL, D_FF), D_MODEL ** -0.5),
        "w_up": nrm(ks[16], (DEPTH, D_MODEL, D_FF), D_MODEL ** -0.5),
        "w_down": nrm(ks[17], (DEPTH, D_FF, D_MODEL), D_FF ** -0.5),
        "final_g": 1.0 + nrm(ks[18], (D_MODEL,), 0.02),
    }


def reference(x_prompt, x_sample, cache_k, cache_v, page_table, norm1_g, w_in, attn_out_g,
              sgu_ln_g, sgu_ln_b, sgu_w, sgu_b, sgu_out_g, w_out, norm2_g, w_gate, w_up,
              w_down, final_g):
    dec_b, n_pages = page_table.shape
    page = cache_k.shape[2]
    past_len = n_pages * page
    seq = x_prompt.shape[1]
    dec_seq = x_sample.shape[1]

    def prompt_attend(q, k, v):
        return moba_attention(q, jnp.arange(seq), k, v)

    xp, xs = x_prompt, x_sample
    kp_l, vp_l, ks_l, vs_l, sv_l = [], [], [], [], []
    for l in range(DEPTH):
        params = (norm1_g[l], w_in[l], attn_out_g[l], sgu_ln_g[l], sgu_ln_b[l], sgu_w[l], sgu_b[l],
                  sgu_out_g[l], w_out[l], norm2_g[l], w_gate[l], w_up[l], w_down[l])

        def sample_attend(q, k, v, l=l):
            k_past = cache_k[l][page_table].reshape(dec_b, past_len, N_ATTN_HEADS, HEAD_DIM)
            v_past = cache_v[l][page_table].reshape(dec_b, past_len, N_ATTN_HEADS, HEAD_DIM)
            k_all = jnp.concatenate([k_past.astype(k.dtype), k], axis=1)
            v_all = jnp.concatenate([v_past.astype(v.dtype), v], axis=1)
            return moba_attention(q, past_len + jnp.arange(dec_seq), k_all, v_all)

        xp, kp, vp, _ = decoder_layer(xp, prompt_attend, *params)
        xs, ksmp, vsmp, svs = decoder_layer(xs, sample_attend, *params)
        kp_l.append(kp)
        vp_l.append(vp)
        ks_l.append(ksmp)
        vs_l.append(vsmp)
        sv_l.append(svs)

    y_prompt = rmsnorm(xp, final_g)
    y_sample = rmsnorm(xs, final_g)
    k_prompt = jnp.stack(kp_l)
    v_prompt = jnp.stack(vp_l)
    k_sample = jnp.stack(ks_l)
    v_sample = jnp.stack(vs_l)
    sgu_v_sample = jnp.stack(sv_l)
    return (y_prompt, y_sample, k_prompt, v_prompt, k_sample, v_sample, sgu_v_sample)
```

```python
import functools
import math

import numpy as np
import jax
import jax.numpy as jnp
from jax import lax
from jax.experimental import pallas as pl
from jax.experimental.pallas import tpu as pltpu

HEAD_DIM = 64
N_HEADS = 8
ATTN_WIDTH = N_HEADS * HEAD_DIM
N_SGU_GROUPS = 8
SGU_GROUP_DIM = 64
SGU_WIDTH = N_SGU_GROUPS * SGU_GROUP_DIM
SGU_CHUNK = 128
MOBA_BLOCK = 256
MOBA_TOPK = 3
RMS_EPS = 1e-6
LN_EPS = 1e-5
NEG_INF = -1e30
LOG2E = 1.4426950408889634

LANES = 128
HEADS_PER_TILE = LANES // HEAD_DIM
N_HEAD_PAIRS = N_HEADS // HEADS_PER_TILE

F32 = jnp.float32
BF16 = jnp.bfloat16

_NT = (((1,), (1,)), ((), ()))
_NN = (((1,), (0,)), ((), ()))


def _dot(a, b, dims=_NN, precision=None):
    return lax.dot_general(a, b, dims, precision=precision, preferred_element_type=F32)


def _rmsnorm(x, g):
    r = lax.rsqrt(jnp.mean(x * x, axis=-1, keepdims=True) + RMS_EPS)
    return (x * r) * g


def _gelu_tanh(x):
    c = math.sqrt(2.0 / math.pi)
    return 0.5 * x * (1.0 + jnp.tanh(c * (x + 0.044715 * (x * x * x))))


def _alibi_slopes(n):
    start = 2.0 ** (-8.0 / n)
    return np.asarray([start ** (i + 1) for i in range(n)], dtype=np.float32)


def _split_weights(w_ref, whi_ref, wlo_ref, n_precise):
    rows = w_ref.shape[0]
    step = 128
    for r in range(0, rows, step):
        w = w_ref[r:r + step, :]
        hi = w.astype(BF16)
        whi_ref[r:r + step, :] = hi
        wlo_ref[r:r + step, :] = (w[:, :n_precise] - hi[:, :n_precise].astype(F32)).astype(BF16)


def _project(x_ref, g1_ref, whi_ref, wlo_ref):
    n_precise = wlo_ref.shape[1]
    h = _rmsnorm(x_ref[...], g1_ref[...])
    h_hi = h.astype(BF16)
    h_lo = (h - h_hi.astype(F32)).astype(BF16)
    w_qk = whi_ref[:, :n_precise]
    z_qk = _dot(h_hi, w_qk) + (_dot(h_lo, w_qk) + _dot(h_hi, wlo_ref[...]))
    z_rest = _dot(h_hi, whi_ref[:, n_precise:])
    return z_qk, z_rest


def _layernorm(x, g, b):
    mu = jnp.mean(x, axis=-1, keepdims=True)
    xc = x - mu
    var = jnp.mean(xc * xc, axis=-1, keepdims=True)
    return (xc * lax.rsqrt(var + LN_EPS)) * g + b


def _inproj_prompt_kernel(x_ref, g1_ref, w_ref, lng_ref, lnb_ref, sw_ref, sbx_ref, sog_ref,
                          q_ref, k_ref, v_ref, gn_ref,
                          whi_ref, wlo_ref, wcat_ref, s_ref):
    tm = x_ref.shape[0]
    a = ATTN_WIDTH

    @pl.when(pl.program_id(0) == 0)
    def _prepare():
        _split_weights(w_ref, whi_ref, wlo_ref, 2 * a)
        t = lax.broadcasted_iota(jnp.int32, (SGU_CHUNK, SGU_CHUNK), 0)
        s = lax.broadcasted_iota(jnp.int32, (SGU_CHUNK, SGU_CHUNK), 1)
        causal = t >= s
        for gp in range(N_SGU_GROUPS // 2):
            w0 = jnp.where(causal, sw_ref[2 * gp], 0.0)
            w1 = jnp.where(causal, sw_ref[2 * gp + 1], 0.0)
            wcat_ref[gp] = jnp.concatenate([w0, w1], axis=1).astype(BF16)

    z_qk, z_rest = _project(x_ref, g1_ref, whi_ref, wlo_ref)
    q_ref[...] = z_qk[:, :a]
    k_ref[...] = z_qk[:, a:]
    v_ref[...] = z_rest[:, :a]
    u = _gelu_tanh(z_rest[:, a:a + SGU_WIDTH])
    gv = _gelu_tanh(z_rest[:, a + SGU_WIDTH:])
    vn = _layernorm(gv, lng_ref[...], lnb_ref[...])

    lane = lax.broadcasted_iota(jnp.int32, (SGU_CHUNK, LANES), 1)
    low = lane < SGU_GROUP_DIM
    for c in range(tm // SGU_CHUNK):
        rows = slice(c * SGU_CHUNK, (c + 1) * SGU_CHUNK)
        for gp in range(N_SGU_GROUPS // 2):
            cols = slice(gp * LANES, (gp + 1) * LANES)
            vp = vn[rows, cols]
            rhs = jnp.concatenate([jnp.where(low, vp, 0.0), jnp.where(low, 0.0, vp)], axis=0).astype(BF16)
            s_ref[rows, cols] = _dot(wcat_ref[gp], rhs) + sbx_ref[:, cols]
    g = u * s_ref[...]
    gn_ref[...] = _rmsnorm(g, sog_ref[...]).astype(gn_ref.dtype)


def _inproj_sample_kernel(x_ref, g1_ref, w_ref, lng_ref, lnb_ref, w00_ref, b0_ref, sog_ref,
                          q_ref, k_ref, v_ref, gn_ref, vn_ref,
                          whi_ref, wlo_ref):
    a = ATTN_WIDTH

    @pl.when(pl.program_id(0) == 0)
    def _prepare():
        _split_weights(w_ref, whi_ref, wlo_ref, 2 * a)

    z_qk, z_rest = _project(x_ref, g1_ref, whi_ref, wlo_ref)
    q_ref[...] = z_qk[:, :a]
    k_ref[...] = z_qk[:, a:]
    v_ref[...] = z_rest[:, :a]
    u = _gelu_tanh(z_rest[:, a:a + SGU_WIDTH])
    gv = _gelu_tanh(z_rest[:, a + SGU_WIDTH:])
    vn = _layernorm(gv, lng_ref[...], lnb_ref[...])
    vn_ref[...] = vn
    g = u * (vn * w00_ref[...] + b0_ref[...])
    gn_ref[...] = _rmsnorm(g, sog_ref[...]).astype(gn_ref.dtype)


def _const_spec(shape):
    zeros = (0,) * len(shape)
    return pl.BlockSpec(shape, lambda *_: zeros, pipeline_mode=pl.Buffered(1))


def _inproj_prompt(x, g1, w_in, ln_g, ln_b, sgu_w, sgu_b, sog, *, tm):
    n, d = x.shape
    a = ATTN_WIDTH
    assert n % tm == 0 and tm % SGU_CHUNK == 0
    sbx = jnp.repeat(sgu_b.T, SGU_GROUP_DIM, axis=1)
    row = lambda w: pl.BlockSpec((tm, w), lambda i: (i, 0))
    out_shape = (jax.ShapeDtypeStruct((n, a), F32),) * 3 + (jax.ShapeDtypeStruct((n, SGU_WIDTH), BF16),)
    return pl.pallas_call(
        _inproj_prompt_kernel,
        grid=(n // tm,),
        in_specs=[row(d), _const_spec((1, d)), _const_spec(w_in.shape), _const_spec((1, SGU_WIDTH)),
                  _const_spec((1, SGU_WIDTH)), _const_spec(sgu_w.shape), _const_spec(sbx.shape),
                  _const_spec((1, SGU_WIDTH))],
        out_specs=(row(a), row(a), row(a), row(SGU_WIDTH)),
        out_shape=out_shape,
        scratch_shapes=[pltpu.VMEM(w_in.shape, BF16), pltpu.VMEM((d, 2 * a), BF16),
                        pltpu.VMEM((N_SGU_GROUPS // 2, SGU_CHUNK, 2 * SGU_CHUNK), BF16),
                        pltpu.VMEM((tm, SGU_WIDTH), F32)],
        compiler_params=pltpu.CompilerParams(dimension_semantics=("arbitrary",),
                                             vmem_limit_bytes=56 * 1024 * 1024),
        name="inproj_prompt",
    )(x, g1.reshape(1, d), w_in, ln_g.reshape(1, -1), ln_b.reshape(1, -1), sgu_w, sbx, sog.reshape(1, -1))


def _inproj_sample(x, g1, w_in, ln_g, ln_b, sgu_w, sgu_b, sog):
    n, d = x.shape
    a = ATTN_WIDTH
    w00 = jnp.repeat(sgu_w[:, 0, 0], SGU_GROUP_DIM).reshape(1, SGU_WIDTH)
    b0 = jnp.repeat(sgu_b[:, 0], SGU_GROUP_DIM).reshape(1, SGU_WIDTH)
    row = lambda w: pl.BlockSpec((n, w), lambda i: (0, 0))
    vec = _const_spec((1, SGU_WIDTH))
    out_shape = (jax.ShapeDtypeStruct((n, a), F32),) * 3 + (jax.ShapeDtypeStruct((n, SGU_WIDTH), BF16),
                                                           jax.ShapeDtypeStruct((n, SGU_WIDTH), F32))
    return pl.pallas_call(
        _inproj_sample_kernel,
        grid=(1,),
        in_specs=[row(d), _const_spec((1, d)), _const_spec(w_in.shape), vec, vec, vec, vec, vec],
        out_specs=(row(a), row(a), row(a), row(SGU_WIDTH), row(SGU_WIDTH)),
        out_shape=out_shape,
        scratch_shapes=[pltpu.VMEM(w_in.shape, BF16), pltpu.VMEM((d, 2 * a), BF16)],
        compiler_params=pltpu.CompilerParams(dimension_semantics=("arbitrary",),
                                             vmem_limit_bytes=56 * 1024 * 1024),
        name="inproj_sample",
    )(x, g1.reshape(1, d), w_in, ln_g.reshape(1, -1), ln_b.reshape(1, -1), w00, b0, sog.reshape(1, -1))


def _block_choice(gate_t, own):
    nb = gate_t.shape[0]
    blk = lax.broadcasted_iota(jnp.int32, gate_t.shape, 0)
    g = jnp.where(blk < own, gate_t, NEG_INF)
    rank = jnp.zeros(gate_t.shape, F32)
    for m in range(nb):
        gm = g[m:m + 1, :]
        rank = rank + jnp.where(blk > m, jnp.where(gm >= g, 1.0, 0.0), jnp.where(gm > g, 1.0, 0.0))
    return jnp.where((rank < MOBA_TOPK) & (blk < own), 1.0, 0.0)


def _attn_prompt_kernel(q_ref, k_ref, v_ref, sl_ref, o_ref,
                        kb_ref, vt_ref, kmean_ref, sel_ref, acc_ref):
    i = pl.program_id(2)
    blk = MOBA_BLOCK
    nb = k_ref.shape[1] // blk
    scale2 = LOG2E / math.sqrt(HEAD_DIM)

    @pl.when(i == 0)
    def _prepare():
        def body(c, carry):
            rows = pl.ds(pl.multiple_of(c * blk, blk), blk)
            kc = k_ref[0, rows, :]
            kb_ref[c] = kc.astype(BF16)
            kmean_ref[pl.ds(c, 1), :] = jnp.mean(kc, axis=0, keepdims=True)
            vt_ref[c] = v_ref[0, rows, :].T.astype(BF16)
            return carry
        lax.fori_loop(0, nb, body, 0)

    q = q_ref[0]
    lane = lax.broadcasted_iota(jnp.int32, q.shape, 1)
    kl = lax.broadcasted_iota(jnp.int32, (blk, blk), 0)
    ql = lax.broadcasted_iota(jnp.int32, (blk, blk), 1)
    klf = kl.astype(F32)

    qs, bias, m0, l0 = [], [], [], []
    for hh in range(HEADS_PER_TILE):
        qh = jnp.where((lane >= hh * HEAD_DIM) & (lane < (hh + 1) * HEAD_DIM), q, 0.0)
        gate_t = _dot(kmean_ref[...], qh, _NT, precision=lax.Precision.HIGHEST)
        sel_ref[hh] = _block_choice(gate_t, i)
        qs.append((qh * scale2).astype(BF16))
        slope2 = sl_ref[0, hh:hh + 1, :]
        bias.append(klf * slope2)
        s = _dot(kb_ref[i], qs[hh], _NT) + bias[hh]
        s = jnp.where(kl <= ql, s, NEG_INF)
        m = jnp.max(s, axis=0, keepdims=True)
        p = jnp.exp2(s - m)
        l0.append(jnp.sum(p, axis=0, keepdims=True))
        m0.append(m)
        head_rows = slice(hh * HEAD_DIM, (hh + 1) * HEAD_DIM)
        acc_ref[hh] = _dot(vt_ref[i, head_rows, :], p.astype(BF16))

    def past_block(j, carry):
        kj = kb_ref[j]
        off = (j - i).astype(F32) * float(blk)
        out = []
        for hh in range(HEADS_PER_TILE):
            m, l = carry[hh]
            slope2 = sl_ref[0, hh:hh + 1, :]
            c = slope2 * off
            chosen = sel_ref[hh, pl.ds(j, 1), :] > 0.5
            s = _dot(kj, qs[hh], _NT) + bias[hh]
            m_new = jnp.where(chosen, jnp.maximum(m, jnp.max(s, axis=0, keepdims=True) + c), m)
            alpha = jnp.exp2(m - m_new)
            shift = jnp.where(chosen, m_new - c, -NEG_INF)
            p = jnp.exp2(s - shift)
            l = alpha * l + jnp.sum(p, axis=0, keepdims=True)
            head_rows = slice(hh * HEAD_DIM, (hh + 1) * HEAD_DIM)
            acc_ref[hh] = alpha * acc_ref[hh] + _dot(vt_ref[j, head_rows, :], p.astype(BF16))
            out.append((m_new, l))
        return tuple(out)

    stats = lax.fori_loop(0, i, past_block, tuple(zip(m0, l0)))
    o_t = jnp.concatenate([acc_ref[hh] / stats[hh][1] for hh in range(HEADS_PER_TILE)], axis=0)
    o_ref[0] = o_t.T


def _attn_prompt(q, k, v):
    b, t, a = q.shape
    blk = MOBA_BLOCK
    assert t % blk == 0 and a == ATTN_WIDTH
    nb = t // blk
    slopes2 = (_alibi_slopes(N_HEADS) * np.float32(LOG2E)).reshape(N_HEAD_PAIRS, HEADS_PER_TILE, 1)
    slopes2 = jnp.asarray(np.broadcast_to(slopes2, (N_HEAD_PAIRS, HEADS_PER_TILE, blk)).copy())
    return pl.pallas_call(
        _attn_prompt_kernel,
        grid=(b, N_HEAD_PAIRS, nb),
        in_specs=[pl.BlockSpec((1, blk, LANES), lambda bi, p, i: (bi, i, p)),
                  pl.BlockSpec((1, t, LANES), lambda bi, p, i: (bi, 0, p)),
                  pl.BlockSpec((1, t, LANES), lambda bi, p, i: (bi, 0, p)),
                  pl.BlockSpec((1, HEADS_PER_TILE, blk), lambda bi, p, i: (p, 0, 0))],
        out_specs=pl.BlockSpec((1, blk, LANES), lambda bi, p, i: (bi, i, p)),
        out_shape=jax.ShapeDtypeStruct((b, t, a), F32),
        scratch_shapes=[pltpu.VMEM((nb, blk, LANES), BF16), pltpu.VMEM((nb, LANES, blk), BF16),
                        pltpu.VMEM((nb, LANES), F32), pltpu.VMEM((HEADS_PER_TILE, nb, blk), F32),
                        pltpu.VMEM((HEADS_PER_TILE, HEAD_DIM, blk), F32)],
        compiler_params=pltpu.CompilerParams(dimension_semantics=("arbitrary", "arbitrary", "arbitrary"),
                                             vmem_limit_bytes=48 * 1024 * 1024),
        name="attn_prompt",
    )(q, k, v, slopes2)


def _attn_sample_kernel(pt_ref, q_ref, kn_ref, vn_ref, sl_ref, *refs, n_pages, page):
    del pt_ref
    k_pages, v_pages, o_ref = refs[:n_pages], refs[n_pages:2 * n_pages], refs[2 * n_pages]
    past = n_pages * page
    pages_per_block = MOBA_BLOCK // page
    nb = past // MOBA_BLOCK
    scale = 1.0 / math.sqrt(HEAD_DIM)

    q = q_ref[0]
    head = lax.broadcasted_iota(jnp.int32, (N_HEADS, ATTN_WIDTH), 0)
    lane_head = lax.broadcasted_iota(jnp.int32, (N_HEADS, ATTN_WIDTH), 1) // HEAD_DIM
    mine = head == lane_head
    q_rows = jnp.where(mine, q, 0.0)

    k_mean = jnp.concatenate(
        [sum(jnp.sum(k_pages[n * pages_per_block + r][0], axis=0, keepdims=True) for r in range(pages_per_block))
         for n in range(nb)], axis=0) * (1.0 / MOBA_BLOCK)
    gate = _dot(q_rows, k_mean, _NT, precision=lax.Precision.HIGHEST)
    blk = lax.broadcasted_iota(jnp.int32, gate.shape, 1)
    rank = jnp.where(NEG_INF > gate, 1.0, 0.0)
    for m in range(nb):
        gm = gate[:, m:m + 1]
        rank = rank + jnp.where(blk > m, jnp.where(gm >= gate, 1.0, 0.0), jnp.where(gm > gate, 1.0, 0.0))
    chosen = jnp.where(rank < MOBA_TOPK, 1.0, 0.0)

    slopes = sl_ref[:, :1]
    q_bf = q_rows.astype(BF16)
    s_pages = []
    for pg in range(n_pages):
        s = _dot(q_bf, k_pages[pg][0].astype(BF16), _NT) * scale
        kpos = lax.broadcasted_iota(jnp.int32, s.shape, 1) + pg * page
        s = s - slopes * (past - kpos).astype(F32)
        n = pg // pages_per_block
        ok = jnp.broadcast_to(chosen[:, n:n + 1], s.shape) > 0.5
        s_pages.append(jnp.where(ok, s, NEG_INF))
    s_own = jnp.sum(q_rows * kn_ref[0], axis=1, keepdims=True) * scale
    m = s_own
    for s in s_pages:
        m = jnp.maximum(m, jnp.max(s, axis=1, keepdims=True))
    p_own = jnp.exp(s_own - m)
    l = p_own
    out = p_own * vn_ref[0]
    for pg in range(n_pages):
        p = jnp.exp(s_pages[pg] - m)
        l = l + jnp.sum(p, axis=1, keepdims=True)
        out = out + _dot(p.astype(BF16), v_pages[pg][0].astype(BF16))
    out = out / l
    o_ref[0] = jnp.sum(jnp.where(mine, out, 0.0), axis=0, keepdims=True)


def _attn_sample(q, k_new, v_new, cache_k, cache_v, page_table):
    n, a = q.shape
    n_pool, page, h, dh = cache_k.shape
    n_pages = page_table.shape[1]
    assert h * dh == a and MOBA_BLOCK % page == 0 and (n_pages * page) % MOBA_BLOCK == 0
    ck = cache_k.reshape(n_pool, page, a)
    cv = cache_v.reshape(n_pool, page, a)
    slopes = jnp.asarray(np.broadcast_to(_alibi_slopes(N_HEADS).reshape(N_HEADS, 1), (N_HEADS, LANES)).copy())
    tok = pl.BlockSpec((1, 1, a), lambda b, pt: (b, 0, 0))
    page_spec = lambda pg: pl.BlockSpec((1, page, a), lambda b, pt: (pt[b, pg], 0, 0))
    grid_spec = pltpu.PrefetchScalarGridSpec(
        num_scalar_prefetch=1,
        grid=(n,),
        in_specs=[tok, tok, tok, pl.BlockSpec((N_HEADS, LANES), lambda b, pt: (0, 0))]
                 + [page_spec(pg) for pg in range(n_pages)] * 2,
        out_specs=tok,
    )
    out = pl.pallas_call(
        functools.partial(_attn_sample_kernel, n_pages=n_pages, page=page),
        grid_spec=grid_spec,
        out_shape=jax.ShapeDtypeStruct((n, 1, a), F32),
        compiler_params=pltpu.CompilerParams(dimension_semantics=("arbitrary",),
                                             vmem_limit_bytes=48 * 1024 * 1024),
        name="attn_sample",
    )(page_table, q.reshape(n, 1, a), k_new.reshape(n, 1, a), v_new.reshape(n, 1, a), slopes,
      *([ck] * n_pages), *([cv] * n_pages))
    return out.reshape(n, a)


def _outffn_kernel(x_ref, a_ref, gn_ref, ag_ref, wo_ref, g2_ref, wg_ref, wu_ref, wd_ref, gf_ref, y_ref,
                   *, ff_chunk):
    a = ATTN_WIDTH
    an = _rmsnorm(a_ref[...], ag_ref[...]).astype(BF16)
    mixed = _dot(an, wo_ref[:a, :]) + _dot(gn_ref[...], wo_ref[a:, :])
    x1 = x_ref[...] + mixed
    hf = _rmsnorm(x1, g2_ref[...]).astype(BF16)
    d_ff = wg_ref.shape[1]
    ff = None
    for c in range(0, d_ff, ff_chunk):
        gate = _dot(hf, wg_ref[:, c:c + ff_chunk])
        up = _dot(hf, wu_ref[:, c:c + ff_chunk])
        act = (gate * jax.nn.sigmoid(gate) * up).astype(BF16)
        part = _dot(act, wd_ref[c:c + ff_chunk, :])
        ff = part if ff is None else ff + part
    y_ref[...] = _rmsnorm(x1 + ff, gf_ref[...])


def _outffn(x, a, gn, ag, w_out, g2, w_gate, w_up, w_down, gf, *, tm):
    n, d = x.shape
    d_ff = w_gate.shape[1]
    assert n % tm == 0
    ff_chunk = d_ff // 2 if (d_ff // 2) % LANES == 0 else d_ff
    row = lambda w: pl.BlockSpec((tm, w), lambda i: (i, 0))
    return pl.pallas_call(
        functools.partial(_outffn_kernel, ff_chunk=ff_chunk),
        grid=(n // tm,),
        in_specs=[row(d), row(ATTN_WIDTH), row(SGU_WIDTH), _const_spec((1, ATTN_WIDTH)), _const_spec(w_out.shape),
                  _const_spec((1, d)), _const_spec(w_gate.shape), _const_spec(w_up.shape),
                  _const_spec(w_down.shape), _const_spec((1, d))],
        out_specs=row(d),
        out_shape=jax.ShapeDtypeStruct((n, d), F32),
        compiler_params=pltpu.CompilerParams(dimension_semantics=("arbitrary",),
                                             vmem_limit_bytes=56 * 1024 * 1024),
        name="outffn",
    )(x, a, gn, ag.reshape(1, -1), w_out, g2.reshape(1, -1), w_gate, w_up, w_down, gf.reshape(1, -1))


def kernel(x_prompt, x_sample, cache_k, cache_v, page_table, norm1_g, w_in, attn_out_g, sgu_ln_g, sgu_ln_b,
           sgu_w, sgu_b, sgu_out_g, w_out, norm2_g, w_gate, w_up, w_down, final_g):
    depth = w_in.shape[0]
    assert depth == 1, "single-layer stack"
    l = 0
    bsz, seq, d = x_prompt.shape
    dec_b, dec_seq, _ = x_sample.shape
    assert dec_seq == 1
    xp = x_prompt.reshape(bsz * seq, d)
    xs = x_sample.reshape(dec_b, d)

    wo, wg, wu, wd = (w[l].astype(BF16) for w in (w_out, w_gate, w_up, w_down))

    qp, kp, vp, gnp = _inproj_prompt(xp, norm1_g[l], w_in[l], sgu_ln_g[l], sgu_ln_b[l], sgu_w[l], sgu_b[l],
                                     sgu_out_g[l], tm=512)
    ap = _attn_prompt(qp.reshape(bsz, seq, -1), kp.reshape(bsz, seq, -1), vp.reshape(bsz, seq, -1))
    yp = _outffn(xp, ap.reshape(bsz * seq, -1), gnp, attn_out_g[l], wo, norm2_g[l], wg, wu, wd, final_g, tm=512)

    qs, ks, vs, gns, vns = _inproj_sample(xs, norm1_g[l], w_in[l], sgu_ln_g[l], sgu_ln_b[l], sgu_w[l], sgu_b[l],
                                          sgu_out_g[l])
    a_s = _attn_sample(qs, ks, vs, cache_k[l], cache_v[l], page_table)
    ys = _outffn(xs, a_s, gns, attn_out_g[l], wo, norm2_g[l], wg, wu, wd, final_g, tm=dec_b)

    y_prompt = yp.reshape(bsz, seq, d)
    y_sample = ys.reshape(dec_b, dec_seq, d)
    k_prompt = kp.reshape(depth, bsz, seq, N_HEADS, HEAD_DIM)
    v_prompt = vp.reshape(depth, bsz, seq, N_HEADS, HEAD_DIM)
    k_sample = ks.reshape(depth, dec_b, dec_seq, N_HEADS, HEAD_DIM)
    v_sample = vs.reshape(depth, dec_b, dec_seq, N_HEADS, HEAD_DIM)
    sgu_v_sample = vns.reshape(depth, dec_b, dec_seq, SGU_WIDTH)
    return (y_prompt, y_sample, k_prompt, v_prompt, k_sample, v_sample, sgu_v_sample)
```

```python
import functools
import math

import numpy as np
import jax
import jax.numpy as jnp
from jax import lax
from jax.experimental import pallas as pl
from jax.experimental.pallas import tpu as pltpu

HEAD_DIM = 64
N_HEADS = 8
ATTN_WIDTH = N_HEADS * HEAD_DIM
N_SGU_GROUPS = 8
SGU_GROUP_DIM = 64
SGU_WIDTH = N_SGU_GROUPS * SGU_GROUP_DIM
SGU_CHUNK = 128
MOBA_BLOCK = 256
MOBA_TOPK = 3
RMS_EPS = 1e-6
LN_EPS = 1e-5
NEG_INF = -1e30
LOG2E = 1.4426950408889634

LANES = 128
HEADS_PER_TILE = LANES // HEAD_DIM
N_HEAD_PAIRS = N_HEADS // HEADS_PER_TILE

F32 = jnp.float32
BF16 = jnp.bfloat16

_NT = (((1,), (1,)), ((), ()))
_NN = (((1,), (0,)), ((), ()))


def _dot(a, b, dims=_NN, precision=None):
    return lax.dot_general(a, b, dims, precision=precision, preferred_element_type=F32)


def _rmsnorm(x, g):
    r = lax.rsqrt(jnp.mean(x * x, axis=-1, keepdims=True) + RMS_EPS)
    return (x * r) * g


def _gelu_tanh(x):
    c = math.sqrt(2.0 / math.pi)
    return 0.5 * x * (1.0 + jnp.tanh(c * (x + 0.044715 * (x * x * x))))


def _alibi_slopes(n):
    start = 2.0 ** (-8.0 / n)
    return np.asarray([start ** (i + 1) for i in range(n)], dtype=np.float32)


def _split_weights(w_ref, whi_ref, wlo_ref, n_precise):
    rows = w_ref.shape[0]
    step = 128
    for r in range(0, rows, step):
        w = w_ref[r:r + step, :]
        hi = w.astype(BF16)
        whi_ref[r:r + step, :] = hi
        wlo_ref[r:r + step, :] = (w[:, :n_precise] - hi[:, :n_precise].astype(F32)).astype(BF16)


def _project(x_ref, g1_ref, whi_ref, wlo_ref):
    n_precise = wlo_ref.shape[1]
    h = _rmsnorm(x_ref[...], g1_ref[...])
    h_hi = h.astype(BF16)
    h_lo = (h - h_hi.astype(F32)).astype(BF16)
    w_qk = whi_ref[:, :n_precise]
    z_qk = _dot(h_hi, w_qk) + (_dot(h_lo, w_qk) + _dot(h_hi, wlo_ref[...]))
    z_rest = _dot(h_hi, whi_ref[:, n_precise:])
    return z_qk, z_rest


def _layernorm(x, g, b):
    mu = jnp.mean(x, axis=-1, keepdims=True)
    xc = x - mu
    var = jnp.mean(xc * xc, axis=-1, keepdims=True)
    return (xc * lax.rsqrt(var + LN_EPS)) * g + b


def _inproj_prompt_kernel(x_ref, g1_ref, w_ref, lng_ref, lnb_ref, sw_ref, sbx_ref, sog_ref,
                          q_ref, k_ref, v_ref, gn_ref,
                          whi_ref, wlo_ref, wcat_ref, s_ref):
    tm = x_ref.shape[0]
    a = ATTN_WIDTH

    @pl.when(pl.program_id(0) == 0)
    def _prepare():
        _split_weights(w_ref, whi_ref, wlo_ref, 2 * a)
        t = lax.broadcasted_iota(jnp.int32, (SGU_CHUNK, SGU_CHUNK), 0)
        s = lax.broadcasted_iota(jnp.int32, (SGU_CHUNK, SGU_CHUNK), 1)
        causal = t >= s
        for gp in range(N_SGU_GROUPS // 2):
            w0 = jnp.where(causal, sw_ref[2 * gp], 0.0)
            w1 = jnp.where(causal, sw_ref[2 * gp + 1], 0.0)
            wcat_ref[gp] = jnp.concatenate([w0, w1], axis=1).astype(BF16)

    z_qk, z_rest = _project(x_ref, g1_ref, whi_ref, wlo_ref)
    q_ref[...] = z_qk[:, :a]
    k_ref[...] = z_qk[:, a:]
    v_ref[...] = z_rest[:, :a]
    u = _gelu_tanh(z_rest[:, a:a + SGU_WIDTH])
    gv = _gelu_tanh(z_rest[:, a + SGU_WIDTH:])
    vn = _layernorm(gv, lng_ref[...], lnb_ref[...])

    lane = lax.broadcasted_iota(jnp.int32, (SGU_CHUNK, LANES), 1)
    low = lane < SGU_GROUP_DIM
    for c in range(tm // SGU_CHUNK):
        rows = slice(c * SGU_CHUNK, (c + 1) * SGU_CHUNK)
        for gp in range(N_SGU_GROUPS // 2):
            cols = slice(gp * LANES, (gp + 1) * LANES)
            vp = vn[rows, cols]
            rhs = jnp.concatenate([jnp.where(low, vp, 0.0), jnp.where(low, 0.0, vp)], axis=0).astype(BF16)
            s_ref[rows, cols] = _dot(wcat_ref[gp], rhs) + sbx_ref[:, cols]
    g = u * s_ref[...]
    gn_ref[...] = _rmsnorm(g, sog_ref[...]).astype(gn_ref.dtype)


def _inproj_sample_kernel(x_ref, g1_ref, w_ref, lng_ref, lnb_ref, w00_ref, b0_ref, sog_ref,
                          q_ref, k_ref, v_ref, gn_ref, vn_ref,
                          whi_ref, wlo_ref):
    a = ATTN_WIDTH

    @pl.when(pl.program_id(0) == 0)
    def _prepare():
        _split_weights(w_ref, whi_ref, wlo_ref, 2 * a)

    z_qk, z_rest = _project(x_ref, g1_ref, whi_ref, wlo_ref)
    q_ref[...] = z_qk[:, :a]
    k_ref[...] = z_qk[:, a:]
    v_ref[...] = z_rest[:, :a]
    u = _gelu_tanh(z_rest[:, a:a + SGU_WIDTH])
    gv = _gelu_tanh(z_rest[:, a + SGU_WIDTH:])
    vn = _layernorm(gv, lng_ref[...], lnb_ref[...])
    vn_ref[...] = vn
    g = u * (vn * w00_ref[...] + b0_ref[...])
    gn_ref[...] = _rmsnorm(g, sog_ref[...]).astype(gn_ref.dtype)


def _const_spec(shape):
    zeros = (0,) * len(shape)
    return pl.BlockSpec(shape, lambda *_: zeros, pipeline_mode=pl.Buffered(1))


def _inproj_prompt(x, g1, w_in, ln_g, ln_b, sgu_w, sgu_b, sog, *, tm):
    n, d = x.shape
    a = ATTN_WIDTH
    assert n % tm == 0 and tm % SGU_CHUNK == 0
    sbx = jnp.repeat(sgu_b.T, SGU_GROUP_DIM, axis=1)
    row = lambda w: pl.BlockSpec((tm, w), lambda i: (i, 0))
    out_shape = (jax.ShapeDtypeStruct((n, a), F32),) * 3 + (jax.ShapeDtypeStruct((n, SGU_WIDTH), BF16),)
    return pl.pallas_call(
        _inproj_prompt_kernel,
        grid=(n // tm,),
        in_specs=[row(d), _const_spec((1, d)), _const_spec(w_in.shape), _const_spec((1, SGU_WIDTH)),
                  _const_spec((1, SGU_WIDTH)), _const_spec(sgu_w.shape), _const_spec(sbx.shape),
                  _const_spec((1, SGU_WIDTH))],
        out_specs=(row(a), row(a), row(a), row(SGU_WIDTH)),
        out_shape=out_shape,
        scratch_shapes=[pltpu.VMEM(w_in.shape, BF16), pltpu.VMEM((d, 2 * a), BF16),
                        pltpu.VMEM((N_SGU_GROUPS // 2, SGU_CHUNK, 2 * SGU_CHUNK), BF16),
                        pltpu.VMEM((tm, SGU_WIDTH), F32)],
        compiler_params=pltpu.CompilerParams(dimension_semantics=("arbitrary",),
                                             vmem_limit_bytes=56 * 1024 * 1024),
        name="inproj_prompt",
    )(x, g1.reshape(1, d), w_in, ln_g.reshape(1, -1), ln_b.reshape(1, -1), sgu_w, sbx, sog.reshape(1, -1))


def _inproj_sample(x, g1, w_in, ln_g, ln_b, sgu_w, sgu_b, sog):
    n, d = x.shape
    a = ATTN_WIDTH
    w00 = jnp.repeat(sgu_w[:, 0, 0], SGU_GROUP_DIM).reshape(1, SGU_WIDTH)
    b0 = jnp.repeat(sgu_b[:, 0], SGU_GROUP_DIM).reshape(1, SGU_WIDTH)
    row = lambda w: pl.BlockSpec((n, w), lambda i: (0, 0))
    vec = _const_spec((1, SGU_WIDTH))
    out_shape = (jax.ShapeDtypeStruct((n, a), F32),) * 3 + (jax.ShapeDtypeStruct((n, SGU_WIDTH), BF16),
                                                           jax.ShapeDtypeStruct((n, SGU_WIDTH), F32))
    return pl.pallas_call(
        _inproj_sample_kernel,
        grid=(1,),
        in_specs=[row(d), _const_spec((1, d)), _const_spec(w_in.shape), vec, vec, vec, vec, vec],
        out_specs=(row(a), row(a), row(a), row(SGU_WIDTH), row(SGU_WIDTH)),
        out_shape=out_shape,
        scratch_shapes=[pltpu.VMEM(w_in.shape, BF16), pltpu.VMEM((d, 2 * a), BF16)],
        compiler_params=pltpu.CompilerParams(dimension_semantics=("arbitrary",),
                                             vmem_limit_bytes=56 * 1024 * 1024),
        name="inproj_sample",
    )(x, g1.reshape(1, d), w_in, ln_g.reshape(1, -1), ln_b.reshape(1, -1), w00, b0, sog.reshape(1, -1))


def _block_choice(gate_t, own):
    nb = gate_t.shape[0]
    blk = lax.broadcasted_iota(jnp.int32, gate_t.shape, 0)
    g = jnp.where(blk < own, gate_t, NEG_INF)
    rank = jnp.zeros(gate_t.shape, F32)
    for m in range(nb):
        gm = g[m:m + 1, :]
        rank = rank + jnp.where(blk > m, jnp.where(gm >= g, 1.0, 0.0), jnp.where(gm > g, 1.0, 0.0))
    return jnp.where((rank < MOBA_TOPK) & (blk < own), 1.0, 0.0)


def _attn_prompt_kernel(q_ref, k_ref, v_ref, sl_ref, o_ref,
                        kb_ref, vt_ref, kmean_ref, sel_ref, acc_ref):
    i = pl.program_id(2)
    blk = MOBA_BLOCK
    nb = k_ref.shape[1] // blk
    scale2 = LOG2E / math.sqrt(HEAD_DIM)

    @pl.when(i == 0)
    def _prepare():
        def body(c, carry):
            rows = pl.ds(pl.multiple_of(c * blk, blk), blk)
            kc = k_ref[0, rows, :]
            kb_ref[c] = kc.astype(BF16)
            kmean_ref[pl.ds(c, 1), :] = jnp.mean(kc, axis=0, keepdims=True)
            vt_ref[c] = v_ref[0, rows, :].T.astype(BF16)
            return carry
        lax.fori_loop(0, nb, body, 0)

    q = q_ref[0]
    lane = lax.broadcasted_iota(jnp.int32, q.shape, 1)
    kl = lax.broadcasted_iota(jnp.int32, (blk, blk), 0)
    ql = lax.broadcasted_iota(jnp.int32, (blk, blk), 1)
    klf = kl.astype(F32)

    qs, bias, m0, l0 = [], [], [], []
    for hh in range(HEADS_PER_TILE):
        qh = jnp.where((lane >= hh * HEAD_DIM) & (lane < (hh + 1) * HEAD_DIM), q, 0.0)
        gate_t = _dot(kmean_ref[...], qh, _NT, precision=lax.Precision.HIGHEST)
        sel_ref[hh] = _block_choice(gate_t, i)
        qs.append((qh * scale2).astype(BF16))
        slope2 = sl_ref[0, hh:hh + 1, :]
        bias.append(klf * slope2)
        s = _dot(kb_ref[i], qs[hh], _NT) + bias[hh]
        s = jnp.where(kl <= ql, s, NEG_INF)
        m = jnp.max(s, axis=0, keepdims=True)
        p = jnp.exp2(s - m)
        l0.append(jnp.sum(p, axis=0, keepdims=True))
        m0.append(m)
        head_rows = slice(hh * HEAD_DIM, (hh + 1) * HEAD_DIM)
        acc_ref[hh] = _dot(vt_ref[i, head_rows, :], p.astype(BF16))

    def past_block(j, carry):
        kj = kb_ref[j]
        off = (j - i).astype(F32) * float(blk)
        out = []
        for hh in range(HEADS_PER_TILE):
            m, l = carry[hh]
            slope2 = sl_ref[0, hh:hh + 1, :]
            c = slope2 * off
            chosen = sel_ref[hh, pl.ds(j, 1), :] > 0.5
            s = _dot(kj, qs[hh], _NT) + bias[hh]
            m_new = jnp.where(chosen, jnp.maximum(m, jnp.max(s, axis=0, keepdims=True) + c), m)
            alpha = jnp.exp2(m - m_new)
            shift = jnp.where(chosen, m_new - c, -NEG_INF)
            p = jnp.exp2(s - shift)
            l = alpha * l + jnp.sum(p, axis=0, keepdims=True)
            head_rows = slice(hh * HEAD_DIM, (hh + 1) * HEAD_DIM)
            acc_ref[hh] = alpha * acc_ref[hh] + _dot(vt_ref[j, head_rows, :], p.astype(BF16))
            out.append((m_new, l))
        return tuple(out)

    stats = lax.fori_loop(0, i, past_block, tuple(zip(m0, l0)))
    o_t = jnp.concatenate([acc_ref[hh] / stats[hh][1] for hh in range(HEADS_PER_TILE)], axis=0)
    o_ref[0] = o_t.T


def _attn_prompt(q, k, v):
    b, t, a = q.shape
    blk = MOBA_BLOCK
    assert t % blk == 0 and a == ATTN_WIDTH
    nb = t // blk
    slopes2 = (_alibi_slopes(N_HEADS) * np.float32(LOG2E)).reshape(N_HEAD_PAIRS, HEADS_PER_TILE, 1)
    slopes2 = jnp.asarray(np.broadcast_to(slopes2, (N_HEAD_PAIRS, HEADS_PER_TILE, blk)).copy())
    return pl.pallas_call(
        _attn_prompt_kernel,
        grid=(b, N_HEAD_PAIRS, nb),
        in_specs=[pl.BlockSpec((1, blk, LANES), lambda bi, p, i: (bi, i, p)),
                  pl.BlockSpec((1, t, LANES), lambda bi, p, i: (bi, 0, p)),
                  pl.BlockSpec((1, t, LANES), lambda bi, p, i: (bi, 0, p)),
                  pl.BlockSpec((1, HEADS_PER_TILE, blk), lambda bi, p, i: (p, 0, 0))],
        out_specs=pl.BlockSpec((1, blk, LANES), lambda bi, p, i: (bi, i, p)),
        out_shape=jax.ShapeDtypeStruct((b, t, a), F32),
        scratch_shapes=[pltpu.VMEM((nb, blk, LANES), BF16), pltpu.VMEM((nb, LANES, blk), BF16),
                        pltpu.VMEM((nb, LANES), F32), pltpu.VMEM((HEADS_PER_TILE, nb, blk), F32),
                        pltpu.VMEM((HEADS_PER_TILE, HEAD_DIM, blk), F32)],
        compiler_params=pltpu.CompilerParams(dimension_semantics=("arbitrary", "arbitrary", "arbitrary"),
                                             vmem_limit_bytes=48 * 1024 * 1024),
        name="attn_prompt",
    )(q, k, v, slopes2)


def _attn_sample_kernel(pt_ref, q_ref, kn_ref, vn_ref, sl_ref, *refs, n_pages, page):
    del pt_ref
    k_pages, v_pages, o_ref = refs[:n_pages], refs[n_pages:2 * n_pages], refs[2 * n_pages]
    past = n_pages * page
    pages_per_block = MOBA_BLOCK // page
    nb = past // MOBA_BLOCK
    scale = 1.0 / math.sqrt(HEAD_DIM)

    q = q_ref[0]
    k_mean = []
    for n in range(nb):
        tot = None
        for r in range(pages_per_block):
            part = jnp.sum(k_pages[n * pages_per_block + r][0], axis=0)
            tot = part if tot is None else tot + part
        k_mean.append(tot * (1.0 / MOBA_BLOCK))
    gate = [jnp.sum(q * km, axis=1, keepdims=True) for km in k_mean]
    chosen = []
    for n in range(nb):
        rank = jnp.where(NEG_INF > gate[n], 1.0, 0.0)
        for m in range(nb):
            if m != n:
                ahead = (gate[m] >= gate[n]) if m < n else (gate[m] > gate[n])
                rank = rank + jnp.where(ahead, 1.0, 0.0)
        chosen.append(jnp.where(rank < MOBA_TOPK, 1.0, 0.0))

    slopes = sl_ref[:, :1]
    q_bf = q.astype(BF16)
    cols = page * N_HEADS
    col = lax.broadcasted_iota(jnp.int32, (N_HEADS, cols), 1)
    same_head = (col % N_HEADS) == lax.broadcasted_iota(jnp.int32, (N_HEADS, cols), 0)
    dist0 = (past - col // N_HEADS).astype(F32)
    s_pages = []
    for pg in range(n_pages):
        rows = k_pages[pg][0].reshape(cols, HEAD_DIM).astype(BF16)
        s = _dot(q_bf, rows, _NT) * scale - slopes * (dist0 - float(pg * page))
        ok = same_head & (jnp.broadcast_to(chosen[pg // pages_per_block], s.shape) > 0.5)
        s_pages.append(jnp.where(ok, s, NEG_INF))
    s_own = jnp.sum(q * kn_ref[0], axis=1, keepdims=True) * scale
    m = s_own
    for s in s_pages:
        m = jnp.maximum(m, jnp.max(s, axis=1, keepdims=True))
    p_own = jnp.exp(s_own - m)
    l = p_own
    out = p_own * vn_ref[0]
    for pg in range(n_pages):
        p = jnp.exp(s_pages[pg] - m)
        l = l + jnp.sum(p, axis=1, keepdims=True)
        out = out + _dot(p.astype(BF16), v_pages[pg][0].reshape(cols, HEAD_DIM).astype(BF16))
    o_ref[0] = out / l


def _attn_sample(q, k_new, v_new, cache_k, cache_v, page_table):
    n, a = q.shape
    n_pool, page, h, dh = cache_k.shape
    n_pages = page_table.shape[1]
    assert h * dh == a and MOBA_BLOCK % page == 0 and (n_pages * page) % MOBA_BLOCK == 0
    slopes = jnp.asarray(np.broadcast_to(_alibi_slopes(N_HEADS).reshape(N_HEADS, 1), (N_HEADS, LANES)).copy())
    tok = pl.BlockSpec((1, h, dh), lambda b, pt: (b, 0, 0))
    page_spec = lambda pg: pl.BlockSpec((1, page, h, dh), lambda b, pt: (pt[b, pg], 0, 0, 0))
    grid_spec = pltpu.PrefetchScalarGridSpec(
        num_scalar_prefetch=1,
        grid=(n,),
        in_specs=[tok, tok, tok, pl.BlockSpec((N_HEADS, LANES), lambda b, pt: (0, 0))]
                 + [page_spec(pg) for pg in range(n_pages)] * 2,
        out_specs=tok,
    )
    out = pl.pallas_call(
        functools.partial(_attn_sample_kernel, n_pages=n_pages, page=page),
        grid_spec=grid_spec,
        out_shape=jax.ShapeDtypeStruct((n, h, dh), F32),
        compiler_params=pltpu.CompilerParams(dimension_semantics=("arbitrary",),
                                             vmem_limit_bytes=48 * 1024 * 1024),
        name="attn_sample",
    )(page_table, q.reshape(n, h, dh), k_new.reshape(n, h, dh), v_new.reshape(n, h, dh), slopes,
      *([cache_k] * n_pages), *([cache_v] * n_pages))
    return out.reshape(n, a)


def _outffn_kernel(x_ref, a_ref, gn_ref, ag_ref, wo_ref, g2_ref, wg_ref, wu_ref, wd_ref, gf_ref, y_ref,
                   *, ff_chunk):
    a = ATTN_WIDTH
    an = _rmsnorm(a_ref[...], ag_ref[...]).astype(BF16)
    mixed = _dot(an, wo_ref[:a, :]) + _dot(gn_ref[...], wo_ref[a:, :])
    x1 = x_ref[...] + mixed
    hf = _rmsnorm(x1, g2_ref[...]).astype(BF16)
    d_ff = wg_ref.shape[1]
    ff = None
    for c in range(0, d_ff, ff_chunk):
        gate = _dot(hf, wg_ref[:, c:c + ff_chunk])
        up = _dot(hf, wu_ref[:, c:c + ff_chunk])
        act = (gate * jax.nn.sigmoid(gate) * up).astype(BF16)
        part = _dot(act, wd_ref[c:c + ff_chunk, :])
        ff = part if ff is None else ff + part
    y_ref[...] = _rmsnorm(x1 + ff, gf_ref[...])


def _outffn(x, a, gn, ag, w_out, g2, w_gate, w_up, w_down, gf, *, tm):
    n, d = x.shape
    d_ff = w_gate.shape[1]
    assert n % tm == 0
    ff_chunk = d_ff // 2 if (d_ff // 2) % LANES == 0 else d_ff
    row = lambda w: pl.BlockSpec((tm, w), lambda i: (i, 0))
    return pl.pallas_call(
        functools.partial(_outffn_kernel, ff_chunk=ff_chunk),
        grid=(n // tm,),
        in_specs=[row(d), row(ATTN_WIDTH), row(SGU_WIDTH), _const_spec((1, ATTN_WIDTH)), _const_spec(w_out.shape),
                  _const_spec((1, d)), _const_spec(w_gate.shape), _const_spec(w_up.shape),
                  _const_spec(w_down.shape), _const_spec((1, d))],
        out_specs=row(d),
        out_shape=jax.ShapeDtypeStruct((n, d), F32),
        compiler_params=pltpu.CompilerParams(dimension_semantics=("arbitrary",),
                                             vmem_limit_bytes=56 * 1024 * 1024),
        name="outffn",
    )(x, a, gn, ag.reshape(1, -1), w_out, g2.reshape(1, -1), w_gate, w_up, w_down, gf.reshape(1, -1))


def kernel(x_prompt, x_sample, cache_k, cache_v, page_table, norm1_g, w_in, attn_out_g, sgu_ln_g, sgu_ln_b,
           sgu_w, sgu_b, sgu_out_g, w_out, norm2_g, w_gate, w_up, w_down, final_g):
    depth = w_in.shape[0]
    assert depth == 1, "single-layer stack"
    l = 0
    bsz, seq, d = x_prompt.shape
    dec_b, dec_seq, _ = x_sample.shape
    assert dec_seq == 1
    xp = x_prompt.reshape(bsz * seq, d)
    xs = x_sample.reshape(dec_b, d)

    wo, wg, wu, wd = (w[l].astype(BF16) for w in (w_out, w_gate, w_up, w_down))

    qp, kp, vp, gnp = _inproj_prompt(xp, norm1_g[l], w_in[l], sgu_ln_g[l], sgu_ln_b[l], sgu_w[l], sgu_b[l],
                                     sgu_out_g[l], tm=512)
    ap = _attn_prompt(qp.reshape(bsz, seq, -1), kp.reshape(bsz, seq, -1), vp.reshape(bsz, seq, -1))
    yp = _outffn(xp, ap.reshape(bsz * seq, -1), gnp, attn_out_g[l], wo, norm2_g[l], wg, wu, wd, final_g, tm=512)

    qs, ks, vs, gns, vns = _inproj_sample(xs, norm1_g[l], w_in[l], sgu_ln_g[l], sgu_ln_b[l], sgu_w[l], sgu_b[l],
                                          sgu_out_g[l])
    a_s = _attn_sample(qs, ks, vs, cache_k[l], cache_v[l], page_table)
    ys = _outffn(xs, a_s, gns, attn_out_g[l], wo, norm2_g[l], wg, wu, wd, final_g, tm=dec_b)

    y_prompt = yp.reshape(bsz, seq, d)
    y_sample = ys.reshape(dec_b, dec_seq, d)
    k_prompt = kp.reshape(depth, bsz, seq, N_HEADS, HEAD_DIM)
    v_prompt = vp.reshape(depth, bsz, seq, N_HEADS, HEAD_DIM)
    k_sample = ks.reshape(depth, dec_b, dec_seq, N_HEADS, HEAD_DIM)
    v_sample = vs.reshape(depth, dec_b, dec_seq, N_HEADS, HEAD_DIM)
    sgu_v_sample = vns.reshape(depth, dec_b, dec_seq, SGU_WIDTH)
    return (y_prompt, y_sample, k_prompt, v_prompt, k_sample, v_sample, sgu_v_sample)
```

```python
import functools
import math

import numpy as np
import jax
import jax.numpy as jnp
from jax import lax
from jax.experimental import pallas as pl
from jax.experimental.pallas import tpu as pltpu

HEAD_DIM = 64
N_HEADS = 8
ATTN_WIDTH = N_HEADS * HEAD_DIM
N_SGU_GROUPS = 8
SGU_GROUP_DIM = 64
SGU_WIDTH = N_SGU_GROUPS * SGU_GROUP_DIM
SGU_CHUNK = 128
MOBA_BLOCK = 256
MOBA_TOPK = 3
RMS_EPS = 1e-6
LN_EPS = 1e-5
NEG_INF = -1e30
LOG2E = 1.4426950408889634

LANES = 128
HEADS_PER_TILE = LANES // HEAD_DIM
N_HEAD_PAIRS = N_HEADS // HEADS_PER_TILE

F32 = jnp.float32
BF16 = jnp.bfloat16

_NT = (((1,), (1,)), ((), ()))
_NN = (((1,), (0,)), ((), ()))


def _dot(a, b, dims=_NN, precision=None):
    return lax.dot_general(a, b, dims, precision=precision, preferred_element_type=F32)


def _rmsnorm(x, g):
    r = lax.rsqrt(jnp.mean(x * x, axis=-1, keepdims=True) + RMS_EPS)
    return (x * r) * g


def _gelu_tanh(x):
    c = math.sqrt(2.0 / math.pi)
    return 0.5 * x * (1.0 + jnp.tanh(c * (x + 0.044715 * (x * x * x))))


def _alibi_slopes(n):
    start = 2.0 ** (-8.0 / n)
    return np.asarray([start ** (i + 1) for i in range(n)], dtype=np.float32)


def _split_weights(w_ref, whi_ref, wlo_ref, n_precise):
    rows = w_ref.shape[0]
    step = 128
    for r in range(0, rows, step):
        w = w_ref[r:r + step, :]
        hi = w.astype(BF16)
        whi_ref[r:r + step, :] = hi
        wlo_ref[r:r + step, :] = (w[:, :n_precise] - hi[:, :n_precise].astype(F32)).astype(BF16)


def _project(x_ref, g1_ref, whi_ref, wlo_ref):
    n_precise = wlo_ref.shape[1]
    h = _rmsnorm(x_ref[...], g1_ref[...])
    h_hi = h.astype(BF16)
    h_lo = (h - h_hi.astype(F32)).astype(BF16)
    w_qk = whi_ref[:, :n_precise]
    z_qk = _dot(h_hi, w_qk) + (_dot(h_lo, w_qk) + _dot(h_hi, wlo_ref[...]))
    z_rest = _dot(h_hi, whi_ref[:, n_precise:])
    return z_qk, z_rest


def _layernorm(x, g, b):
    mu = jnp.mean(x, axis=-1, keepdims=True)
    xc = x - mu
    var = jnp.mean(xc * xc, axis=-1, keepdims=True)
    return (xc * lax.rsqrt(var + LN_EPS)) * g + b


def _inproj_prompt_kernel(x_ref, g1_ref, w_ref, lng_ref, lnb_ref, sw_ref, sbx_ref, sog_ref,
                          q_ref, k_ref, v_ref, gn_ref,
                          whi_ref, wlo_ref, wcat_ref, s_ref):
    tm = x_ref.shape[0]
    a = ATTN_WIDTH

    @pl.when(pl.program_id(0) == 0)
    def _prepare():
        _split_weights(w_ref, whi_ref, wlo_ref, 2 * a)
        t = lax.broadcasted_iota(jnp.int32, (SGU_CHUNK, SGU_CHUNK), 0)
        s = lax.broadcasted_iota(jnp.int32, (SGU_CHUNK, SGU_CHUNK), 1)
        causal = t >= s
        for gp in range(N_SGU_GROUPS // 2):
            w0 = jnp.where(causal, sw_ref[2 * gp], 0.0)
            w1 = jnp.where(causal, sw_ref[2 * gp + 1], 0.0)
            wcat_ref[gp] = jnp.concatenate([w0, w1], axis=1).astype(BF16)

    z_qk, z_rest = _project(x_ref, g1_ref, whi_ref, wlo_ref)
    q_ref[...] = z_qk[:, :a]
    k_ref[...] = z_qk[:, a:]
    v_ref[...] = z_rest[:, :a]
    u = _gelu_tanh(z_rest[:, a:a + SGU_WIDTH])
    gv = _gelu_tanh(z_rest[:, a + SGU_WIDTH:])
    vn = _layernorm(gv, lng_ref[...], lnb_ref[...])

    lane = lax.broadcasted_iota(jnp.int32, (SGU_CHUNK, LANES), 1)
    low = lane < SGU_GROUP_DIM
    for c in range(tm // SGU_CHUNK):
        rows = slice(c * SGU_CHUNK, (c + 1) * SGU_CHUNK)
        for gp in range(N_SGU_GROUPS // 2):
            cols = slice(gp * LANES, (gp + 1) * LANES)
            vp = vn[rows, cols]
            rhs = jnp.concatenate([jnp.where(low, vp, 0.0), jnp.where(low, 0.0, vp)], axis=0).astype(BF16)
            s_ref[rows, cols] = _dot(wcat_ref[gp], rhs) + sbx_ref[:, cols]
    g = u * s_ref[...]
    gn_ref[...] = _rmsnorm(g, sog_ref[...]).astype(gn_ref.dtype)


def _inproj_sample_kernel(x_ref, g1_ref, w_ref, lng_ref, lnb_ref, w00_ref, b0_ref, sog_ref,
                          q_ref, k_ref, v_ref, gn_ref, vn_ref,
                          whi_ref, wlo_ref):
    a = ATTN_WIDTH

    @pl.when(pl.program_id(0) == 0)
    def _prepare():
        _split_weights(w_ref, whi_ref, wlo_ref, 2 * a)

    z_qk, z_rest = _project(x_ref, g1_ref, whi_ref, wlo_ref)
    q_ref[...] = z_qk[:, :a]
    k_ref[...] = z_qk[:, a:]
    v_ref[...] = z_rest[:, :a]
    u = _gelu_tanh(z_rest[:, a:a + SGU_WIDTH])
    gv = _gelu_tanh(z_rest[:, a + SGU_WIDTH:])
    vn = _layernorm(gv, lng_ref[...], lnb_ref[...])
    vn_ref[...] = vn
    g = u * (vn * w00_ref[...] + b0_ref[...])
    gn_ref[...] = _rmsnorm(g, sog_ref[...]).astype(gn_ref.dtype)


def _const_spec(shape):
    zeros = (0,) * len(shape)
    return pl.BlockSpec(shape, lambda *_: zeros, pipeline_mode=pl.Buffered(1))


def _inproj_prompt(x, g1, w_in, ln_g, ln_b, sgu_w, sgu_b, sog, *, tm):
    n, d = x.shape
    a = ATTN_WIDTH
    assert n % tm == 0 and tm % SGU_CHUNK == 0
    sbx = jnp.repeat(sgu_b.T, SGU_GROUP_DIM, axis=1)
    row = lambda w: pl.BlockSpec((tm, w), lambda i: (i, 0))
    out_shape = (jax.ShapeDtypeStruct((n, a), F32),) * 3 + (jax.ShapeDtypeStruct((n, SGU_WIDTH), BF16),)
    return pl.pallas_call(
        _inproj_prompt_kernel,
        grid=(n // tm,),
        in_specs=[row(d), _const_spec((1, d)), _const_spec(w_in.shape), _const_spec((1, SGU_WIDTH)),
                  _const_spec((1, SGU_WIDTH)), _const_spec(sgu_w.shape), _const_spec(sbx.shape),
                  _const_spec((1, SGU_WIDTH))],
        out_specs=(row(a), row(a), row(a), row(SGU_WIDTH)),
        out_shape=out_shape,
        scratch_shapes=[pltpu.VMEM(w_in.shape, BF16), pltpu.VMEM((d, 2 * a), BF16),
                        pltpu.VMEM((N_SGU_GROUPS // 2, SGU_CHUNK, 2 * SGU_CHUNK), BF16),
                        pltpu.VMEM((tm, SGU_WIDTH), F32)],
        compiler_params=pltpu.CompilerParams(dimension_semantics=("arbitrary",),
                                             vmem_limit_bytes=56 * 1024 * 1024),
        name="inproj_prompt",
    )(x, g1.reshape(1, d), w_in, ln_g.reshape(1, -1), ln_b.reshape(1, -1), sgu_w, sbx, sog.reshape(1, -1))


def _inproj_sample(x, g1, w_in, ln_g, ln_b, sgu_w, sgu_b, sog):
    n, d = x.shape
    a = ATTN_WIDTH
    w00 = jnp.repeat(sgu_w[:, 0, 0], SGU_GROUP_DIM).reshape(1, SGU_WIDTH)
    b0 = jnp.repeat(sgu_b[:, 0], SGU_GROUP_DIM).reshape(1, SGU_WIDTH)
    row = lambda w: pl.BlockSpec((n, w), lambda i: (0, 0))
    vec = _const_spec((1, SGU_WIDTH))
    out_shape = (jax.ShapeDtypeStruct((n, a), F32),) * 3 + (jax.ShapeDtypeStruct((n, SGU_WIDTH), BF16),
                                                           jax.ShapeDtypeStruct((n, SGU_WIDTH), F32))
    return pl.pallas_call(
        _inproj_sample_kernel,
        grid=(1,),
        in_specs=[row(d), _const_spec((1, d)), _const_spec(w_in.shape), vec, vec, vec, vec, vec],
        out_specs=(row(a), row(a), row(a), row(SGU_WIDTH), row(SGU_WIDTH)),
        out_shape=out_shape,
        scratch_shapes=[pltpu.VMEM(w_in.shape, BF16), pltpu.VMEM((d, 2 * a), BF16)],
        compiler_params=pltpu.CompilerParams(dimension_semantics=("arbitrary",),
                                             vmem_limit_bytes=56 * 1024 * 1024),
        name="inproj_sample",
    )(x, g1.reshape(1, d), w_in, ln_g.reshape(1, -1), ln_b.reshape(1, -1), w00, b0, sog.reshape(1, -1))


def _block_choice(gate_t, own):
    nb = gate_t.shape[0]
    blk = lax.broadcasted_iota(jnp.int32, gate_t.shape, 0)
    g = jnp.where(blk < own, gate_t, NEG_INF)
    rank = jnp.zeros(gate_t.shape, F32)
    for m in range(nb):
        gm = g[m:m + 1, :]
        rank = rank + jnp.where(blk > m, jnp.where(gm >= g, 1.0, 0.0), jnp.where(gm > g, 1.0, 0.0))
    return jnp.where((rank < MOBA_TOPK) & (blk < own), 1.0, 0.0)


def _attn_prompt_kernel(q_ref, k_ref, v_ref, sl_ref, o_ref,
                        kb_ref, vt_ref, kmean_ref, sel_ref, acc_ref):
    i = pl.program_id(2)
    blk = MOBA_BLOCK
    nb = k_ref.shape[1] // blk
    scale2 = LOG2E / math.sqrt(HEAD_DIM)

    @pl.when(i == 0)
    def _prepare():
        def body(c, carry):
            rows = pl.ds(pl.multiple_of(c * blk, blk), blk)
            kc = k_ref[0, rows, :]
            kb_ref[c] = kc.astype(BF16)
            kmean_ref[pl.ds(c, 1), :] = jnp.mean(kc, axis=0, keepdims=True)
            vt_ref[c] = v_ref[0, rows, :].T.astype(BF16)
            return carry
        lax.fori_loop(0, nb, body, 0)

    q = q_ref[0]
    lane = lax.broadcasted_iota(jnp.int32, q.shape, 1)
    kl = lax.broadcasted_iota(jnp.int32, (blk, blk), 0)
    ql = lax.broadcasted_iota(jnp.int32, (blk, blk), 1)
    klf = kl.astype(F32)

    qs, bias, m0, l0 = [], [], [], []
    for hh in range(HEADS_PER_TILE):
        qh = jnp.where((lane >= hh * HEAD_DIM) & (lane < (hh + 1) * HEAD_DIM), q, 0.0)
        gate_t = _dot(kmean_ref[...], qh, _NT, precision=lax.Precision.HIGHEST)
        sel_ref[hh] = _block_choice(gate_t, i)
        qs.append((qh * scale2).astype(BF16))
        slope2 = sl_ref[0, hh:hh + 1, :]
        bias.append(klf * slope2)
        s = _dot(kb_ref[i], qs[hh], _NT) + bias[hh]
        s = jnp.where(kl <= ql, s, NEG_INF)
        m = jnp.max(s, axis=0, keepdims=True)
        p = jnp.exp2(s - m)
        l0.append(jnp.sum(p, axis=0, keepdims=True))
        m0.append(m)
        head_rows = slice(hh * HEAD_DIM, (hh + 1) * HEAD_DIM)
        acc_ref[hh] = _dot(vt_ref[i, head_rows, :], p.astype(BF16))

    def past_block(j, carry):
        kj = kb_ref[j]
        off = (j - i).astype(F32) * float(blk)
        out = []
        for hh in range(HEADS_PER_TILE):
            m, l = carry[hh]
            slope2 = sl_ref[0, hh:hh + 1, :]
            c = slope2 * off
            chosen = sel_ref[hh, pl.ds(j, 1), :] > 0.5
            s = _dot(kj, qs[hh], _NT) + bias[hh]
            m_new = jnp.where(chosen, jnp.maximum(m, jnp.max(s, axis=0, keepdims=True) + c), m)
            alpha = jnp.exp2(m - m_new)
            shift = jnp.where(chosen, m_new - c, -NEG_INF)
            p = jnp.exp2(s - shift)
            l = alpha * l + jnp.sum(p, axis=0, keepdims=True)
            head_rows = slice(hh * HEAD_DIM, (hh + 1) * HEAD_DIM)
            acc_ref[hh] = alpha * acc_ref[hh] + _dot(vt_ref[j, head_rows, :], p.astype(BF16))
            out.append((m_new, l))
        return tuple(out)

    stats = lax.fori_loop(0, i, past_block, tuple(zip(m0, l0)))
    o_t = jnp.concatenate([acc_ref[hh] / stats[hh][1] for hh in range(HEADS_PER_TILE)], axis=0)
    o_ref[0] = o_t.T


def _attn_prompt(q, k, v):
    b, t, a = q.shape
    blk = MOBA_BLOCK
    assert t % blk == 0 and a == ATTN_WIDTH
    nb = t // blk
    slopes2 = (_alibi_slopes(N_HEADS) * np.float32(LOG2E)).reshape(N_HEAD_PAIRS, HEADS_PER_TILE, 1)
    slopes2 = jnp.asarray(np.broadcast_to(slopes2, (N_HEAD_PAIRS, HEADS_PER_TILE, blk)).copy())
    return pl.pallas_call(
        _attn_prompt_kernel,
        grid=(b, N_HEAD_PAIRS, nb),
        in_specs=[pl.BlockSpec((1, blk, LANES), lambda bi, p, i: (bi, i, p)),
                  pl.BlockSpec((1, t, LANES), lambda bi, p, i: (bi, 0, p)),
                  pl.BlockSpec((1, t, LANES), lambda bi, p, i: (bi, 0, p)),
                  pl.BlockSpec((1, HEADS_PER_TILE, blk), lambda bi, p, i: (p, 0, 0))],
        out_specs=pl.BlockSpec((1, blk, LANES), lambda bi, p, i: (bi, i, p)),
        out_shape=jax.ShapeDtypeStruct((b, t, a), F32),
        scratch_shapes=[pltpu.VMEM((nb, blk, LANES), BF16), pltpu.VMEM((nb, LANES, blk), BF16),
                        pltpu.VMEM((nb, LANES), F32), pltpu.VMEM((HEADS_PER_TILE, nb, blk), F32),
                        pltpu.VMEM((HEADS_PER_TILE, HEAD_DIM, blk), F32)],
        compiler_params=pltpu.CompilerParams(dimension_semantics=("arbitrary", "arbitrary", "arbitrary"),
                                             vmem_limit_bytes=48 * 1024 * 1024),
        name="attn_prompt",
    )(q, k, v, slopes2)


def _attn_sample_kernel(pt_ref, q_ref, kn_ref, vn_ref, sl_ref, *refs, n_pages, page):
    del pt_ref
    k_pages, v_pages, o_ref = refs[:n_pages], refs[n_pages:2 * n_pages], refs[2 * n_pages]
    past = n_pages * page
    pages_per_block = MOBA_BLOCK // page
    nb = past // MOBA_BLOCK
    scale = 1.0 / math.sqrt(HEAD_DIM)

    q_t = q_ref[0]
    q_b = [jnp.broadcast_to(q_t[:, h:h + 1], (HEAD_DIM, page)) for h in range(N_HEADS)]

    def raw_scores(tile):
        return jnp.concatenate([jnp.sum(q_b[h] * tile(h), axis=0, keepdims=True) for h in range(N_HEADS)], axis=0)

    raw = [raw_scores(lambda h, pg=pg: k_pages[pg][0, h]) for pg in range(n_pages)]
    gate = []
    for n in range(nb):
        tot = raw[n * pages_per_block]
        for r in range(1, pages_per_block):
            tot = tot + raw[n * pages_per_block + r]
        gate.append(jnp.sum(tot, axis=1, keepdims=True) * (1.0 / MOBA_BLOCK))
    chosen = []
    for n in range(nb):
        rank = jnp.where(NEG_INF > gate[n], 1.0, 0.0)
        for m in range(nb):
            if m != n:
                ahead = (gate[m] >= gate[n]) if m < n else (gate[m] > gate[n])
                rank = rank + jnp.where(ahead, 1.0, 0.0)
        chosen.append(jnp.where(rank < MOBA_TOPK, 1.0, 0.0))

    slopes = sl_ref[:, :1]
    lane = lax.broadcasted_iota(jnp.int32, (N_HEADS, page), 1)
    dist0 = (past - lane).astype(F32)
    s_pages = []
    for pg in range(n_pages):
        s = raw[pg] * scale - slopes * (dist0 - float(pg * page))
        ok = jnp.broadcast_to(chosen[pg // pages_per_block], s.shape) > 0.5
        s_pages.append(jnp.where(ok, s, NEG_INF))
    kn_t, vn_t = kn_ref[0], vn_ref[0]
    own = raw_scores(lambda h: jnp.broadcast_to(kn_t[:, h:h + 1], (HEAD_DIM, page))) * scale
    s_pages.append(jnp.where(lane == 0, own, NEG_INF))
    m = jnp.max(s_pages[0], axis=1, keepdims=True)
    for s in s_pages[1:]:
        m = jnp.maximum(m, jnp.max(s, axis=1, keepdims=True))
    p_pages = [jnp.exp(s - m) for s in s_pages]
    l = jnp.sum(p_pages[0], axis=1, keepdims=True)
    for p in p_pages[1:]:
        l = l + jnp.sum(p, axis=1, keepdims=True)
    outs = []
    for h in range(N_HEADS):
        acc = jnp.broadcast_to(p_pages[n_pages][h:h + 1, :], (HEAD_DIM, page)) * \
            jnp.broadcast_to(vn_t[:, h:h + 1], (HEAD_DIM, page))
        for pg in range(n_pages):
            acc = acc + jnp.broadcast_to(p_pages[pg][h:h + 1, :], (HEAD_DIM, page)) * v_pages[pg][0, h]
        outs.append(jnp.sum(acc, axis=1, keepdims=True) / l[h:h + 1, :])
    o_ref[0] = jnp.concatenate(outs, axis=1)


def _attn_sample(q, k_new, v_new, cache_k, cache_v, page_table):
    n, a = q.shape
    n_pool, page, h, dh = cache_k.shape
    n_pages = page_table.shape[1]
    assert h * dh == a and MOBA_BLOCK % page == 0 and (n_pages * page) % MOBA_BLOCK == 0
    slopes = jnp.asarray(np.broadcast_to(_alibi_slopes(N_HEADS).reshape(N_HEADS, 1), (N_HEADS, LANES)).copy())
    ck = jnp.transpose(cache_k, (0, 2, 3, 1))
    cv = jnp.transpose(cache_v, (0, 2, 3, 1))
    tok_t = lambda x: jnp.transpose(x.reshape(n, h, dh), (0, 2, 1))
    tok = pl.BlockSpec((1, dh, h), lambda b, pt: (b, 0, 0))
    page_spec = lambda pg: pl.BlockSpec((1, h, dh, page), lambda b, pt: (pt[b, pg], 0, 0, 0))
    grid_spec = pltpu.PrefetchScalarGridSpec(
        num_scalar_prefetch=1,
        grid=(n,),
        in_specs=[tok, tok, tok, pl.BlockSpec((N_HEADS, LANES), lambda b, pt: (0, 0))]
                 + [page_spec(pg) for pg in range(n_pages)] * 2,
        out_specs=tok,
    )
    out = pl.pallas_call(
        functools.partial(_attn_sample_kernel, n_pages=n_pages, page=page),
        grid_spec=grid_spec,
        out_shape=jax.ShapeDtypeStruct((n, dh, h), F32),
        compiler_params=pltpu.CompilerParams(dimension_semantics=("arbitrary",),
                                             vmem_limit_bytes=48 * 1024 * 1024),
        name="attn_sample",
    )(page_table, tok_t(q), tok_t(k_new), tok_t(v_new), slopes, *([ck] * n_pages), *([cv] * n_pages))
    return jnp.transpose(out, (0, 2, 1)).reshape(n, a)


def _outffn_kernel(x_ref, a_ref, gn_ref, ag_ref, wo_ref, g2_ref, wg_ref, wu_ref, wd_ref, gf_ref, y_ref,
                   *, ff_chunk):
    a = ATTN_WIDTH
    an = _rmsnorm(a_ref[...], ag_ref[...]).astype(BF16)
    mixed = _dot(an, wo_ref[:a, :]) + _dot(gn_ref[...], wo_ref[a:, :])
    x1 = x_ref[...] + mixed
    hf = _rmsnorm(x1, g2_ref[...]).astype(BF16)
    d_ff = wg_ref.shape[1]
    ff = None
    for c in range(0, d_ff, ff_chunk):
        gate = _dot(hf, wg_ref[:, c:c + ff_chunk])
        up = _dot(hf, wu_ref[:, c:c + ff_chunk])
        act = (gate * jax.nn.sigmoid(gate) * up).astype(BF16)
        part = _dot(act, wd_ref[c:c + ff_chunk, :])
        ff = part if ff is None else ff + part
    y_ref[...] = _rmsnorm(x1 + ff, gf_ref[...])


def _outffn(x, a, gn, ag, w_out, g2, w_gate, w_up, w_down, gf, *, tm):
    n, d = x.shape
    d_ff = w_gate.shape[1]
    assert n % tm == 0
    ff_chunk = d_ff // 2 if (d_ff // 2) % LANES == 0 else d_ff
    row = lambda w: pl.BlockSpec((tm, w), lambda i: (i, 0))
    return pl.pallas_call(
        functools.partial(_outffn_kernel, ff_chunk=ff_chunk),
        grid=(n // tm,),
        in_specs=[row(d), row(ATTN_WIDTH), row(SGU_WIDTH), _const_spec((1, ATTN_WIDTH)), _const_spec(w_out.shape),
                  _const_spec((1, d)), _const_spec(w_gate.shape), _const_spec(w_up.shape),
                  _const_spec(w_down.shape), _const_spec((1, d))],
        out_specs=row(d),
        out_shape=jax.ShapeDtypeStruct((n, d), F32),
        compiler_params=pltpu.CompilerParams(dimension_semantics=("arbitrary",),
                                             vmem_limit_bytes=56 * 1024 * 1024),
        name="outffn",
    )(x, a, gn, ag.reshape(1, -1), w_out, g2.reshape(1, -1), w_gate, w_up, w_down, gf.reshape(1, -1))


def kernel(x_prompt, x_sample, cache_k, cache_v, page_table, norm1_g, w_in, attn_out_g, sgu_ln_g, sgu_ln_b,
           sgu_w, sgu_b, sgu_out_g, w_out, norm2_g, w_gate, w_up, w_down, final_g):
    depth = w_in.shape[0]
    assert depth == 1, "single-layer stack"
    l = 0
    bsz, seq, d = x_prompt.shape
    dec_b, dec_seq, _ = x_sample.shape
    assert dec_seq == 1
    xp = x_prompt.reshape(bsz * seq, d)
    xs = x_sample.reshape(dec_b, d)

    wo, wg, wu, wd = (w[l].astype(BF16) for w in (w_out, w_gate, w_up, w_down))

    qp, kp, vp, gnp = _inproj_prompt(xp, norm1_g[l], w_in[l], sgu_ln_g[l], sgu_ln_b[l], sgu_w[l], sgu_b[l],
                                     sgu_out_g[l], tm=512)
    ap = _attn_prompt(qp.reshape(bsz, seq, -1), kp.reshape(bsz, seq, -1), vp.reshape(bsz, seq, -1))
    yp = _outffn(xp, ap.reshape(bsz * seq, -1), gnp, attn_out_g[l], wo, norm2_g[l], wg, wu, wd, final_g, tm=512)

    qs, ks, vs, gns, vns = _inproj_sample(xs, norm1_g[l], w_in[l], sgu_ln_g[l], sgu_ln_b[l], sgu_w[l], sgu_b[l],
                                          sgu_out_g[l])
    a_s = _attn_sample(qs, ks, vs, cache_k[l], cache_v[l], page_table)
    ys = _outffn(xs, a_s, gns, attn_out_g[l], wo, norm2_g[l], wg, wu, wd, final_g, tm=dec_b)

    y_prompt = yp.reshape(bsz, seq, d)
    y_sample = ys.reshape(dec_b, dec_seq, d)
    k_prompt = kp.reshape(depth, bsz, seq, N_HEADS, HEAD_DIM)
    v_prompt = vp.reshape(depth, bsz, seq, N_HEADS, HEAD_DIM)
    k_sample = ks.reshape(depth, dec_b, dec_seq, N_HEADS, HEAD_DIM)
    v_sample = vs.reshape(depth, dec_b, dec_seq, N_HEADS, HEAD_DIM)
    sgu_v_sample = vns.reshape(depth, dec_b, dec_seq, SGU_WIDTH)
    return (y_prompt, y_sample, k_prompt, v_prompt, k_sample, v_sample, sgu_v_sample)
```

```python
import functools
import math

import numpy as np
import jax
import jax.numpy as jnp
from jax import lax
from jax.experimental import pallas as pl
from jax.experimental.pallas import tpu as pltpu

HEAD_DIM = 64
N_HEADS = 8
ATTN_WIDTH = N_HEADS * HEAD_DIM
N_SGU_GROUPS = 8
SGU_GROUP_DIM = 64
SGU_WIDTH = N_SGU_GROUPS * SGU_GROUP_DIM
SGU_CHUNK = 128
MOBA_BLOCK = 256
MOBA_TOPK = 3
RMS_EPS = 1e-6
LN_EPS = 1e-5
NEG_INF = -1e30
LOG2E = 1.4426950408889634

LANES = 128
HEADS_PER_TILE = LANES // HEAD_DIM
N_HEAD_PAIRS = N_HEADS // HEADS_PER_TILE

F32 = jnp.float32
BF16 = jnp.bfloat16

_NT = (((1,), (1,)), ((), ()))
_NN = (((1,), (0,)), ((), ()))


def _dot(a, b, dims=_NN, precision=None):
    return lax.dot_general(a, b, dims, precision=precision, preferred_element_type=F32)


def _rmsnorm(x, g):
    r = lax.rsqrt(jnp.mean(x * x, axis=-1, keepdims=True) + RMS_EPS)
    return (x * r) * g


def _gelu_tanh(x):
    c = math.sqrt(2.0 / math.pi)
    return 0.5 * x * (1.0 + jnp.tanh(c * (x + 0.044715 * (x * x * x))))


def _alibi_slopes(n):
    start = 2.0 ** (-8.0 / n)
    return np.asarray([start ** (i + 1) for i in range(n)], dtype=np.float32)


def _split_weights(w_ref, whi_ref, wlo_ref, n_precise):
    rows = w_ref.shape[0]
    step = 128
    for r in range(0, rows, step):
        w = w_ref[r:r + step, :]
        hi = w.astype(BF16)
        whi_ref[r:r + step, :] = hi
        wlo_ref[r:r + step, :] = (w[:, :n_precise] - hi[:, :n_precise].astype(F32)).astype(BF16)


def _project(x, g1_ref, whi_ref, wlo_ref):
    n_precise = wlo_ref.shape[1]
    h = _rmsnorm(x, g1_ref[...])
    h_hi = h.astype(BF16)
    h_lo = (h - h_hi.astype(F32)).astype(BF16)
    w_qk = whi_ref[:, :n_precise]
    z_qk = _dot(h_hi, w_qk) + (_dot(h_lo, w_qk) + _dot(h_hi, wlo_ref[...]))
    z_rest = _dot(h_hi, whi_ref[:, n_precise:])
    return z_qk, z_rest


def _layernorm(x, g, b):
    mu = jnp.mean(x, axis=-1, keepdims=True)
    xc = x - mu
    var = jnp.mean(xc * xc, axis=-1, keepdims=True)
    return (xc * lax.rsqrt(var + LN_EPS)) * g + b


def _inproj_prompt_kernel(x_ref, g1_ref, w_ref, lng_ref, lnb_ref, sw_ref, sbx_ref, sog_ref,
                          q_ref, kb_ref, kt_ref, vt_ref, vtb_ref, kmean_ref, gn_ref,
                          whi_ref, wlo_ref, wcat_ref, s_ref):
    tm = x_ref.shape[1]
    a = ATTN_WIDTH
    blk = MOBA_BLOCK

    @pl.when((pl.program_id(0) == 0) & (pl.program_id(1) == 0))
    def _prepare():
        _split_weights(w_ref, whi_ref, wlo_ref, 2 * a)
        t = lax.broadcasted_iota(jnp.int32, (SGU_CHUNK, SGU_CHUNK), 0)
        s = lax.broadcasted_iota(jnp.int32, (SGU_CHUNK, SGU_CHUNK), 1)
        causal = t >= s
        for gp in range(N_SGU_GROUPS // 2):
            w0 = jnp.where(causal, sw_ref[2 * gp], 0.0)
            w1 = jnp.where(causal, sw_ref[2 * gp + 1], 0.0)
            wcat_ref[gp] = jnp.concatenate([w0, w1], axis=1).astype(BF16)

    z_qk, z_rest = _project(x_ref[0], g1_ref, whi_ref, wlo_ref)
    q_ref[0] = z_qk[:, :a]
    zk = z_qk[:, a:]
    zv = z_rest[:, :a]
    kt_ref[0] = zk.T
    vt = zv.T
    vt_ref[0] = vt
    for r in range(tm // blk):
        rows = slice(r * blk, (r + 1) * blk)
        kb_ref[0, r] = zk[rows, :].astype(BF16)
        vtb_ref[0, r] = vt[:, rows].astype(BF16)
        kmean_ref[r] = jnp.mean(zk[rows, :], axis=0, keepdims=True)

    u = _gelu_tanh(z_rest[:, a:a + SGU_WIDTH])
    gv = _gelu_tanh(z_rest[:, a + SGU_WIDTH:])
    vn = _layernorm(gv, lng_ref[...], lnb_ref[...])

    lane = lax.broadcasted_iota(jnp.int32, (SGU_CHUNK, LANES), 1)
    low = lane < SGU_GROUP_DIM
    for c in range(tm // SGU_CHUNK):
        rows = slice(c * SGU_CHUNK, (c + 1) * SGU_CHUNK)
        for gp in range(N_SGU_GROUPS // 2):
            cols = slice(gp * LANES, (gp + 1) * LANES)
            vp = vn[rows, cols]
            rhs = jnp.concatenate([jnp.where(low, vp, 0.0), jnp.where(low, 0.0, vp)], axis=0).astype(BF16)
            s_ref[rows, cols] = _dot(wcat_ref[gp], rhs) + sbx_ref[:, cols]
    g = u * s_ref[...]
    gn_ref[0] = _rmsnorm(g, sog_ref[...]).astype(gn_ref.dtype)


def _inproj_sample_kernel(x_ref, g1_ref, w_ref, lng_ref, lnb_ref, w00_ref, b0_ref, sog_ref,
                          q_ref, k_ref, v_ref, gn_ref, vn_ref,
                          whi_ref, wlo_ref):
    a = ATTN_WIDTH

    @pl.when(pl.program_id(0) == 0)
    def _prepare():
        _split_weights(w_ref, whi_ref, wlo_ref, 2 * a)

    z_qk, z_rest = _project(x_ref[...], g1_ref, whi_ref, wlo_ref)
    q_ref[...] = z_qk[:, :a]
    k_ref[...] = z_qk[:, a:]
    v_ref[...] = z_rest[:, :a]
    u = _gelu_tanh(z_rest[:, a:a + SGU_WIDTH])
    gv = _gelu_tanh(z_rest[:, a + SGU_WIDTH:])
    vn = _layernorm(gv, lng_ref[...], lnb_ref[...])
    vn_ref[...] = vn
    g = u * (vn * w00_ref[...] + b0_ref[...])
    gn_ref[...] = _rmsnorm(g, sog_ref[...]).astype(gn_ref.dtype)


def _const_spec(shape):
    zeros = (0,) * len(shape)
    return pl.BlockSpec(shape, lambda *_: zeros, pipeline_mode=pl.Buffered(1))


def _inproj_prompt(x, g1, w_in, ln_g, ln_b, sgu_w, sgu_b, sog, *, tm):
    b, t, d = x.shape
    a = ATTN_WIDTH
    blk = MOBA_BLOCK
    assert t % tm == 0 and tm % SGU_CHUNK == 0 and tm % blk == 0
    nb = t // blk
    per = tm // blk
    sbx = jnp.repeat(sgu_b.T, SGU_GROUP_DIM, axis=1)
    row = lambda w: pl.BlockSpec((1, tm, w), lambda bi, i: (bi, i, 0))
    col = lambda w: pl.BlockSpec((1, w, tm), lambda bi, i: (bi, 0, i))
    out_specs = (row(a),
                 pl.BlockSpec((1, per, blk, a), lambda bi, i: (bi, i, 0, 0)),
                 col(a), col(a),
                 pl.BlockSpec((1, per, a, blk), lambda bi, i: (bi, i, 0, 0)),
                 pl.BlockSpec((per, 1, a), lambda bi, i: (bi * (nb // per) + i, 0, 0)),
                 row(SGU_WIDTH))
    out_shape = (jax.ShapeDtypeStruct((b, t, a), F32),
                 jax.ShapeDtypeStruct((b, nb, blk, a), BF16),
                 jax.ShapeDtypeStruct((b, a, t), F32),
                 jax.ShapeDtypeStruct((b, a, t), F32),
                 jax.ShapeDtypeStruct((b, nb, a, blk), BF16),
                 jax.ShapeDtypeStruct((b * nb, 1, a), F32),
                 jax.ShapeDtypeStruct((b, t, SGU_WIDTH), BF16))
    return pl.pallas_call(
        _inproj_prompt_kernel,
        grid=(b, t // tm),
        in_specs=[row(d), _const_spec((1, d)), _const_spec(w_in.shape), _const_spec((1, SGU_WIDTH)),
                  _const_spec((1, SGU_WIDTH)), _const_spec(sgu_w.shape), _const_spec(sbx.shape),
                  _const_spec((1, SGU_WIDTH))],
        out_specs=out_specs,
        out_shape=out_shape,
        scratch_shapes=[pltpu.VMEM(w_in.shape, BF16), pltpu.VMEM((d, 2 * a), BF16),
                        pltpu.VMEM((N_SGU_GROUPS // 2, SGU_CHUNK, 2 * SGU_CHUNK), BF16),
                        pltpu.VMEM((tm, SGU_WIDTH), F32)],
        compiler_params=pltpu.CompilerParams(dimension_semantics=("arbitrary", "arbitrary"),
                                             vmem_limit_bytes=56 * 1024 * 1024),
        name="inproj_prompt",
    )(x, g1.reshape(1, d), w_in, ln_g.reshape(1, -1), ln_b.reshape(1, -1), sgu_w, sbx, sog.reshape(1, -1))


def _inproj_sample(x, g1, w_in, ln_g, ln_b, sgu_w, sgu_b, sog):
    n, d = x.shape
    a = ATTN_WIDTH
    w00 = jnp.repeat(sgu_w[:, 0, 0], SGU_GROUP_DIM).reshape(1, SGU_WIDTH)
    b0 = jnp.repeat(sgu_b[:, 0], SGU_GROUP_DIM).reshape(1, SGU_WIDTH)
    row = lambda w: pl.BlockSpec((n, w), lambda i: (0, 0))
    vec = _const_spec((1, SGU_WIDTH))
    out_shape = (jax.ShapeDtypeStruct((n, a), F32),) * 3 + (jax.ShapeDtypeStruct((n, SGU_WIDTH), BF16),
                                                           jax.ShapeDtypeStruct((n, SGU_WIDTH), F32))
    return pl.pallas_call(
        _inproj_sample_kernel,
        grid=(1,),
        in_specs=[row(d), _const_spec((1, d)), _const_spec(w_in.shape), vec, vec, vec, vec, vec],
        out_specs=(row(a), row(a), row(a), row(SGU_WIDTH), row(SGU_WIDTH)),
        out_shape=out_shape,
        scratch_shapes=[pltpu.VMEM(w_in.shape, BF16), pltpu.VMEM((d, 2 * a), BF16)],
        compiler_params=pltpu.CompilerParams(dimension_semantics=("arbitrary",),
                                             vmem_limit_bytes=56 * 1024 * 1024),
        name="inproj_sample",
    )(x, g1.reshape(1, d), w_in, ln_g.reshape(1, -1), ln_b.reshape(1, -1), w00, b0, sog.reshape(1, -1))


def _block_choice(gate_t, own):
    nb = gate_t.shape[0]
    blk = lax.broadcasted_iota(jnp.int32, gate_t.shape, 0)
    g = jnp.where(blk < own, gate_t, NEG_INF)
    picked = jnp.zeros(gate_t.shape, F32)
    for _ in range(MOBA_TOPK):
        top = jnp.max(g, axis=0, keepdims=True)
        first = jnp.min(jnp.where(g == top, blk, nb), axis=0, keepdims=True)
        hit = blk == first
        picked = jnp.where(hit, 1.0, picked)
        g = jnp.where(hit, -jnp.inf, g)
    return jnp.where(blk < own, picked, 0.0)


def _attn_prompt_kernel(q_ref, kb_ref, vtb_ref, km_ref, sl_ref, o_ref,
                        kmt_ref, bias_ref, qs_ref, sel_ref, acc_ref, m_ref, l_ref, s_ref, cm_ref):
    i = pl.program_id(1)
    blk = MOBA_BLOCK
    nb = kb_ref.shape[1]
    scale2 = LOG2E / math.sqrt(HEAD_DIM)

    @pl.when(i == 0)
    def _prepare():
        km = km_ref[:, 0, :]
        lane_head = lax.broadcasted_iota(jnp.int32, km.shape, 1) // HEAD_DIM
        for h in range(N_HEADS):
            kmt_ref[h * nb:(h + 1) * nb, :] = jnp.where(lane_head == h, km, 0.0)
        klf = lax.broadcasted_iota(jnp.int32, (blk, blk), 0).astype(F32)
        for h in range(N_HEADS):
            bias_ref[h] = klf * sl_ref[h:h + 1, :]

    q = q_ref[0]
    gate_t = _dot(kmt_ref[...], q, _NT, precision=lax.Precision.HIGHEST)
    lane = lax.broadcasted_iota(jnp.int32, (blk, LANES), 1)
    for h in range(N_HEADS):
        sel_ref[h] = _block_choice(gate_t[h * nb:(h + 1) * nb, :], i)
        pair, hh = divmod(h, HEADS_PER_TILE)
        q_pair = q[:, pair * LANES:(pair + 1) * LANES]
        mine = (lane >= hh * HEAD_DIM) & (lane < (hh + 1) * HEAD_DIM)
        qs_ref[h] = (jnp.where(mine, q_pair, 0.0) * scale2).astype(BF16)

    def scores(j, h):
        pair = h // HEADS_PER_TILE
        return _dot(kb_ref[0, j, :, pair * LANES:(pair + 1) * LANES], qs_ref[h], _NT) + bias_ref[h]

    def value_rows(j, h):
        return vtb_ref[0, j, h * HEAD_DIM:(h + 1) * HEAD_DIM, :]

    kl = lax.broadcasted_iota(jnp.int32, (blk, blk), 0)
    ql = lax.broadcasted_iota(jnp.int32, (blk, blk), 1)
    for h in range(N_HEADS):
        s = jnp.where(kl <= ql, scores(i, h), NEG_INF)
        m = jnp.max(s, axis=0, keepdims=True)
        p = jnp.exp2(s - m)
        m_ref[h] = m
        l_ref[h] = jnp.sum(p, axis=0, keepdims=True)
        acc_ref[h] = _dot(value_rows(i, h), p.astype(BF16))

    def issue_scores(j, slot):
        for h in range(N_HEADS):
            s = scores(j, h)
            s_ref[slot, h] = s
            cm_ref[slot, h] = jnp.max(s, axis=0, keepdims=True)

    def finish_block(j, slot):
        off = (j - i).astype(F32) * float(blk)
        for h in range(N_HEADS):
            m, l = m_ref[h], l_ref[h]
            c = sl_ref[h:h + 1, :] * off
            chosen = sel_ref[h, pl.ds(j, 1), :] > 0.5
            m_new = jnp.where(chosen, jnp.maximum(m, cm_ref[slot, h] + c), m)
            alpha = jnp.exp2(m - m_new)
            shift = jnp.where(chosen, m_new - c, -NEG_INF)
            p = jnp.exp2(s_ref[slot, h] - shift)
            m_ref[h] = m_new
            l_ref[h] = alpha * l + jnp.sum(p, axis=0, keepdims=True)
            acc_ref[h] = alpha * acc_ref[h] + _dot(value_rows(j, h), p.astype(BF16))

    issue_scores(0, 0)

    def trip(j, carry):
        slot = j % 2
        finish_block(j, slot)
        issue_scores(jnp.minimum(j + 1, nb - 1), 1 - slot)
        return carry

    lax.fori_loop(0, i, trip, 0)
    o_t = jnp.concatenate([acc_ref[h] / l_ref[h] for h in range(N_HEADS)], axis=0)
    o_ref[0] = o_t.T


def _attn_prompt(q, kb, vtb, kmean):
    b, t, a = q.shape
    blk = MOBA_BLOCK
    nb = t // blk
    assert t % blk == 0 and a == ATTN_WIDTH and kb.shape == (b, nb, blk, a) and vtb.shape == (b, nb, a, blk)
    slopes2 = (_alibi_slopes(N_HEADS) * np.float32(LOG2E)).reshape(N_HEADS, 1)
    slopes2 = jnp.asarray(np.broadcast_to(slopes2, (N_HEADS, blk)).copy())
    once = pl.Buffered(1)
    return pl.pallas_call(
        _attn_prompt_kernel,
        grid=(b, nb),
        in_specs=[pl.BlockSpec((1, blk, a), lambda bi, i: (bi, i, 0)),
                  pl.BlockSpec((1, nb, blk, a), lambda bi, i: (bi, 0, 0, 0), pipeline_mode=once),
                  pl.BlockSpec((1, nb, a, blk), lambda bi, i: (bi, 0, 0, 0), pipeline_mode=once),
                  pl.BlockSpec((nb, 1, a), lambda bi, i: (bi, 0, 0)),
                  pl.BlockSpec((N_HEADS, blk), lambda bi, i: (0, 0))],
        out_specs=pl.BlockSpec((1, blk, a), lambda bi, i: (bi, i, 0)),
        out_shape=jax.ShapeDtypeStruct((b, t, a), F32),
        scratch_shapes=[pltpu.VMEM((N_HEADS * nb, a), F32),
                        pltpu.VMEM((N_HEADS, blk, blk), F32),
                        pltpu.VMEM((N_HEADS, blk, LANES), BF16),
                        pltpu.VMEM((N_HEADS, nb, blk), F32),
                        pltpu.VMEM((N_HEADS, HEAD_DIM, blk), F32),
                        pltpu.VMEM((N_HEADS, 1, blk), F32),
                        pltpu.VMEM((N_HEADS, 1, blk), F32),
                        pltpu.VMEM((2, N_HEADS, blk, blk), F32),
                        pltpu.VMEM((2, N_HEADS, 1, blk), F32)],
        compiler_params=pltpu.CompilerParams(dimension_semantics=("arbitrary", "arbitrary"),
                                             vmem_limit_bytes=48 * 1024 * 1024),
        name="attn_prompt",
    )(q, kb, vtb, kmean, slopes2)


def _attn_sample_kernel(pt_ref, q_ref, kn_ref, vn_ref, sl_ref, *refs, n_pages, page):
    del pt_ref
    k_pages, v_pages, o_ref = refs[:n_pages], refs[n_pages:2 * n_pages], refs[2 * n_pages]
    past = n_pages * page
    pages_per_block = MOBA_BLOCK // page
    nb = past // MOBA_BLOCK
    scale = 1.0 / math.sqrt(HEAD_DIM)

    q_t = q_ref[0]
    q_b = [jnp.broadcast_to(q_t[:, h:h + 1], (HEAD_DIM, page)) for h in range(N_HEADS)]

    def raw_scores(tile):
        return jnp.concatenate([jnp.sum(q_b[h] * tile(h), axis=0, keepdims=True) for h in range(N_HEADS)], axis=0)

    raw = [raw_scores(lambda h, pg=pg: k_pages[pg][0, h]) for pg in range(n_pages)]
    gate = []
    for n in range(nb):
        tot = raw[n * pages_per_block]
        for r in range(1, pages_per_block):
            tot = tot + raw[n * pages_per_block + r]
        gate.append(jnp.sum(tot, axis=1, keepdims=True) * (1.0 / MOBA_BLOCK))
    chosen = []
    for n in range(nb):
        rank = jnp.where(NEG_INF > gate[n], 1.0, 0.0)
        for m in range(nb):
            if m != n:
                ahead = (gate[m] >= gate[n]) if m < n else (gate[m] > gate[n])
                rank = rank + jnp.where(ahead, 1.0, 0.0)
        chosen.append(jnp.where(rank < MOBA_TOPK, 1.0, 0.0))

    slopes = sl_ref[:, :1]
    lane = lax.broadcasted_iota(jnp.int32, (N_HEADS, page), 1)
    dist0 = (past - lane).astype(F32)
    s_pages = []
    for pg in range(n_pages):
        s = raw[pg] * scale - slopes * (dist0 - float(pg * page))
        ok = jnp.broadcast_to(chosen[pg // pages_per_block], s.shape) > 0.5
        s_pages.append(jnp.where(ok, s, NEG_INF))
    kn_t, vn_t = kn_ref[0], vn_ref[0]
    own = raw_scores(lambda h: jnp.broadcast_to(kn_t[:, h:h + 1], (HEAD_DIM, page))) * scale
    s_pages.append(jnp.where(lane == 0, own, NEG_INF))
    m = jnp.max(s_pages[0], axis=1, keepdims=True)
    for s in s_pages[1:]:
        m = jnp.maximum(m, jnp.max(s, axis=1, keepdims=True))
    p_pages = [jnp.exp(s - m) for s in s_pages]
    l = jnp.sum(p_pages[0], axis=1, keepdims=True)
    for p in p_pages[1:]:
        l = l + jnp.sum(p, axis=1, keepdims=True)
    outs = []
    for h in range(N_HEADS):
        acc = jnp.broadcast_to(p_pages[n_pages][h:h + 1, :], (HEAD_DIM, page)) * \
            jnp.broadcast_to(vn_t[:, h:h + 1], (HEAD_DIM, page))
        for pg in range(n_pages):
            acc = acc + jnp.broadcast_to(p_pages[pg][h:h + 1, :], (HEAD_DIM, page)) * v_pages[pg][0, h]
        outs.append(jnp.sum(acc, axis=1, keepdims=True) / l[h:h + 1, :])
    o_ref[0] = jnp.concatenate(outs, axis=1)


def _attn_sample(q, k_new, v_new, cache_k, cache_v, page_table):
    n, a = q.shape
    n_pool, page, h, dh = cache_k.shape
    n_pages = page_table.shape[1]
    assert h * dh == a and MOBA_BLOCK % page == 0 and (n_pages * page) % MOBA_BLOCK == 0
    slopes = jnp.asarray(np.broadcast_to(_alibi_slopes(N_HEADS).reshape(N_HEADS, 1), (N_HEADS, LANES)).copy())
    ck = jnp.transpose(cache_k, (0, 2, 3, 1))
    cv = jnp.transpose(cache_v, (0, 2, 3, 1))
    tok_t = lambda x: jnp.transpose(x.reshape(n, h, dh), (0, 2, 1))
    tok = pl.BlockSpec((1, dh, h), lambda b, pt: (b, 0, 0))
    page_spec = lambda pg: pl.BlockSpec((1, h, dh, page), lambda b, pt: (pt[b, pg], 0, 0, 0))
    grid_spec = pltpu.PrefetchScalarGridSpec(
        num_scalar_prefetch=1,
        grid=(n,),
        in_specs=[tok, tok, tok, pl.BlockSpec((N_HEADS, LANES), lambda b, pt: (0, 0))]
                 + [page_spec(pg) for pg in range(n_pages)] * 2,
        out_specs=tok,
    )
    out = pl.pallas_call(
        functools.partial(_attn_sample_kernel, n_pages=n_pages, page=page),
        grid_spec=grid_spec,
        out_shape=jax.ShapeDtypeStruct((n, dh, h), F32),
        compiler_params=pltpu.CompilerParams(dimension_semantics=("arbitrary",),
                                             vmem_limit_bytes=48 * 1024 * 1024),
        name="attn_sample",
    )(page_table, tok_t(q), tok_t(k_new), tok_t(v_new), slopes, *([ck] * n_pages), *([cv] * n_pages))
    return jnp.transpose(out, (0, 2, 1)).reshape(n, a)


def _outffn_kernel(x_ref, a_ref, gn_ref, ag_ref, wo_ref, g2_ref, wg_ref, wu_ref, wd_ref, gf_ref, y_ref,
                   *, ff_chunk):
    a = ATTN_WIDTH
    an = _rmsnorm(a_ref[...], ag_ref[...]).astype(BF16)
    mixed = _dot(an, wo_ref[:a, :]) + _dot(gn_ref[...], wo_ref[a:, :])
    x1 = x_ref[...] + mixed
    hf = _rmsnorm(x1, g2_ref[...]).astype(BF16)
    d_ff = wg_ref.shape[1]
    ff = None
    for c in range(0, d_ff, ff_chunk):
        gate = _dot(hf, wg_ref[:, c:c + ff_chunk])
        up = _dot(hf, wu_ref[:, c:c + ff_chunk])
        act = (gate * jax.nn.sigmoid(gate) * up).astype(BF16)
        part = _dot(act, wd_ref[c:c + ff_chunk, :])
        ff = part if ff is None else ff + part
    y_ref[...] = _rmsnorm(x1 + ff, gf_ref[...])


def _outffn(x, a, gn, ag, w_out, g2, w_gate, w_up, w_down, gf, *, tm):
    n, d = x.shape
    d_ff = w_gate.shape[1]
    assert n % tm == 0
    ff_chunk = d_ff // 2 if (d_ff // 2) % LANES == 0 else d_ff
    row = lambda w: pl.BlockSpec((tm, w), lambda i: (i, 0))
    return pl.pallas_call(
        functools.partial(_outffn_kernel, ff_chunk=ff_chunk),
        grid=(n // tm,),
        in_specs=[row(d), row(ATTN_WIDTH), row(SGU_WIDTH), _const_spec((1, ATTN_WIDTH)), _const_spec(w_out.shape),
                  _const_spec((1, d)), _const_spec(w_gate.shape), _const_spec(w_up.shape),
                  _const_spec(w_down.shape), _const_spec((1, d))],
        out_specs=row(d),
        out_shape=jax.ShapeDtypeStruct((n, d), F32),
        compiler_params=pltpu.CompilerParams(dimension_semantics=("arbitrary",),
                                             vmem_limit_bytes=56 * 1024 * 1024),
        name="outffn",
    )(x, a, gn, ag.reshape(1, -1), w_out, g2.reshape(1, -1), w_gate, w_up, w_down, gf.reshape(1, -1))


def kernel(x_prompt, x_sample, cache_k, cache_v, page_table, norm1_g, w_in, attn_out_g, sgu_ln_g, sgu_ln_b,
           sgu_w, sgu_b, sgu_out_g, w_out, norm2_g, w_gate, w_up, w_down, final_g):
    depth = w_in.shape[0]
    assert depth == 1, "single-layer stack"
    l = 0
    bsz, seq, d = x_prompt.shape
    dec_b, dec_seq, _ = x_sample.shape
    assert dec_seq == 1
    xs = x_sample.reshape(dec_b, d)

    wo, wg, wu, wd = (w[l].astype(BF16) for w in (w_out, w_gate, w_up, w_down))

    qp, kb, kt, vt, vtb, kmean, gnp = _inproj_prompt(x_prompt, norm1_g[l], w_in[l], sgu_ln_g[l], sgu_ln_b[l],
                                                     sgu_w[l], sgu_b[l], sgu_out_g[l], tm=512)
    ap = _attn_prompt(qp, kb, vtb, kmean)
    yp = _outffn(x_prompt.reshape(bsz * seq, d), ap.reshape(bsz * seq, -1), gnp.reshape(bsz * seq, -1),
                 attn_out_g[l], wo, norm2_g[l], wg, wu, wd, final_g, tm=512)

    qs, ks, vs, gns, vns = _inproj_sample(xs, norm1_g[l], w_in[l], sgu_ln_g[l], sgu_ln_b[l], sgu_w[l], sgu_b[l],
                                          sgu_out_g[l])
    a_s = _attn_sample(qs, ks, vs, cache_k[l], cache_v[l], page_table)
    ys = _outffn(xs, a_s, gns, attn_out_g[l], wo, norm2_g[l], wg, wu, wd, final_g, tm=dec_b)

    heads_last = lambda x_t: jnp.transpose(x_t.reshape(bsz, N_HEADS, HEAD_DIM, seq), (0, 3, 1, 2))[None]
    y_prompt = yp.reshape(bsz, seq, d)
    y_sample = ys.reshape(dec_b, dec_seq, d)
    k_prompt = heads_last(kt)
    v_prompt = heads_last(vt)
    k_sample = ks.reshape(depth, dec_b, dec_seq, N_HEADS, HEAD_DIM)
    v_sample = vs.reshape(depth, dec_b, dec_seq, N_HEADS, HEAD_DIM)
    sgu_v_sample = vns.reshape(depth, dec_b, dec_seq, SGU_WIDTH)
    return (y_prompt, y_sample, k_prompt, v_prompt, k_sample, v_sample, sgu_v_sample)
```

```python
import functools
import math

import numpy as np
import jax
import jax.numpy as jnp
from jax import lax
from jax.experimental import pallas as pl
from jax.experimental.pallas import tpu as pltpu

HEAD_DIM = 64
N_HEADS = 8
ATTN_WIDTH = N_HEADS * HEAD_DIM
N_SGU_GROUPS = 8
SGU_GROUP_DIM = 64
SGU_WIDTH = N_SGU_GROUPS * SGU_GROUP_DIM
SGU_CHUNK = 128
MOBA_BLOCK = 256
MOBA_TOPK = 3
RMS_EPS = 1e-6
LN_EPS = 1e-5
NEG_INF = -1e30
LOG2E = 1.4426950408889634

LANES = 128
BF16_SUBLANES = 16
HEADS_PER_TILE = LANES // HEAD_DIM
N_HEAD_PAIRS = N_HEADS // HEADS_PER_TILE

F32 = jnp.float32
BF16 = jnp.bfloat16

_NT = (((1,), (1,)), ((), ()))
_NN = (((1,), (0,)), ((), ()))


def _dot(a, b, dims=_NN, precision=None):
    return lax.dot_general(a, b, dims, precision=precision, preferred_element_type=F32)


def _rmsnorm(x, g):
    r = lax.rsqrt(jnp.mean(x * x, axis=-1, keepdims=True) + RMS_EPS)
    return (x * r) * g


def _gelu_tanh(x):
    c = math.sqrt(2.0 / math.pi)
    return 0.5 * x * (1.0 + jnp.tanh(c * (x + 0.044715 * (x * x * x))))


def _alibi_slopes(n):
    start = 2.0 ** (-8.0 / n)
    return np.asarray([start ** (i + 1) for i in range(n)], dtype=np.float32)


def _split_weights(w_ref, whi_ref, wlo_ref, n_precise):
    rows = w_ref.shape[0]
    step = 128
    for r in range(0, rows, step):
        w = w_ref[r:r + step, :]
        hi = w.astype(BF16)
        whi_ref[r:r + step, :] = hi
        wlo_ref[r:r + step, :] = (w[:, :n_precise] - hi[:, :n_precise].astype(F32)).astype(BF16)


def _project(x, g1_ref, whi_ref, wlo_ref):
    n_precise = wlo_ref.shape[1]
    h = _rmsnorm(x, g1_ref[...])
    h_hi = h.astype(BF16)
    h_lo = (h - h_hi.astype(F32)).astype(BF16)
    w_qk = whi_ref[:, :n_precise]
    z_qk = _dot(h_hi, w_qk) + (_dot(h_lo, w_qk) + _dot(h_hi, wlo_ref[...]))
    z_rest = _dot(h_hi, whi_ref[:, n_precise:])
    return z_qk, z_rest


def _layernorm(x, g, b):
    mu = jnp.mean(x, axis=-1, keepdims=True)
    xc = x - mu
    var = jnp.mean(xc * xc, axis=-1, keepdims=True)
    return (xc * lax.rsqrt(var + LN_EPS)) * g + b


def _inproj_prompt_kernel(x_ref, g1_ref, w_ref, lng_ref, lnb_ref, sw_ref, sbx_ref, sog_ref,
                          q_ref, kb_ref, kt_ref, vt_ref, vtb_ref, kmean_ref, gn_ref,
                          whi_ref, wlo_ref, wcat_ref, s_ref):
    tm = x_ref.shape[1]
    a = ATTN_WIDTH
    blk = MOBA_BLOCK

    @pl.when((pl.program_id(0) == 0) & (pl.program_id(1) == 0))
    def _prepare():
        _split_weights(w_ref, whi_ref, wlo_ref, 2 * a)
        t = lax.broadcasted_iota(jnp.int32, (SGU_CHUNK, SGU_CHUNK), 0)
        s = lax.broadcasted_iota(jnp.int32, (SGU_CHUNK, SGU_CHUNK), 1)
        causal = t >= s
        for gp in range(N_SGU_GROUPS // 2):
            w0 = jnp.where(causal, sw_ref[2 * gp], 0.0)
            w1 = jnp.where(causal, sw_ref[2 * gp + 1], 0.0)
            wcat_ref[gp] = jnp.concatenate([w0, w1], axis=1).astype(BF16)

    z_qk, z_rest = _project(x_ref[0], g1_ref, whi_ref, wlo_ref)
    q_ref[0] = z_qk[:, :a]
    zk = z_qk[:, a:]
    zv = z_rest[:, :a]
    kt_ref[0] = zk.T
    vt = zv.T
    vt_ref[0] = vt
    for r in range(tm // blk):
        rows = slice(r * blk, (r + 1) * blk)
        kb_ref[0, r] = zk[rows, :].astype(BF16)
        vtb_ref[0, r] = vt[:, rows].astype(BF16)
        kmean_ref[r] = jnp.mean(zk[rows, :], axis=0, keepdims=True)

    u = _gelu_tanh(z_rest[:, a:a + SGU_WIDTH])
    gv = _gelu_tanh(z_rest[:, a + SGU_WIDTH:])
    vn = _layernorm(gv, lng_ref[...], lnb_ref[...])

    lane = lax.broadcasted_iota(jnp.int32, (SGU_CHUNK, LANES), 1)
    low = lane < SGU_GROUP_DIM
    for c in range(tm // SGU_CHUNK):
        rows = slice(c * SGU_CHUNK, (c + 1) * SGU_CHUNK)
        for gp in range(N_SGU_GROUPS // 2):
            cols = slice(gp * LANES, (gp + 1) * LANES)
            vp = vn[rows, cols]
            rhs = jnp.concatenate([jnp.where(low, vp, 0.0), jnp.where(low, 0.0, vp)], axis=0).astype(BF16)
            s_ref[rows, cols] = _dot(wcat_ref[gp], rhs) + sbx_ref[:, cols]
    g = u * s_ref[...]
    gn_ref[0] = _rmsnorm(g, sog_ref[...]).astype(gn_ref.dtype)


def _inproj_sample_kernel(x_ref, g1_ref, w_ref, lng_ref, lnb_ref, w00_ref, b0_ref, sog_ref,
                          q_ref, k_ref, v_ref, gn_ref, vn_ref,
                          whi_ref, wlo_ref):
    a = ATTN_WIDTH

    @pl.when(pl.program_id(0) == 0)
    def _prepare():
        _split_weights(w_ref, whi_ref, wlo_ref, 2 * a)

    z_qk, z_rest = _project(x_ref[...], g1_ref, whi_ref, wlo_ref)
    q_ref[...] = z_qk[:, :a]
    k_ref[...] = z_qk[:, a:]
    v_ref[...] = z_rest[:, :a]
    u = _gelu_tanh(z_rest[:, a:a + SGU_WIDTH])
    gv = _gelu_tanh(z_rest[:, a + SGU_WIDTH:])
    vn = _layernorm(gv, lng_ref[...], lnb_ref[...])
    vn_ref[...] = vn
    g = u * (vn * w00_ref[...] + b0_ref[...])
    gn_ref[...] = _rmsnorm(g, sog_ref[...]).astype(gn_ref.dtype)


def _const_spec(shape):
    zeros = (0,) * len(shape)
    return pl.BlockSpec(shape, lambda *_: zeros, pipeline_mode=pl.Buffered(1))


def _inproj_prompt(x, g1, w_in, ln_g, ln_b, sgu_w, sgu_b, sog, *, tm):
    b, t, d = x.shape
    a = ATTN_WIDTH
    blk = MOBA_BLOCK
    assert t % tm == 0 and tm % SGU_CHUNK == 0 and tm % blk == 0
    nb = t // blk
    per = tm // blk
    sbx = jnp.repeat(sgu_b.T, SGU_GROUP_DIM, axis=1)
    row = lambda w: pl.BlockSpec((1, tm, w), lambda bi, i: (bi, i, 0))
    col = lambda w: pl.BlockSpec((1, w, tm), lambda bi, i: (bi, 0, i))
    out_specs = (row(a),
                 pl.BlockSpec((1, per, blk, a), lambda bi, i: (bi, i, 0, 0)),
                 col(a), col(a),
                 pl.BlockSpec((1, per, a, blk), lambda bi, i: (bi, i, 0, 0)),
                 pl.BlockSpec((per, 1, a), lambda bi, i: (bi * (nb // per) + i, 0, 0)),
                 row(SGU_WIDTH))
    out_shape = (jax.ShapeDtypeStruct((b, t, a), F32),
                 jax.ShapeDtypeStruct((b, nb, blk, a), BF16),
                 jax.ShapeDtypeStruct((b, a, t), F32),
                 jax.ShapeDtypeStruct((b, a, t), F32),
                 jax.ShapeDtypeStruct((b, nb, a, blk), BF16),
                 jax.ShapeDtypeStruct((b * nb, 1, a), F32),
                 jax.ShapeDtypeStruct((b, t, SGU_WIDTH), BF16))
    return pl.pallas_call(
        _inproj_prompt_kernel,
        grid=(b, t // tm),
        in_specs=[row(d), _const_spec((1, d)), _const_spec(w_in.shape), _const_spec((1, SGU_WIDTH)),
                  _const_spec((1, SGU_WIDTH)), _const_spec(sgu_w.shape), _const_spec(sbx.shape),
                  _const_spec((1, SGU_WIDTH))],
        out_specs=out_specs,
        out_shape=out_shape,
        scratch_shapes=[pltpu.VMEM(w_in.shape, BF16), pltpu.VMEM((d, 2 * a), BF16),
                        pltpu.VMEM((N_SGU_GROUPS // 2, SGU_CHUNK, 2 * SGU_CHUNK), BF16),
                        pltpu.VMEM((tm, SGU_WIDTH), F32)],
        compiler_params=pltpu.CompilerParams(dimension_semantics=("arbitrary", "arbitrary"),
                                             vmem_limit_bytes=56 * 1024 * 1024),
        name="inproj_prompt",
    )(x, g1.reshape(1, d), w_in, ln_g.reshape(1, -1), ln_b.reshape(1, -1), sgu_w, sbx, sog.reshape(1, -1))


def _inproj_sample(x, g1, w_in, ln_g, ln_b, sgu_w, sgu_b, sog):
    n, d = x.shape
    a = ATTN_WIDTH
    w00 = jnp.repeat(sgu_w[:, 0, 0], SGU_GROUP_DIM).reshape(1, SGU_WIDTH)
    b0 = jnp.repeat(sgu_b[:, 0], SGU_GROUP_DIM).reshape(1, SGU_WIDTH)
    row = lambda w: pl.BlockSpec((n, w), lambda i: (0, 0))
    vec = _const_spec((1, SGU_WIDTH))
    out_shape = (jax.ShapeDtypeStruct((n, a), F32),) * 3 + (jax.ShapeDtypeStruct((n, SGU_WIDTH), BF16),
                                                           jax.ShapeDtypeStruct((n, SGU_WIDTH), F32))
    return pl.pallas_call(
        _inproj_sample_kernel,
        grid=(1,),
        in_specs=[row(d), _const_spec((1, d)), _const_spec(w_in.shape), vec, vec, vec, vec, vec],
        out_specs=(row(a), row(a), row(a), row(SGU_WIDTH), row(SGU_WIDTH)),
        out_shape=out_shape,
        scratch_shapes=[pltpu.VMEM(w_in.shape, BF16), pltpu.VMEM((d, 2 * a), BF16)],
        compiler_params=pltpu.CompilerParams(dimension_semantics=("arbitrary",),
                                             vmem_limit_bytes=56 * 1024 * 1024),
        name="inproj_sample",
    )(x, g1.reshape(1, d), w_in, ln_g.reshape(1, -1), ln_b.reshape(1, -1), w00, b0, sog.reshape(1, -1))


def _block_choice(gate_t, own):
    nb = gate_t.shape[0]
    blk = lax.broadcasted_iota(jnp.int32, gate_t.shape, 0)
    g = jnp.where(blk < own, gate_t, NEG_INF)
    picked = jnp.zeros(gate_t.shape, F32)
    for _ in range(MOBA_TOPK):
        top = jnp.max(g, axis=0, keepdims=True)
        first = jnp.min(jnp.where(g == top, blk, nb), axis=0, keepdims=True)
        hit = blk == first
        picked = jnp.where(hit, 1.0, picked)
        g = jnp.where(hit, -jnp.inf, g)
    return jnp.where(blk < own, picked, 0.0)


def _split3(x):
    hi = x.astype(BF16)
    r = x - hi.astype(F32)
    mid = r.astype(BF16)
    return hi, mid, (r - mid.astype(F32)).astype(BF16)


def _attn_prompt_kernel(q_ref, kb_ref, vtb_ref, km_ref, sl_ref, o_ref,
                        kmh_ref, kml_ref, kl_ref, qs_ref, sel_ref, acc_ref, m_ref, l_ref, s_ref, cm_ref):
    i = pl.program_id(1)
    blk = MOBA_BLOCK
    nb = kb_ref.shape[1]
    scale2 = LOG2E / math.sqrt(HEAD_DIM)

    @pl.when(i == 0)
    def _prepare():
        km = km_ref[:, 0, :]
        lane_head = lax.broadcasted_iota(jnp.int32, km.shape, 1) // HEAD_DIM
        for h in range(N_HEADS):
            mine = jnp.where(lane_head == h, km, 0.0)
            hi = mine.astype(BF16)
            kmh_ref[h * nb:(h + 1) * nb, :] = hi
            kml_ref[h * nb:(h + 1) * nb, :] = (mine - hi.astype(F32)).astype(BF16)
        lane = lax.broadcasted_iota(jnp.int32, (blk, LANES), 1)
        kl = lax.broadcasted_iota(jnp.int32, (blk, LANES), 0).astype(F32)
        kl_ref[...] = jnp.where(lane < 3, kl, 0.0).astype(BF16)
        row = lax.broadcasted_iota(jnp.int32, (LANES, blk), 0)
        for h in range(N_HEADS):
            hi, mid, lo = (x.astype(F32) for x in _split3(jnp.broadcast_to(sl_ref[h:h + 1, :], (LANES, blk))))
            terms = jnp.where(row == 0, hi, jnp.where(row == 1, mid, jnp.where(row == 2, lo, 0.0)))
            qs_ref[h, LANES:, :] = terms.astype(BF16)

    q = q_ref[0]
    lane = lax.broadcasted_iota(jnp.int32, (blk, LANES), 1)
    for h in range(N_HEADS):
        pair, hh = divmod(h, HEADS_PER_TILE)
        q_pair = q[:, pair * LANES:(pair + 1) * LANES]
        mine = (lane >= hh * HEAD_DIM) & (lane < (hh + 1) * HEAD_DIM)
        qs_ref[h, :LANES, :] = (jnp.where(mine, q_pair, 0.0) * scale2).T.astype(BF16)

    def scores(j, h):
        pair = h // HEADS_PER_TILE
        keys = jnp.concatenate([kb_ref[0, j, :, pair * LANES:(pair + 1) * LANES], kl_ref[...]], axis=1)
        return _dot(keys, qs_ref[h])

    ones_rows = jnp.ones((BF16_SUBLANES, blk), BF16)

    def value_product(j, h, p):
        rows = jnp.concatenate([vtb_ref[0, j, h * HEAD_DIM:(h + 1) * HEAD_DIM, :], ones_rows], axis=0)
        pv = _dot(rows, p.astype(BF16))
        return pv[:HEAD_DIM], pv[HEAD_DIM:HEAD_DIM + 1]

    def issue_scores(j, slot, h):
        s = scores(j, h)
        s_ref[slot, h] = s
        cm_ref[slot, h] = jnp.max(s, axis=0, keepdims=True)

    for h in range(N_HEADS):
        s_ref[1, h] = scores(i, h)
    for h in range(N_HEADS):
        issue_scores(0, 0, h)
    q_hi = q.astype(BF16)
    q_lo = (q - q_hi.astype(F32)).astype(BF16)
    gate_t = _dot(kmh_ref[...], q_hi, _NT) + (_dot(kml_ref[...], q_hi, _NT) + _dot(kmh_ref[...], q_lo, _NT))
    for h in range(N_HEADS):
        sel_ref[h] = _block_choice(gate_t[h * nb:(h + 1) * nb, :], i)

    kl = lax.broadcasted_iota(jnp.int32, (blk, blk), 0)
    ql = lax.broadcasted_iota(jnp.int32, (blk, blk), 1)
    for h in range(N_HEADS):
        s = jnp.where(kl <= ql, s_ref[1, h], NEG_INF)
        m = jnp.max(s, axis=0, keepdims=True)
        m_ref[h] = m
        acc_ref[h], l_ref[h] = value_product(i, h, jnp.exp2(s - m))

    def finish_block(j, slot, h):
        m, l = m_ref[h], l_ref[h]
        c = sl_ref[h:h + 1, :] * ((j - i).astype(F32) * float(blk))
        chosen = sel_ref[h, pl.ds(j, 1), :] > 0.5
        m_new = jnp.where(chosen, jnp.maximum(m, cm_ref[slot, h] + c), m)
        alpha = jnp.exp2(m - m_new)
        shift = jnp.where(chosen, m_new - c, -NEG_INF)
        pv, p_sum = value_product(j, h, jnp.exp2(s_ref[slot, h] - shift))
        m_ref[h] = m_new
        l_ref[h] = alpha * l + p_sum
        acc_ref[h] = alpha * acc_ref[h] + pv

    def trip(t, carry):
        for slot in range(2):
            j = 2 * t + slot
            for h in range(N_HEADS):
                issue_scores(jnp.minimum(j + 1, nb - 1), 1 - slot, h)
                finish_block(j, slot, h)
        return carry

    lax.fori_loop(0, (i + 1) // 2, trip, 0)
    o_t = jnp.concatenate([acc_ref[h] / l_ref[h] for h in range(N_HEADS)], axis=0)
    o_ref[0] = o_t.T


def _attn_prompt(q, kb, vtb, kmean):
    b, t, a = q.shape
    blk = MOBA_BLOCK
    nb = t // blk
    assert t % blk == 0 and a == ATTN_WIDTH and kb.shape == (b, nb, blk, a) and vtb.shape == (b, nb, a, blk)
    slopes2 = (_alibi_slopes(N_HEADS) * np.float32(LOG2E)).reshape(N_HEADS, 1)
    slopes2 = jnp.asarray(np.broadcast_to(slopes2, (N_HEADS, blk)).copy())
    once = pl.Buffered(1)
    return pl.pallas_call(
        _attn_prompt_kernel,
        grid=(b, nb),
        in_specs=[pl.BlockSpec((1, blk, a), lambda bi, i: (bi, i, 0)),
                  pl.BlockSpec((1, nb, blk, a), lambda bi, i: (bi, 0, 0, 0), pipeline_mode=once),
                  pl.BlockSpec((1, nb, a, blk), lambda bi, i: (bi, 0, 0, 0), pipeline_mode=once),
                  pl.BlockSpec((nb, 1, a), lambda bi, i: (bi, 0, 0)),
                  pl.BlockSpec((N_HEADS, blk), lambda bi, i: (0, 0))],
        out_specs=pl.BlockSpec((1, blk, a), lambda bi, i: (bi, i, 0)),
        out_shape=jax.ShapeDtypeStruct((b, t, a), F32),
        scratch_shapes=[pltpu.VMEM((N_HEADS * nb, a), BF16),
                        pltpu.VMEM((N_HEADS * nb, a), BF16),
                        pltpu.VMEM((blk, LANES), BF16),
                        pltpu.VMEM((N_HEADS, 2 * LANES, blk), BF16),
                        pltpu.VMEM((N_HEADS, nb, blk), F32),
                        pltpu.VMEM((N_HEADS, HEAD_DIM, blk), F32),
                        pltpu.VMEM((N_HEADS, 1, blk), F32),
                        pltpu.VMEM((N_HEADS, 1, blk), F32),
                        pltpu.VMEM((2, N_HEADS, blk, blk), F32),
                        pltpu.VMEM((2, N_HEADS, 1, blk), F32)],
        compiler_params=pltpu.CompilerParams(dimension_semantics=("arbitrary", "arbitrary"),
                                             vmem_limit_bytes=48 * 1024 * 1024),
        name="attn_prompt",
    )(q, kb, vtb, kmean, slopes2)


def _attn_sample_kernel(pt_ref, q_ref, kn_ref, vn_ref, sl_ref, *refs, n_pages, page):
    del pt_ref
    k_pages, v_pages, o_ref = refs[:n_pages], refs[n_pages:2 * n_pages], refs[2 * n_pages]
    past = n_pages * page
    pages_per_block = MOBA_BLOCK // page
    nb = past // MOBA_BLOCK
    scale = 1.0 / math.sqrt(HEAD_DIM)

    q_t = q_ref[0]
    q_b = [jnp.broadcast_to(q_t[:, h:h + 1], (HEAD_DIM, page)) for h in range(N_HEADS)]

    def raw_scores(tile):
        return jnp.concatenate([jnp.sum(q_b[h] * tile(h), axis=0, keepdims=True) for h in range(N_HEADS)], axis=0)

    raw = [raw_scores(lambda h, pg=pg: k_pages[pg][0, h]) for pg in range(n_pages)]
    gate = []
    for n in range(nb):
        tot = raw[n * pages_per_block]
        for r in range(1, pages_per_block):
            tot = tot + raw[n * pages_per_block + r]
        gate.append(jnp.sum(tot, axis=1, keepdims=True) * (1.0 / MOBA_BLOCK))
    chosen = []
    for n in range(nb):
        rank = jnp.where(NEG_INF > gate[n], 1.0, 0.0)
        for m in range(nb):
            if m != n:
                ahead = (gate[m] >= gate[n]) if m < n else (gate[m] > gate[n])
                rank = rank + jnp.where(ahead, 1.0, 0.0)
        chosen.append(jnp.where(rank < MOBA_TOPK, 1.0, 0.0))

    slopes = sl_ref[:, :1]
    lane = lax.broadcasted_iota(jnp.int32, (N_HEADS, page), 1)
    dist0 = (past - lane).astype(F32)
    s_pages = []
    for pg in range(n_pages):
        s = raw[pg] * scale - slopes * (dist0 - float(pg * page))
        ok = jnp.broadcast_to(chosen[pg // pages_per_block], s.shape) > 0.5
        s_pages.append(jnp.where(ok, s, NEG_INF))
    kn_t, vn_t = kn_ref[0], vn_ref[0]
    own = raw_scores(lambda h: jnp.broadcast_to(kn_t[:, h:h + 1], (HEAD_DIM, page))) * scale
    s_pages.append(jnp.where(lane == 0, own, NEG_INF))
    m = jnp.max(s_pages[0], axis=1, keepdims=True)
    for s in s_pages[1:]:
        m = jnp.maximum(m, jnp.max(s, axis=1, keepdims=True))
    p_pages = [jnp.exp(s - m) for s in s_pages]
    l = jnp.sum(p_pages[0], axis=1, keepdims=True)
    for p in p_pages[1:]:
        l = l + jnp.sum(p, axis=1, keepdims=True)
    outs = []
    for h in range(N_HEADS):
        acc = jnp.broadcast_to(p_pages[n_pages][h:h + 1, :], (HEAD_DIM, page)) * \
            jnp.broadcast_to(vn_t[:, h:h + 1], (HEAD_DIM, page))
        for pg in range(n_pages):
            acc = acc + jnp.broadcast_to(p_pages[pg][h:h + 1, :], (HEAD_DIM, page)) * v_pages[pg][0, h]
        outs.append(jnp.sum(acc, axis=1, keepdims=True) / l[h:h + 1, :])
    o_ref[0] = jnp.concatenate(outs, axis=1)


def _attn_sample(q, k_new, v_new, cache_k, cache_v, page_table):
    n, a = q.shape
    n_pool, page, h, dh = cache_k.shape
    n_pages = page_table.shape[1]
    assert h * dh == a and MOBA_BLOCK % page == 0 and (n_pages * page) % MOBA_BLOCK == 0
    slopes = jnp.asarray(np.broadcast_to(_alibi_slopes(N_HEADS).reshape(N_HEADS, 1), (N_HEADS, LANES)).copy())
    ck = jnp.transpose(cache_k, (0, 2, 3, 1))
    cv = jnp.transpose(cache_v, (0, 2, 3, 1))
    tok_t = lambda x: jnp.transpose(x.reshape(n, h, dh), (0, 2, 1))
    tok = pl.BlockSpec((1, dh, h), lambda b, pt: (b, 0, 0))
    page_spec = lambda pg: pl.BlockSpec((1, h, dh, page), lambda b, pt: (pt[b, pg], 0, 0, 0))
    grid_spec = pltpu.PrefetchScalarGridSpec(
        num_scalar_prefetch=1,
        grid=(n,),
        in_specs=[tok, tok, tok, pl.BlockSpec((N_HEADS, LANES), lambda b, pt: (0, 0))]
                 + [page_spec(pg) for pg in range(n_pages)] * 2,
        out_specs=tok,
    )
    out = pl.pallas_call(
        functools.partial(_attn_sample_kernel, n_pages=n_pages, page=page),
        grid_spec=grid_spec,
        out_shape=jax.ShapeDtypeStruct((n, dh, h), F32),
        compiler_params=pltpu.CompilerParams(dimension_semantics=("arbitrary",),
                                             vmem_limit_bytes=48 * 1024 * 1024),
        name="attn_sample",
    )(page_table, tok_t(q), tok_t(k_new), tok_t(v_new), slopes, *([ck] * n_pages), *([cv] * n_pages))
    return jnp.transpose(out, (0, 2, 1)).reshape(n, a)


def _outffn_kernel(x_ref, a_ref, gn_ref, ag_ref, wo_ref, g2_ref, wg_ref, wu_ref, wd_ref, gf_ref, y_ref,
                   *, ff_chunk):
    a = ATTN_WIDTH
    an = _rmsnorm(a_ref[...], ag_ref[...]).astype(BF16)
    mixed = _dot(an, wo_ref[:a, :]) + _dot(gn_ref[...], wo_ref[a:, :])
    x1 = x_ref[...] + mixed
    hf = _rmsnorm(x1, g2_ref[...]).astype(BF16)
    d_ff = wg_ref.shape[1]
    ff = None
    for c in range(0, d_ff, ff_chunk):
        gate = _dot(hf, wg_ref[:, c:c + ff_chunk])
        up = _dot(hf, wu_ref[:, c:c + ff_chunk])
        act = (gate * jax.nn.sigmoid(gate) * up).astype(BF16)
        part = _dot(act, wd_ref[c:c + ff_chunk, :])
        ff = part if ff is None else ff + part
    y_ref[...] = _rmsnorm(x1 + ff, gf_ref[...])


def _outffn(x, a, gn, ag, w_out, g2, w_gate, w_up, w_down, gf, *, tm):
    n, d = x.shape
    d_ff = w_gate.shape[1]
    assert n % tm == 0
    ff_chunk = d_ff // 2 if (d_ff // 2) % LANES == 0 else d_ff
    row = lambda w: pl.BlockSpec((tm, w), lambda i: (i, 0))
    return pl.pallas_call(
        functools.partial(_outffn_kernel, ff_chunk=ff_chunk),
        grid=(n // tm,),
        in_specs=[row(d), row(ATTN_WIDTH), row(SGU_WIDTH), _const_spec((1, ATTN_WIDTH)), _const_spec(w_out.shape),
                  _const_spec((1, d)), _const_spec(w_gate.shape), _const_spec(w_up.shape),
                  _const_spec(w_down.shape), _const_spec((1, d))],
        out_specs=row(d),
        out_shape=jax.ShapeDtypeStruct((n, d), F32),
        compiler_params=pltpu.CompilerParams(dimension_semantics=("arbitrary",),
                                             vmem_limit_bytes=56 * 1024 * 1024),
        name="outffn",
    )(x, a, gn, ag.reshape(1, -1), w_out, g2.reshape(1, -1), w_gate, w_up, w_down, gf.reshape(1, -1))


def kernel(x_prompt, x_sample, cache_k, cache_v, page_table, norm1_g, w_in, attn_out_g, sgu_ln_g, sgu_ln_b,
           sgu_w, sgu_b, sgu_out_g, w_out, norm2_g, w_gate, w_up, w_down, final_g):
    depth = w_in.shape[0]
    assert depth == 1, "single-layer stack"
    l = 0
    bsz, seq, d = x_prompt.shape
    dec_b, dec_seq, _ = x_sample.shape
    assert dec_seq == 1
    xs = x_sample.reshape(dec_b, d)

    wo, wg, wu, wd = (w[l].astype(BF16) for w in (w_out, w_gate, w_up, w_down))

    qp, kb, kt, vt, vtb, kmean, gnp = _inproj_prompt(x_prompt, norm1_g[l], w_in[l], sgu_ln_g[l], sgu_ln_b[l],
                                                     sgu_w[l], sgu_b[l], sgu_out_g[l], tm=512)
    ap = _attn_prompt(qp, kb, vtb, kmean)
    yp = _outffn(x_prompt.reshape(bsz * seq, d), ap.reshape(bsz * seq, -1), gnp.reshape(bsz * seq, -1),
                 attn_out_g[l], wo, norm2_g[l], wg, wu, wd, final_g, tm=512)

    qs, ks, vs, gns, vns = _inproj_sample(xs, norm1_g[l], w_in[l], sgu_ln_g[l], sgu_ln_b[l], sgu_w[l], sgu_b[l],
                                          sgu_out_g[l])
    a_s = _attn_sample(qs, ks, vs, cache_k[l], cache_v[l], page_table)
    ys = _outffn(xs, a_s, gns, attn_out_g[l], wo, norm2_g[l], wg, wu, wd, final_g, tm=dec_b)

    heads_last = lambda x_t: jnp.transpose(x_t.reshape(bsz, N_HEADS, HEAD_DIM, seq), (0, 3, 1, 2))[None]
    y_prompt = yp.reshape(bsz, seq, d)
    y_sample = ys.reshape(dec_b, dec_seq, d)
    k_prompt = heads_last(kt)
    v_prompt = heads_last(vt)
    k_sample = ks.reshape(depth, dec_b, dec_seq, N_HEADS, HEAD_DIM)
    v_sample = vs.reshape(depth, dec_b, dec_seq, N_HEADS, HEAD_DIM)
    sgu_v_sample = vns.reshape(depth, dec_b, dec_seq, SGU_WIDTH)
    return (y_prompt, y_sample, k_prompt, v_prompt, k_sample, v_sample, sgu_v_sample)
```

```python
import functools
import math

import numpy as np
import jax
import jax.numpy as jnp
from jax import lax
from jax.experimental import pallas as pl
from jax.experimental.pallas import tpu as pltpu

HEAD_DIM = 64
N_HEADS = 8
ATTN_WIDTH = N_HEADS * HEAD_DIM
N_SGU_GROUPS = 8
SGU_GROUP_DIM = 64
SGU_WIDTH = N_SGU_GROUPS * SGU_GROUP_DIM
SGU_CHUNK = 128
MOBA_BLOCK = 256
MOBA_TOPK = 3
RMS_EPS = 1e-6
LN_EPS = 1e-5
NEG_INF = -1e30
LOG2E = 1.4426950408889634

LANES = 128
BF16_SUBLANES = 16
HEADS_PER_TILE = LANES // HEAD_DIM
N_HEAD_PAIRS = N_HEADS // HEADS_PER_TILE

F32 = jnp.float32
BF16 = jnp.bfloat16

_NT = (((1,), (1,)), ((), ()))
_NN = (((1,), (0,)), ((), ()))


def _dot(a, b, dims=_NN, precision=None):
    return lax.dot_general(a, b, dims, precision=precision, preferred_element_type=F32)


def _rmsnorm(x, g):
    r = lax.rsqrt(jnp.mean(x * x, axis=-1, keepdims=True) + RMS_EPS)
    return (x * r) * g


def _gelu_tanh(x):
    c = math.sqrt(2.0 / math.pi)
    return 0.5 * x * (1.0 + jnp.tanh(c * (x + 0.044715 * (x * x * x))))


def _alibi_slopes(n):
    start = 2.0 ** (-8.0 / n)
    return np.asarray([start ** (i + 1) for i in range(n)], dtype=np.float32)


def _split_weights(w_ref, whi_ref, wlo_ref, n_precise):
    rows = w_ref.shape[0]
    step = 128
    for r in range(0, rows, step):
        w = w_ref[r:r + step, :]
        hi = w.astype(BF16)
        whi_ref[r:r + step, :] = hi
        wlo_ref[r:r + step, :] = (w[:, :n_precise] - hi[:, :n_precise].astype(F32)).astype(BF16)


def _project(x, g1_ref, whi_ref, wlo_ref):
    n_precise = wlo_ref.shape[1]
    h = _rmsnorm(x, g1_ref[...])
    h_hi = h.astype(BF16)
    h_lo = (h - h_hi.astype(F32)).astype(BF16)
    w_qk = whi_ref[:, :n_precise]
    z_qk = _dot(h_hi, w_qk) + (_dot(h_lo, w_qk) + _dot(h_hi, wlo_ref[...]))
    z_rest = _dot(h_hi, whi_ref[:, n_precise:])
    return z_qk, z_rest


def _layernorm(x, g, b):
    mu = jnp.mean(x, axis=-1, keepdims=True)
    xc = x - mu
    var = jnp.mean(xc * xc, axis=-1, keepdims=True)
    return (xc * lax.rsqrt(var + LN_EPS)) * g + b


def _inproj_prompt_kernel(x_ref, g1_ref, w_ref, lng_ref, lnb_ref, sw_ref, sbx_ref, sog_ref,
                          q_ref, kb_ref, kt_ref, vt_ref, vtb_ref, kmean_ref, gn_ref,
                          wb_ref, wcat_ref, s_ref):
    tm = x_ref.shape[1]
    a = ATTN_WIDTH
    blk = MOBA_BLOCK

    @pl.when((pl.program_id(0) == 0) & (pl.program_id(1) == 0))
    def _prepare():
        for r in range(0, w_ref.shape[0], LANES):
            wb_ref[r:r + LANES, :] = w_ref[r:r + LANES, :].astype(BF16)
        t = lax.broadcasted_iota(jnp.int32, (SGU_CHUNK, SGU_CHUNK), 0)
        s = lax.broadcasted_iota(jnp.int32, (SGU_CHUNK, SGU_CHUNK), 1)
        causal = t >= s
        for gp in range(N_SGU_GROUPS // 2):
            w0 = jnp.where(causal, sw_ref[2 * gp], 0.0)
            w1 = jnp.where(causal, sw_ref[2 * gp + 1], 0.0)
            wcat_ref[gp] = jnp.concatenate([w0, w1], axis=1).astype(BF16)

    z = _dot(_rmsnorm(x_ref[0], g1_ref[...]).astype(BF16), wb_ref[...])
    q_ref[0] = z[:, :a].T
    zk = z[:, a:2 * a]
    zv = z[:, 2 * a:3 * a]
    z_rest = z[:, 2 * a:]
    kt_ref[0] = zk.T
    vt = zv.T
    vt_ref[0] = vt
    for r in range(tm // blk):
        rows = slice(r * blk, (r + 1) * blk)
        kb_ref[0, r] = zk[rows, :].astype(BF16)
        vtb_ref[0, r] = vt[:, rows].astype(BF16)
        kmean_ref[r] = jnp.mean(zk[rows, :], axis=0, keepdims=True)

    u = _gelu_tanh(z_rest[:, a:a + SGU_WIDTH])
    gv = _gelu_tanh(z_rest[:, a + SGU_WIDTH:])
    vn = _layernorm(gv, lng_ref[...], lnb_ref[...])

    lane = lax.broadcasted_iota(jnp.int32, (SGU_CHUNK, LANES), 1)
    low = lane < SGU_GROUP_DIM
    for c in range(tm // SGU_CHUNK):
        rows = slice(c * SGU_CHUNK, (c + 1) * SGU_CHUNK)
        for gp in range(N_SGU_GROUPS // 2):
            cols = slice(gp * LANES, (gp + 1) * LANES)
            vp = vn[rows, cols]
            rhs = jnp.concatenate([jnp.where(low, vp, 0.0), jnp.where(low, 0.0, vp)], axis=0).astype(BF16)
            s_ref[rows, cols] = _dot(wcat_ref[gp], rhs) + sbx_ref[:, cols]
    g = u * s_ref[...]
    gn_ref[0] = _rmsnorm(g, sog_ref[...]).astype(gn_ref.dtype)


def _inproj_sample_kernel(x_ref, g1_ref, w_ref, lng_ref, lnb_ref, w00_ref, b0_ref, sog_ref,
                          q_ref, k_ref, v_ref, gn_ref, vn_ref,
                          whi_ref, wlo_ref):
    a = ATTN_WIDTH

    @pl.when(pl.program_id(0) == 0)
    def _prepare():
        _split_weights(w_ref, whi_ref, wlo_ref, 2 * a)

    z_qk, z_rest = _project(x_ref[...], g1_ref, whi_ref, wlo_ref)
    q_ref[...] = z_qk[:, :a]
    k_ref[...] = z_qk[:, a:]
    v_ref[...] = z_rest[:, :a]
    u = _gelu_tanh(z_rest[:, a:a + SGU_WIDTH])
    gv = _gelu_tanh(z_rest[:, a + SGU_WIDTH:])
    vn = _layernorm(gv, lng_ref[...], lnb_ref[...])
    vn_ref[...] = vn
    g = u * (vn * w00_ref[...] + b0_ref[...])
    gn_ref[...] = _rmsnorm(g, sog_ref[...]).astype(gn_ref.dtype)


def _const_spec(shape):
    zeros = (0,) * len(shape)
    return pl.BlockSpec(shape, lambda *_: zeros, pipeline_mode=pl.Buffered(1))


def _inproj_prompt(x, g1, w_in, ln_g, ln_b, sgu_w, sgu_b, sog, *, tm):
    b, t, d = x.shape
    a = ATTN_WIDTH
    blk = MOBA_BLOCK
    assert t % tm == 0 and tm % SGU_CHUNK == 0 and tm % blk == 0
    nb = t // blk
    per = tm // blk
    sbx = jnp.repeat(sgu_b.T, SGU_GROUP_DIM, axis=1)
    row = lambda w: pl.BlockSpec((1, tm, w), lambda bi, i: (bi, i, 0))
    col = lambda w: pl.BlockSpec((1, w, tm), lambda bi, i: (bi, 0, i))
    out_specs = (col(a),
                 pl.BlockSpec((1, per, blk, a), lambda bi, i: (bi, i, 0, 0)),
                 col(a), col(a),
                 pl.BlockSpec((1, per, a, blk), lambda bi, i: (bi, i, 0, 0)),
                 pl.BlockSpec((per, 1, a), lambda bi, i: (bi * (nb // per) + i, 0, 0)),
                 row(SGU_WIDTH))
    out_shape = (jax.ShapeDtypeStruct((b, a, t), F32),
                 jax.ShapeDtypeStruct((b, nb, blk, a), BF16),
                 jax.ShapeDtypeStruct((b, a, t), F32),
                 jax.ShapeDtypeStruct((b, a, t), F32),
                 jax.ShapeDtypeStruct((b, nb, a, blk), BF16),
                 jax.ShapeDtypeStruct((b * nb, 1, a), F32),
                 jax.ShapeDtypeStruct((b, t, SGU_WIDTH), BF16))
    return pl.pallas_call(
        _inproj_prompt_kernel,
        grid=(b, t // tm),
        in_specs=[row(d), _const_spec((1, d)), _const_spec(w_in.shape), _const_spec((1, SGU_WIDTH)),
                  _const_spec((1, SGU_WIDTH)), _const_spec(sgu_w.shape), _const_spec(sbx.shape),
                  _const_spec((1, SGU_WIDTH))],
        out_specs=out_specs,
        out_shape=out_shape,
        scratch_shapes=[pltpu.VMEM(w_in.shape, BF16),
                        pltpu.VMEM((N_SGU_GROUPS // 2, SGU_CHUNK, 2 * SGU_CHUNK), BF16),
                        pltpu.VMEM((tm, SGU_WIDTH), F32)],
        compiler_params=pltpu.CompilerParams(dimension_semantics=("arbitrary", "arbitrary"),
                                             vmem_limit_bytes=56 * 1024 * 1024),
        name="inproj_prompt",
    )(x, g1.reshape(1, d), w_in, ln_g.reshape(1, -1), ln_b.reshape(1, -1), sgu_w, sbx, sog.reshape(1, -1))


def _inproj_sample(x, g1, w_in, ln_g, ln_b, sgu_w, sgu_b, sog):
    n, d = x.shape
    a = ATTN_WIDTH
    w00 = jnp.repeat(sgu_w[:, 0, 0], SGU_GROUP_DIM).reshape(1, SGU_WIDTH)
    b0 = jnp.repeat(sgu_b[:, 0], SGU_GROUP_DIM).reshape(1, SGU_WIDTH)
    row = lambda w: pl.BlockSpec((n, w), lambda i: (0, 0))
    vec = _const_spec((1, SGU_WIDTH))
    out_shape = (jax.ShapeDtypeStruct((n, a), F32),) * 3 + (jax.ShapeDtypeStruct((n, SGU_WIDTH), BF16),
                                                           jax.ShapeDtypeStruct((n, SGU_WIDTH), F32))
    return pl.pallas_call(
        _inproj_sample_kernel,
        grid=(1,),
        in_specs=[row(d), _const_spec((1, d)), _const_spec(w_in.shape), vec, vec, vec, vec, vec],
        out_specs=(row(a), row(a), row(a), row(SGU_WIDTH), row(SGU_WIDTH)),
        out_shape=out_shape,
        scratch_shapes=[pltpu.VMEM(w_in.shape, BF16), pltpu.VMEM((d, 2 * a), BF16)],
        compiler_params=pltpu.CompilerParams(dimension_semantics=("arbitrary",),
                                             vmem_limit_bytes=56 * 1024 * 1024),
        name="inproj_sample",
    )(x, g1.reshape(1, d), w_in, ln_g.reshape(1, -1), ln_b.reshape(1, -1), w00, b0, sog.reshape(1, -1))


def _block_choice(gate_t, own):
    nb = gate_t.shape[0]
    blk = lax.broadcasted_iota(jnp.int32, gate_t.shape, 0)
    g = jnp.where(blk < own, gate_t, NEG_INF)
    picked = jnp.zeros(gate_t.shape, F32)
    for _ in range(MOBA_TOPK):
        top = jnp.max(g, axis=0, keepdims=True)
        first = jnp.min(jnp.where(g == top, blk, nb), axis=0, keepdims=True)
        hit = blk == first
        picked = jnp.where(hit, 1.0, picked)
        g = jnp.where(hit, -jnp.inf, g)
    return jnp.where(blk < own, picked, 0.0)


def _split3(x):
    hi = x.astype(BF16)
    r = x - hi.astype(F32)
    mid = r.astype(BF16)
    return hi, mid, (r - mid.astype(F32)).astype(BF16)


def _attn_prompt_kernel(q_ref, kb_ref, vtb_ref, km_ref, sl_ref, o_ref,
                        kmh_ref, kml_ref, kl_ref, qs_ref, sel_ref, acc_ref, m_ref, l_ref, s_ref, cm_ref):
    i = pl.program_id(1)
    blk = MOBA_BLOCK
    nb = kb_ref.shape[1]
    scale2 = LOG2E / math.sqrt(HEAD_DIM)

    @pl.when(i == 0)
    def _prepare():
        km = km_ref[:, 0, :]
        lane_head = lax.broadcasted_iota(jnp.int32, km.shape, 1) // HEAD_DIM
        for h in range(N_HEADS):
            mine = jnp.where(lane_head == h, km, 0.0)
            hi = mine.astype(BF16)
            kmh_ref[h * nb:(h + 1) * nb, :] = hi
            kml_ref[h * nb:(h + 1) * nb, :] = (mine - hi.astype(F32)).astype(BF16)
        lane = lax.broadcasted_iota(jnp.int32, (blk, LANES), 1)
        kl = lax.broadcasted_iota(jnp.int32, (blk, LANES), 0).astype(F32)
        kl_ref[...] = jnp.where(lane < 3, kl, 0.0).astype(BF16)
        row = lax.broadcasted_iota(jnp.int32, (LANES, blk), 0)
        for h in range(N_HEADS):
            hi, mid, lo = (x.astype(F32) for x in _split3(jnp.broadcast_to(sl_ref[h:h + 1, :], (LANES, blk))))
            terms = jnp.where(row == 0, hi, jnp.where(row == 1, mid, jnp.where(row == 2, lo, 0.0)))
            qs_ref[h, LANES:, :] = terms.astype(BF16)

    q_t = q_ref[0]
    q_hi = q_t.astype(BF16)
    q_lo = (q_t - q_hi.astype(F32)).astype(BF16)
    gate_t = _dot(kmh_ref[...], q_hi) + (_dot(kml_ref[...], q_hi) + _dot(kmh_ref[...], q_lo))
    row = lax.broadcasted_iota(jnp.int32, (LANES, blk), 0)
    for h in range(N_HEADS):
        pair, hh = divmod(h, HEADS_PER_TILE)
        q_pair = q_t[pair * LANES:(pair + 1) * LANES, :]
        mine = (row >= hh * HEAD_DIM) & (row < (hh + 1) * HEAD_DIM)
        qs_ref[h, :LANES, :] = (jnp.where(mine, q_pair, 0.0) * scale2).astype(BF16)

    def scores(j, h):
        pair = h // HEADS_PER_TILE
        keys = jnp.concatenate([kb_ref[0, j, :, pair * LANES:(pair + 1) * LANES], kl_ref[...]], axis=1)
        return _dot(keys, qs_ref[h])

    ones_rows = jnp.ones((BF16_SUBLANES, blk), BF16)

    def value_product(j, h, p):
        rows = jnp.concatenate([vtb_ref[0, j, h * HEAD_DIM:(h + 1) * HEAD_DIM, :], ones_rows], axis=0)
        pv = _dot(rows, p.astype(BF16))
        return pv[:HEAD_DIM], pv[HEAD_DIM:HEAD_DIM + 1]

    def issue_scores(j, slot, h):
        s = scores(j, h)
        s_ref[slot, h] = s
        cm_ref[slot, h] = jnp.max(s, axis=0, keepdims=True)

    for h in range(N_HEADS):
        s_ref[1, h] = scores(i, h)
    for h in range(N_HEADS):
        issue_scores(0, 0, h)
    for h in range(N_HEADS):
        sel_ref[h] = _block_choice(gate_t[h * nb:(h + 1) * nb, :], i)

    kl = lax.broadcasted_iota(jnp.int32, (blk, blk), 0)
    ql = lax.broadcasted_iota(jnp.int32, (blk, blk), 1)
    for h in range(N_HEADS):
        s = jnp.where(kl <= ql, s_ref[1, h], NEG_INF)
        m = jnp.max(s, axis=0, keepdims=True)
        m_ref[h] = m
        acc_ref[h], l_ref[h] = value_product(i, h, jnp.exp2(s - m))

    def finish_block(j, slot, h):
        m, l = m_ref[h], l_ref[h]
        c = sl_ref[h:h + 1, :] * ((j - i).astype(F32) * float(blk))
        chosen = sel_ref[h, pl.ds(j, 1), :] > 0.5
        m_new = jnp.where(chosen, jnp.maximum(m, cm_ref[slot, h] + c), m)
        alpha = jnp.exp2(m - m_new)
        shift = jnp.where(chosen, m_new - c, -NEG_INF)
        pv, p_sum = value_product(j, h, jnp.exp2(s_ref[slot, h] - shift))
        m_ref[h] = m_new
        l_ref[h] = alpha * l + p_sum
        acc_ref[h] = alpha * acc_ref[h] + pv

    def trip(t, carry):
        for slot in range(2):
            j = 2 * t + slot
            for h in range(N_HEADS):
                issue_scores(jnp.minimum(j + 1, nb - 1), 1 - slot, h)
                finish_block(j, slot, h)
        return carry

    lax.fori_loop(0, (i + 1) // 2, trip, 0)
    for h in range(N_HEADS):
        o_ref[0, h * HEAD_DIM:(h + 1) * HEAD_DIM, :] = acc_ref[h] / l_ref[h]


def _attn_prompt(q_t, kb, vtb, kmean):
    b, a, t = q_t.shape
    blk = MOBA_BLOCK
    nb = t // blk
    assert t % blk == 0 and a == ATTN_WIDTH and kb.shape == (b, nb, blk, a) and vtb.shape == (b, nb, a, blk)
    slopes2 = (_alibi_slopes(N_HEADS) * np.float32(LOG2E)).reshape(N_HEADS, 1)
    slopes2 = jnp.asarray(np.broadcast_to(slopes2, (N_HEADS, blk)).copy())
    once = pl.Buffered(1)
    return pl.pallas_call(
        _attn_prompt_kernel,
        grid=(b, nb),
        in_specs=[pl.BlockSpec((1, a, blk), lambda bi, i: (bi, 0, i)),
                  pl.BlockSpec((1, nb, blk, a), lambda bi, i: (bi, 0, 0, 0), pipeline_mode=once),
                  pl.BlockSpec((1, nb, a, blk), lambda bi, i: (bi, 0, 0, 0), pipeline_mode=once),
                  pl.BlockSpec((nb, 1, a), lambda bi, i: (bi, 0, 0)),
                  pl.BlockSpec((N_HEADS, blk), lambda bi, i: (0, 0))],
        out_specs=pl.BlockSpec((1, a, blk), lambda bi, i: (bi, 0, i)),
        out_shape=jax.ShapeDtypeStruct((b, a, t), F32),
        scratch_shapes=[pltpu.VMEM((N_HEADS * nb, a), BF16),
                        pltpu.VMEM((N_HEADS * nb, a), BF16),
                        pltpu.VMEM((blk, LANES), BF16),
                        pltpu.VMEM((N_HEADS, 2 * LANES, blk), BF16),
                        pltpu.VMEM((N_HEADS, nb, blk), F32),
                        pltpu.VMEM((N_HEADS, HEAD_DIM, blk), F32),
                        pltpu.VMEM((N_HEADS, 1, blk), F32),
                        pltpu.VMEM((N_HEADS, 1, blk), F32),
                        pltpu.VMEM((2, N_HEADS, blk, blk), F32),
                        pltpu.VMEM((2, N_HEADS, 1, blk), F32)],
        compiler_params=pltpu.CompilerParams(dimension_semantics=("arbitrary", "arbitrary"),
                                             vmem_limit_bytes=48 * 1024 * 1024),
        name="attn_prompt",
    )(q_t, kb, vtb, kmean, slopes2)


def _attn_sample_kernel(pt_ref, q_ref, kn_ref, vn_ref, sl_ref, *refs, n_pages, page):
    del pt_ref
    k_pages, v_pages, o_ref = refs[:n_pages], refs[n_pages:2 * n_pages], refs[2 * n_pages]
    past = n_pages * page
    pages_per_block = MOBA_BLOCK // page
    nb = past // MOBA_BLOCK
    scale = 1.0 / math.sqrt(HEAD_DIM)

    q_t = q_ref[0]
    q_b = [jnp.broadcast_to(q_t[:, h:h + 1], (HEAD_DIM, page)) for h in range(N_HEADS)]

    def raw_scores(tile):
        return jnp.concatenate([jnp.sum(q_b[h] * tile(h), axis=0, keepdims=True) for h in range(N_HEADS)], axis=0)

    raw = [raw_scores(lambda h, pg=pg: k_pages[pg][0, h]) for pg in range(n_pages)]
    gate = []
    for n in range(nb):
        tot = raw[n * pages_per_block]
        for r in range(1, pages_per_block):
            tot = tot + raw[n * pages_per_block + r]
        gate.append(jnp.sum(tot, axis=1, keepdims=True) * (1.0 / MOBA_BLOCK))
    chosen = []
    for n in range(nb):
        rank = jnp.where(NEG_INF > gate[n], 1.0, 0.0)
        for m in range(nb):
            if m != n:
                ahead = (gate[m] >= gate[n]) if m < n else (gate[m] > gate[n])
                rank = rank + jnp.where(ahead, 1.0, 0.0)
        chosen.append(jnp.where(rank < MOBA_TOPK, 1.0, 0.0))

    slopes = sl_ref[:, :1]
    lane = lax.broadcasted_iota(jnp.int32, (N_HEADS, page), 1)
    dist0 = (past - lane).astype(F32)
    s_pages = []
    for pg in range(n_pages):
        s = raw[pg] * scale - slopes * (dist0 - float(pg * page))
        ok = jnp.broadcast_to(chosen[pg // pages_per_block], s.shape) > 0.5
        s_pages.append(jnp.where(ok, s, NEG_INF))
    kn_t, vn_t = kn_ref[0], vn_ref[0]
    own = raw_scores(lambda h: jnp.broadcast_to(kn_t[:, h:h + 1], (HEAD_DIM, page))) * scale
    s_pages.append(jnp.where(lane == 0, own, NEG_INF))
    m = jnp.max(s_pages[0], axis=1, keepdims=True)
    for s in s_pages[1:]:
        m = jnp.maximum(m, jnp.max(s, axis=1, keepdims=True))
    p_pages = [jnp.exp(s - m) for s in s_pages]
    l = jnp.sum(p_pages[0], axis=1, keepdims=True)
    for p in p_pages[1:]:
        l = l + jnp.sum(p, axis=1, keepdims=True)
    outs = []
    for h in range(N_HEADS):
        acc = jnp.broadcast_to(p_pages[n_pages][h:h + 1, :], (HEAD_DIM, page)) * \
            jnp.broadcast_to(vn_t[:, h:h + 1], (HEAD_DIM, page))
        for pg in range(n_pages):
            acc = acc + jnp.broadcast_to(p_pages[pg][h:h + 1, :], (HEAD_DIM, page)) * v_pages[pg][0, h]
        outs.append(jnp.sum(acc, axis=1, keepdims=True) / l[h:h + 1, :])
    o_ref[0] = jnp.concatenate(outs, axis=1)


def _attn_sample(q, k_new, v_new, cache_k, cache_v, page_table):
    n, a = q.shape
    n_pool, page, h, dh = cache_k.shape
    n_pages = page_table.shape[1]
    assert h * dh == a and MOBA_BLOCK % page == 0 and (n_pages * page) % MOBA_BLOCK == 0
    slopes = jnp.asarray(np.broadcast_to(_alibi_slopes(N_HEADS).reshape(N_HEADS, 1), (N_HEADS, LANES)).copy())
    ck = jnp.transpose(cache_k, (0, 2, 3, 1))
    cv = jnp.transpose(cache_v, (0, 2, 3, 1))
    tok_t = lambda x: jnp.transpose(x.reshape(n, h, dh), (0, 2, 1))
    tok = pl.BlockSpec((1, dh, h), lambda b, pt: (b, 0, 0))
    page_spec = lambda pg: pl.BlockSpec((1, h, dh, page), lambda b, pt: (pt[b, pg], 0, 0, 0))
    grid_spec = pltpu.PrefetchScalarGridSpec(
        num_scalar_prefetch=1,
        grid=(n,),
        in_specs=[tok, tok, tok, pl.BlockSpec((N_HEADS, LANES), lambda b, pt: (0, 0))]
                 + [page_spec(pg) for pg in range(n_pages)] * 2,
        out_specs=tok,
    )
    out = pl.pallas_call(
        functools.partial(_attn_sample_kernel, n_pages=n_pages, page=page),
        grid_spec=grid_spec,
        out_shape=jax.ShapeDtypeStruct((n, dh, h), F32),
        compiler_params=pltpu.CompilerParams(dimension_semantics=("arbitrary",),
                                             vmem_limit_bytes=48 * 1024 * 1024),
        name="attn_sample",
    )(page_table, tok_t(q), tok_t(k_new), tok_t(v_new), slopes, *([ck] * n_pages), *([cv] * n_pages))
    return jnp.transpose(out, (0, 2, 1)).reshape(n, a)


def _outffn_kernel(x_ref, a_ref, gn_ref, ag_ref, wo_ref, g2_ref, wg_ref, wu_ref, wd_ref, gf_ref, y_ref,
                   *, ff_chunk):
    a = ATTN_WIDTH
    attn = a_ref[0].T if len(a_ref.shape) == 3 else a_ref[...]
    an = _rmsnorm(attn, ag_ref[...]).astype(BF16)
    mixed = _dot(an, wo_ref[:a, :]) + _dot(gn_ref[...], wo_ref[a:, :])
    x1 = x_ref[...] + mixed
    hf = _rmsnorm(x1, g2_ref[...]).astype(BF16)
    d_ff = wg_ref.shape[1]
    ff = None
    for c in range(0, d_ff, ff_chunk):
        gate = _dot(hf, wg_ref[:, c:c + ff_chunk])
        up = _dot(hf, wu_ref[:, c:c + ff_chunk])
        act = (gate * jax.nn.sigmoid(gate) * up).astype(BF16)
        part = _dot(act, wd_ref[c:c + ff_chunk, :])
        ff = part if ff is None else ff + part
    y_ref[...] = _rmsnorm(x1 + ff, gf_ref[...])


def _outffn(x, a, gn, ag, w_out, g2, w_gate, w_up, w_down, gf, *, tm):
    n, d = x.shape
    d_ff = w_gate.shape[1]
    assert n % tm == 0
    ff_chunk = d_ff // 2 if (d_ff // 2) % LANES == 0 else d_ff
    row = lambda w: pl.BlockSpec((tm, w), lambda i: (i, 0))
    if a.ndim == 3:
        per_seq = a.shape[2] // tm
        assert a.shape[2] % tm == 0 and a.shape[0] * a.shape[2] == n
        a_spec = pl.BlockSpec((1, ATTN_WIDTH, tm), lambda i: (i // per_seq, 0, i % per_seq))
    else:
        a_spec = row(ATTN_WIDTH)
    return pl.pallas_call(
        functools.partial(_outffn_kernel, ff_chunk=ff_chunk),
        grid=(n // tm,),
        in_specs=[row(d), a_spec, row(SGU_WIDTH), _const_spec((1, ATTN_WIDTH)), _const_spec(w_out.shape),
                  _const_spec((1, d)), _const_spec(w_gate.shape), _const_spec(w_up.shape),
                  _const_spec(w_down.shape), _const_spec((1, d))],
        out_specs=row(d),
        out_shape=jax.ShapeDtypeStruct((n, d), F32),
        compiler_params=pltpu.CompilerParams(dimension_semantics=("arbitrary",),
                                             vmem_limit_bytes=56 * 1024 * 1024),
        name="outffn",
    )(x, a, gn, ag.reshape(1, -1), w_out, g2.reshape(1, -1), w_gate, w_up, w_down, gf.reshape(1, -1))


def kernel(x_prompt, x_sample, cache_k, cache_v, page_table, norm1_g, w_in, attn_out_g, sgu_ln_g, sgu_ln_b,
           sgu_w, sgu_b, sgu_out_g, w_out, norm2_g, w_gate, w_up, w_down, final_g):
    depth = w_in.shape[0]
    assert depth == 1, "single-layer stack"
    l = 0
    bsz, seq, d = x_prompt.shape
    dec_b, dec_seq, _ = x_sample.shape
    assert dec_seq == 1
    xs = x_sample.reshape(dec_b, d)

    wo, wg, wu, wd = (w[l].astype(BF16) for w in (w_out, w_gate, w_up, w_down))

    qp_t, kb, kt, vt, vtb, kmean, gnp = _inproj_prompt(x_prompt, norm1_g[l], w_in[l], sgu_ln_g[l], sgu_ln_b[l],
                                                     sgu_w[l], sgu_b[l], sgu_out_g[l], tm=512)
    ap_t = _attn_prompt(qp_t, kb, vtb, kmean)
    yp = _outffn(x_prompt.reshape(bsz * seq, d), ap_t, gnp.reshape(bsz * seq, -1),
                 attn_out_g[l], wo, norm2_g[l], wg, wu, wd, final_g, tm=512)

    qs, ks, vs, gns, vns = _inproj_sample(xs, norm1_g[l], w_in[l], sgu_ln_g[l], sgu_ln_b[l], sgu_w[l], sgu_b[l],
                                          sgu_out_g[l])
    a_s = _attn_sample(qs, ks, vs, cache_k[l], cache_v[l], page_table)
    ys = _outffn(xs, a_s, gns, attn_out_g[l], wo, norm2_g[l], wg, wu, wd, final_g, tm=dec_b)

    heads_last = lambda x_t: jnp.transpose(x_t.reshape(bsz, N_HEADS, HEAD_DIM, seq), (0, 3, 1, 2))[None]
    y_prompt = yp.reshape(bsz, seq, d)
    y_sample = ys.reshape(dec_b, dec_seq, d)
    k_prompt = heads_last(kt)
    v_prompt = heads_last(vt)
    k_sample = ks.reshape(depth, dec_b, dec_seq, N_HEADS, HEAD_DIM)
    v_sample = vs.reshape(depth, dec_b, dec_seq, N_HEADS, HEAD_DIM)
    sgu_v_sample = vns.reshape(depth, dec_b, dec_seq, SGU_WIDTH)
    return (y_prompt, y_sample, k_prompt, v_prompt, k_sample, v_sample, sgu_v_sample)
```

```python
import functools
import math

import numpy as np
import jax
import jax.numpy as jnp
from jax import lax
from jax.experimental import pallas as pl
from jax.experimental.pallas import tpu as pltpu

HEAD_DIM = 64
N_HEADS = 8
ATTN_WIDTH = N_HEADS * HEAD_DIM
N_SGU_GROUPS = 8
SGU_GROUP_DIM = 64
SGU_WIDTH = N_SGU_GROUPS * SGU_GROUP_DIM
SGU_CHUNK = 128
MOBA_BLOCK = 256
MOBA_TOPK = 3
RMS_EPS = 1e-6
LN_EPS = 1e-5
NEG_INF = -1e30
LOG2E = 1.4426950408889634

LANES = 128
BF16_SUBLANES = 16
HEADS_PER_TILE = LANES // HEAD_DIM
N_HEAD_PAIRS = N_HEADS // HEADS_PER_TILE

F32 = jnp.float32
BF16 = jnp.bfloat16

_NT = (((1,), (1,)), ((), ()))
_NN = (((1,), (0,)), ((), ()))


def _dot(a, b, dims=_NN, precision=None):
    return lax.dot_general(a, b, dims, precision=precision, preferred_element_type=F32)


def _rmsnorm(x, g):
    r = lax.rsqrt(jnp.mean(x * x, axis=-1, keepdims=True) + RMS_EPS)
    return (x * r) * g


def _gelu_tanh(x):
    c = math.sqrt(2.0 / math.pi)
    return 0.5 * x * (1.0 + jnp.tanh(c * (x + 0.044715 * (x * x * x))))


def _alibi_slopes(n):
    start = 2.0 ** (-8.0 / n)
    return np.asarray([start ** (i + 1) for i in range(n)], dtype=np.float32)


def _split_weights(w_ref, whi_ref, wlo_ref, n_precise):
    rows = w_ref.shape[0]
    step = 128
    for r in range(0, rows, step):
        w = w_ref[r:r + step, :]
        hi = w.astype(BF16)
        whi_ref[r:r + step, :] = hi
        wlo_ref[r:r + step, :] = (w[:, :n_precise] - hi[:, :n_precise].astype(F32)).astype(BF16)


def _project(x, g1_ref, whi_ref, wlo_ref):
    n_precise = wlo_ref.shape[1]
    h = _rmsnorm(x, g1_ref[...])
    h_hi = h.astype(BF16)
    h_lo = (h - h_hi.astype(F32)).astype(BF16)
    w_qk = whi_ref[:, :n_precise]
    z_qk = _dot(h_hi, w_qk) + (_dot(h_lo, w_qk) + _dot(h_hi, wlo_ref[...]))
    z_rest = _dot(h_hi, whi_ref[:, n_precise:])
    return z_qk, z_rest


def _layernorm(x, g, b):
    mu = jnp.mean(x, axis=-1, keepdims=True)
    xc = x - mu
    var = jnp.mean(xc * xc, axis=-1, keepdims=True)
    return (xc * lax.rsqrt(var + LN_EPS)) * g + b


def _inproj_prompt_kernel(x_ref, g1_ref, w_ref, lng_ref, lnb_ref, sw_ref, sbx_ref, sog_ref,
                          q_ref, kb_ref, kt_ref, vt_ref, vtb_ref, kmean_ref, gn_ref,
                          wb_ref, wcat_ref, s_ref):
    tm = x_ref.shape[1]
    a = ATTN_WIDTH
    blk = MOBA_BLOCK

    @pl.when((pl.program_id(0) == 0) & (pl.program_id(1) == 0))
    def _prepare():
        for r in range(0, w_ref.shape[0], LANES):
            wb_ref[r:r + LANES, :] = w_ref[r:r + LANES, :].astype(BF16)
        t = lax.broadcasted_iota(jnp.int32, (SGU_CHUNK, SGU_CHUNK), 0)
        s = lax.broadcasted_iota(jnp.int32, (SGU_CHUNK, SGU_CHUNK), 1)
        causal = t >= s
        for gp in range(N_SGU_GROUPS // 2):
            w0 = jnp.where(causal, sw_ref[2 * gp], 0.0)
            w1 = jnp.where(causal, sw_ref[2 * gp + 1], 0.0)
            wcat_ref[gp] = jnp.concatenate([w0, w1], axis=1).astype(BF16)

    z = _dot(_rmsnorm(x_ref[0], g1_ref[...]).astype(BF16), wb_ref[...])
    q_ref[0] = z[:, :a].T
    zk = z[:, a:2 * a]
    zv = z[:, 2 * a:3 * a]
    z_rest = z[:, 2 * a:]
    kt_ref[0] = zk.T
    vt = zv.T
    vt_ref[0] = vt
    for r in range(tm // blk):
        rows = slice(r * blk, (r + 1) * blk)
        kb_ref[0, r] = zk[rows, :].astype(BF16)
        vtb_ref[0, r] = vt[:, rows].astype(BF16)
        kmean_ref[r] = jnp.mean(zk[rows, :], axis=0, keepdims=True)

    u = _gelu_tanh(z_rest[:, a:a + SGU_WIDTH])
    gv = _gelu_tanh(z_rest[:, a + SGU_WIDTH:])
    vn = _layernorm(gv, lng_ref[...], lnb_ref[...])

    lane = lax.broadcasted_iota(jnp.int32, (SGU_CHUNK, LANES), 1)
    low = lane < SGU_GROUP_DIM
    for c in range(tm // SGU_CHUNK):
        rows = slice(c * SGU_CHUNK, (c + 1) * SGU_CHUNK)
        for gp in range(N_SGU_GROUPS // 2):
            cols = slice(gp * LANES, (gp + 1) * LANES)
            vp = vn[rows, cols]
            rhs = jnp.concatenate([jnp.where(low, vp, 0.0), jnp.where(low, 0.0, vp)], axis=0).astype(BF16)
            s_ref[rows, cols] = _dot(wcat_ref[gp], rhs) + sbx_ref[:, cols]
    g = u * s_ref[...]
    gn_ref[0] = _rmsnorm(g, sog_ref[...]).astype(gn_ref.dtype)


def _inproj_sample_kernel(x_ref, g1_ref, w_ref, lng_ref, lnb_ref, w00_ref, b0_ref, sog_ref,
                          q_ref, k_ref, v_ref, gn_ref, vn_ref,
                          whi_ref, wlo_ref):
    a = ATTN_WIDTH

    @pl.when(pl.program_id(0) == 0)
    def _prepare():
        _split_weights(w_ref, whi_ref, wlo_ref, 2 * a)

    z_qk, z_rest = _project(x_ref[...], g1_ref, whi_ref, wlo_ref)
    q_ref[...] = z_qk[:, :a]
    k_ref[...] = z_qk[:, a:]
    v_ref[...] = z_rest[:, :a]
    u = _gelu_tanh(z_rest[:, a:a + SGU_WIDTH])
    gv = _gelu_tanh(z_rest[:, a + SGU_WIDTH:])
    vn = _layernorm(gv, lng_ref[...], lnb_ref[...])
    vn_ref[...] = vn
    g = u * (vn * w00_ref[...] + b0_ref[...])
    gn_ref[...] = _rmsnorm(g, sog_ref[...]).astype(gn_ref.dtype)


def _const_spec(shape):
    zeros = (0,) * len(shape)
    return pl.BlockSpec(shape, lambda *_: zeros, pipeline_mode=pl.Buffered(1))


def _inproj_prompt(x, g1, w_in, ln_g, ln_b, sgu_w, sgu_b, sog, *, tm):
    b, t, d = x.shape
    a = ATTN_WIDTH
    blk = MOBA_BLOCK
    assert t % tm == 0 and tm % SGU_CHUNK == 0 and tm % blk == 0
    nb = t // blk
    per = tm // blk
    sbx = jnp.repeat(sgu_b.T, SGU_GROUP_DIM, axis=1)
    row = lambda w: pl.BlockSpec((1, tm, w), lambda bi, i: (bi, i, 0))
    col = lambda w: pl.BlockSpec((1, w, tm), lambda bi, i: (bi, 0, i))
    out_specs = (col(a),
                 pl.BlockSpec((1, per, blk, a), lambda bi, i: (bi, i, 0, 0)),
                 col(a), col(a),
                 pl.BlockSpec((1, per, a, blk), lambda bi, i: (bi, i, 0, 0)),
                 pl.BlockSpec((per, 1, a), lambda bi, i: (bi * (nb // per) + i, 0, 0)),
                 row(SGU_WIDTH))
    out_shape = (jax.ShapeDtypeStruct((b, a, t), F32),
                 jax.ShapeDtypeStruct((b, nb, blk, a), BF16),
                 jax.ShapeDtypeStruct((b, a, t), F32),
                 jax.ShapeDtypeStruct((b, a, t), F32),
                 jax.ShapeDtypeStruct((b, nb, a, blk), BF16),
                 jax.ShapeDtypeStruct((b * nb, 1, a), F32),
                 jax.ShapeDtypeStruct((b, t, SGU_WIDTH), BF16))
    return pl.pallas_call(
        _inproj_prompt_kernel,
        grid=(b, t // tm),
        in_specs=[row(d), _const_spec((1, d)), _const_spec(w_in.shape), _const_spec((1, SGU_WIDTH)),
                  _const_spec((1, SGU_WIDTH)), _const_spec(sgu_w.shape), _const_spec(sbx.shape),
                  _const_spec((1, SGU_WIDTH))],
        out_specs=out_specs,
        out_shape=out_shape,
        scratch_shapes=[pltpu.VMEM(w_in.shape, BF16),
                        pltpu.VMEM((N_SGU_GROUPS // 2, SGU_CHUNK, 2 * SGU_CHUNK), BF16),
                        pltpu.VMEM((tm, SGU_WIDTH), F32)],
        compiler_params=pltpu.CompilerParams(dimension_semantics=("arbitrary", "arbitrary"),
                                             vmem_limit_bytes=56 * 1024 * 1024),
        name="inproj_prompt",
    )(x, g1.reshape(1, d), w_in, ln_g.reshape(1, -1), ln_b.reshape(1, -1), sgu_w, sbx, sog.reshape(1, -1))


def _inproj_sample(x, g1, w_in, ln_g, ln_b, sgu_w, sgu_b, sog):
    n, d = x.shape
    a = ATTN_WIDTH
    w00 = jnp.repeat(sgu_w[:, 0, 0], SGU_GROUP_DIM).reshape(1, SGU_WIDTH)
    b0 = jnp.repeat(sgu_b[:, 0], SGU_GROUP_DIM).reshape(1, SGU_WIDTH)
    row = lambda w: pl.BlockSpec((n, w), lambda i: (0, 0))
    vec = _const_spec((1, SGU_WIDTH))
    out_shape = (jax.ShapeDtypeStruct((n, a), F32),) * 3 + (jax.ShapeDtypeStruct((n, SGU_WIDTH), BF16),
                                                           jax.ShapeDtypeStruct((n, SGU_WIDTH), F32))
    return pl.pallas_call(
        _inproj_sample_kernel,
        grid=(1,),
        in_specs=[row(d), _const_spec((1, d)), _const_spec(w_in.shape), vec, vec, vec, vec, vec],
        out_specs=(row(a), row(a), row(a), row(SGU_WIDTH), row(SGU_WIDTH)),
        out_shape=out_shape,
        scratch_shapes=[pltpu.VMEM(w_in.shape, BF16), pltpu.VMEM((d, 2 * a), BF16)],
        compiler_params=pltpu.CompilerParams(dimension_semantics=("arbitrary",),
                                             vmem_limit_bytes=56 * 1024 * 1024),
        name="inproj_sample",
    )(x, g1.reshape(1, d), w_in, ln_g.reshape(1, -1), ln_b.reshape(1, -1), w00, b0, sog.reshape(1, -1))


def _block_choice(gate_t, own):
    nb = gate_t.shape[0]
    blk = lax.broadcasted_iota(jnp.int32, gate_t.shape, 0)
    g = jnp.where(blk < own, gate_t, NEG_INF)
    picked = jnp.zeros(gate_t.shape, F32)
    for _ in range(MOBA_TOPK):
        top = jnp.max(g, axis=0, keepdims=True)
        first = jnp.min(jnp.where(g == top, blk, nb), axis=0, keepdims=True)
        hit = blk == first
        picked = jnp.where(hit, 1.0, picked)
        g = jnp.where(hit, -jnp.inf, g)
    return jnp.where(blk < own, picked, 0.0)


def _split3(x):
    hi = x.astype(BF16)
    r = x - hi.astype(F32)
    mid = r.astype(BF16)
    return hi, mid, (r - mid.astype(F32)).astype(BF16)


def _attn_prompt_kernel(q_ref, kb_ref, vtb_ref, km_ref, sl_ref, o_ref,
                        kmh_ref, kml_ref, kl_ref, qs_ref, sel_ref, acc_ref, m_ref, l_ref, s_ref, cm_ref):
    i = pl.program_id(1)
    blk = MOBA_BLOCK
    nb = kb_ref.shape[1]
    scale2 = LOG2E / math.sqrt(HEAD_DIM)

    @pl.when(i == 0)
    def _prepare():
        km = km_ref[:, 0, :]
        lane_head = lax.broadcasted_iota(jnp.int32, km.shape, 1) // HEAD_DIM
        for h in range(N_HEADS):
            mine = jnp.where(lane_head == h, km, 0.0)
            hi = mine.astype(BF16)
            kmh_ref[h * nb:(h + 1) * nb, :] = hi
            kml_ref[h * nb:(h + 1) * nb, :] = (mine - hi.astype(F32)).astype(BF16)
        lane = lax.broadcasted_iota(jnp.int32, (blk, LANES), 1)
        kl = lax.broadcasted_iota(jnp.int32, (blk, LANES), 0).astype(F32)
        kl_ref[...] = jnp.where(lane < 3, kl, 0.0).astype(BF16)
        row = lax.broadcasted_iota(jnp.int32, (LANES, blk), 0)
        for h in range(N_HEADS):
            hi, mid, lo = (x.astype(F32) for x in _split3(jnp.broadcast_to(sl_ref[h:h + 1, :], (LANES, blk))))
            terms = jnp.where(row == 0, hi, jnp.where(row == 1, mid, jnp.where(row == 2, lo, 0.0)))
            qs_ref[h, LANES:, :] = terms.astype(BF16)

    q_t = q_ref[0]
    q_hi = q_t.astype(BF16)
    q_lo = (q_t - q_hi.astype(F32)).astype(BF16)
    gate_t = _dot(kmh_ref[...], q_hi) + (_dot(kml_ref[...], q_hi) + _dot(kmh_ref[...], q_lo))
    row = lax.broadcasted_iota(jnp.int32, (LANES, blk), 0)
    for h in range(N_HEADS):
        pair, hh = divmod(h, HEADS_PER_TILE)
        q_pair = q_t[pair * LANES:(pair + 1) * LANES, :]
        mine = (row >= hh * HEAD_DIM) & (row < (hh + 1) * HEAD_DIM)
        qs_ref[h, :LANES, :] = (jnp.where(mine, q_pair, 0.0) * scale2).astype(BF16)

    def scores(j, h):
        pair = h // HEADS_PER_TILE
        keys = jnp.concatenate([kb_ref[0, j, :, pair * LANES:(pair + 1) * LANES], kl_ref[...]], axis=1)
        return _dot(keys, qs_ref[h])

    ones_rows = jnp.ones((BF16_SUBLANES, blk), BF16)

    def value_product(j, h, p):
        rows = jnp.concatenate([vtb_ref[0, j, h * HEAD_DIM:(h + 1) * HEAD_DIM, :], ones_rows], axis=0)
        pv = _dot(rows, p.astype(BF16))
        return pv[:HEAD_DIM], pv[HEAD_DIM:HEAD_DIM + 1]

    def issue_scores(j, slot, h):
        s = scores(j, h)
        s_ref[slot, h] = s
        cm_ref[slot, h] = jnp.max(s, axis=0, keepdims=True)

    for h in range(N_HEADS):
        s_ref[1, h] = scores(i, h)
    for h in range(N_HEADS):
        issue_scores(0, 0, h)
    for h in range(N_HEADS):
        sel_ref[h] = _block_choice(gate_t[h * nb:(h + 1) * nb, :], i)

    kl = lax.broadcasted_iota(jnp.int32, (blk, blk), 0)
    ql = lax.broadcasted_iota(jnp.int32, (blk, blk), 1)
    for h in range(N_HEADS):
        s = jnp.where(kl <= ql, s_ref[1, h], NEG_INF)
        m = jnp.max(s, axis=0, keepdims=True)
        m_ref[h] = m
        acc_ref[h], l_ref[h] = value_product(i, h, jnp.exp2(s - m))

    def finish_block(j, slot, h):
        m, l = m_ref[h], l_ref[h]
        c = sl_ref[h:h + 1, :] * ((j - i).astype(F32) * float(blk))
        chosen = sel_ref[h, pl.ds(j, 1), :] > 0.5
        m_new = jnp.where(chosen, jnp.maximum(m, cm_ref[slot, h] + c), m)
        alpha = jnp.exp2(m - m_new)
        shift = jnp.where(chosen, m_new - c, -NEG_INF)
        pv, p_sum = value_product(j, h, jnp.exp2(s_ref[slot, h] - shift))
        m_ref[h] = m_new
        l_ref[h] = alpha * l + p_sum
        acc_ref[h] = alpha * acc_ref[h] + pv

    def trip(t, carry):
        for slot in range(2):
            j = 2 * t + slot
            for h in range(N_HEADS):
                issue_scores(jnp.minimum(j + 1, nb - 1), 1 - slot, h)
                finish_block(j, slot, h)
        return carry

    lax.fori_loop(0, (i + 1) // 2, trip, 0)
    for h in range(N_HEADS):
        o_ref[0, h * HEAD_DIM:(h + 1) * HEAD_DIM, :] = acc_ref[h] / l_ref[h]


def _attn_prompt(q_t, kb, vtb, kmean):
    b, a, t = q_t.shape
    blk = MOBA_BLOCK
    nb = t // blk
    assert t % blk == 0 and a == ATTN_WIDTH and kb.shape == (b, nb, blk, a) and vtb.shape == (b, nb, a, blk)
    slopes2 = (_alibi_slopes(N_HEADS) * np.float32(LOG2E)).reshape(N_HEADS, 1)
    slopes2 = jnp.asarray(np.broadcast_to(slopes2, (N_HEADS, blk)).copy())
    once = pl.Buffered(1)
    return pl.pallas_call(
        _attn_prompt_kernel,
        grid=(b, nb),
        in_specs=[pl.BlockSpec((1, a, blk), lambda bi, i: (bi, 0, i)),
                  pl.BlockSpec((1, nb, blk, a), lambda bi, i: (bi, 0, 0, 0), pipeline_mode=once),
                  pl.BlockSpec((1, nb, a, blk), lambda bi, i: (bi, 0, 0, 0), pipeline_mode=once),
                  pl.BlockSpec((nb, 1, a), lambda bi, i: (bi, 0, 0)),
                  pl.BlockSpec((N_HEADS, blk), lambda bi, i: (0, 0))],
        out_specs=pl.BlockSpec((1, a, blk), lambda bi, i: (bi, 0, i)),
        out_shape=jax.ShapeDtypeStruct((b, a, t), F32),
        scratch_shapes=[pltpu.VMEM((N_HEADS * nb, a), BF16),
                        pltpu.VMEM((N_HEADS * nb, a), BF16),
                        pltpu.VMEM((blk, LANES), BF16),
                        pltpu.VMEM((N_HEADS, 2 * LANES, blk), BF16),
                        pltpu.VMEM((N_HEADS, nb, blk), F32),
                        pltpu.VMEM((N_HEADS, HEAD_DIM, blk), F32),
                        pltpu.VMEM((N_HEADS, 1, blk), F32),
                        pltpu.VMEM((N_HEADS, 1, blk), F32),
                        pltpu.VMEM((2, N_HEADS, blk, blk), F32),
                        pltpu.VMEM((2, N_HEADS, 1, blk), F32)],
        compiler_params=pltpu.CompilerParams(dimension_semantics=("arbitrary", "arbitrary"),
                                             vmem_limit_bytes=48 * 1024 * 1024),
        name="attn_prompt",
    )(q_t, kb, vtb, kmean, slopes2)


def _sample_attention(q_ref, kn_ref, vn_ref, sl_ref, k_pages, v_pages, o_ref):
    n_pages = len(k_pages)
    page = k_pages[0].shape[-1]
    past = n_pages * page
    pages_per_block = MOBA_BLOCK // page
    nb = past // MOBA_BLOCK
    scale = 1.0 / math.sqrt(HEAD_DIM)

    q_t = q_ref[0]
    q_b = [jnp.broadcast_to(q_t[:, h:h + 1], (HEAD_DIM, page)) for h in range(N_HEADS)]

    def raw_scores(tile):
        return jnp.concatenate([jnp.sum(q_b[h] * tile(h), axis=0, keepdims=True) for h in range(N_HEADS)], axis=0)

    raw = [raw_scores(lambda h, pg=pg: k_pages[pg][0, h]) for pg in range(n_pages)]
    gate = []
    for n in range(nb):
        tot = raw[n * pages_per_block]
        for r in range(1, pages_per_block):
            tot = tot + raw[n * pages_per_block + r]
        gate.append(jnp.sum(tot, axis=1, keepdims=True) * (1.0 / MOBA_BLOCK))
    chosen = []
    for n in range(nb):
        rank = jnp.where(NEG_INF > gate[n], 1.0, 0.0)
        for m in range(nb):
            if m != n:
                ahead = (gate[m] >= gate[n]) if m < n else (gate[m] > gate[n])
                rank = rank + jnp.where(ahead, 1.0, 0.0)
        chosen.append(jnp.where(rank < MOBA_TOPK, 1.0, 0.0))

    slopes = sl_ref[:, :1]
    lane = lax.broadcasted_iota(jnp.int32, (N_HEADS, page), 1)
    dist0 = (past - lane).astype(F32)
    s_pages = []
    for pg in range(n_pages):
        s = raw[pg] * scale - slopes * (dist0 - float(pg * page))
        ok = jnp.broadcast_to(chosen[pg // pages_per_block], s.shape) > 0.5
        s_pages.append(jnp.where(ok, s, NEG_INF))
    kn_t, vn_t = kn_ref[0], vn_ref[0]
    own = raw_scores(lambda h: jnp.broadcast_to(kn_t[:, h:h + 1], (HEAD_DIM, page))) * scale
    s_pages.append(jnp.where(lane == 0, own, NEG_INF))
    m = jnp.max(s_pages[0], axis=1, keepdims=True)
    for s in s_pages[1:]:
        m = jnp.maximum(m, jnp.max(s, axis=1, keepdims=True))
    p_pages = [jnp.exp(s - m) for s in s_pages]
    l = jnp.sum(p_pages[0], axis=1, keepdims=True)
    for p in p_pages[1:]:
        l = l + jnp.sum(p, axis=1, keepdims=True)
    outs = []
    for h in range(N_HEADS):
        acc = jnp.broadcast_to(p_pages[n_pages][h:h + 1, :], (HEAD_DIM, page)) * \
            jnp.broadcast_to(vn_t[:, h:h + 1], (HEAD_DIM, page))
        for pg in range(n_pages):
            acc = acc + jnp.broadcast_to(p_pages[pg][h:h + 1, :], (HEAD_DIM, page)) * v_pages[pg][0, h]
        outs.append(jnp.sum(acc, axis=1, keepdims=True) / l[h:h + 1, :])
    o_ref[0] = jnp.concatenate(outs, axis=1)


N_FFN_INPUTS = 10


def _outffn_body(x_ref, a_ref, gn_ref, ag_ref, wo_ref, g2_ref, wg_ref, wu_ref, wd_ref, gf_ref, y_ref, ff_chunk):
    a = ATTN_WIDTH
    attn = a_ref[0].T if len(a_ref.shape) == 3 else a_ref[...]
    an = _rmsnorm(attn, ag_ref[...]).astype(BF16)
    mixed = _dot(an, wo_ref[:a, :]) + _dot(gn_ref[...], wo_ref[a:, :])
    x1 = x_ref[...] + mixed
    hf = _rmsnorm(x1, g2_ref[...]).astype(BF16)
    d_ff = wg_ref.shape[1]
    ff = None
    for c in range(0, d_ff, ff_chunk):
        gate = _dot(hf, wg_ref[:, c:c + ff_chunk])
        up = _dot(hf, wu_ref[:, c:c + ff_chunk])
        act = (gate * jax.nn.sigmoid(gate) * up).astype(BF16)
        part = _dot(act, wd_ref[c:c + ff_chunk, :])
        ff = part if ff is None else ff + part
    y_ref[...] = _rmsnorm(x1 + ff, gf_ref[...])


def _outffn_kernel(*refs, ff_chunk):
    _outffn_body(*refs, ff_chunk)


def _outffn_attn_sample_kernel(pt_ref, *refs, ff_chunk, n_pages):
    del pt_ref
    ffn_in, rest = refs[:N_FFN_INPUTS], refs[N_FFN_INPUTS:]
    q_ref, kn_ref, vn_ref, sl_ref = rest[:4]
    k_pages, v_pages = rest[4:4 + n_pages], rest[4 + n_pages:4 + 2 * n_pages]
    y_ref, o_ref = rest[4 + 2 * n_pages:]
    _outffn_body(*ffn_in, y_ref, ff_chunk)
    _sample_attention(q_ref, kn_ref, vn_ref, sl_ref, k_pages, v_pages, o_ref)


def _ffn_chunk(d_ff):
    return d_ff // 2 if (d_ff // 2) % LANES == 0 else d_ff


def _outffn(x, a, gn, ag, w_out, g2, w_gate, w_up, w_down, gf):
    n, d = x.shape
    row = lambda w: pl.BlockSpec((n, w), lambda i: (0, 0))
    return pl.pallas_call(
        functools.partial(_outffn_kernel, ff_chunk=_ffn_chunk(w_gate.shape[1])),
        grid=(1,),
        in_specs=[row(d), row(ATTN_WIDTH), row(SGU_WIDTH), _const_spec((1, ATTN_WIDTH)), _const_spec(w_out.shape),
                  _const_spec((1, d)), _const_spec(w_gate.shape), _const_spec(w_up.shape),
                  _const_spec(w_down.shape), _const_spec((1, d))],
        out_specs=row(d),
        out_shape=jax.ShapeDtypeStruct((n, d), F32),
        compiler_params=pltpu.CompilerParams(dimension_semantics=("arbitrary",),
                                             vmem_limit_bytes=56 * 1024 * 1024),
        name="outffn_sample",
    )(x, a, gn, ag.reshape(1, -1), w_out, g2.reshape(1, -1), w_gate, w_up, w_down, gf.reshape(1, -1))


def _outffn_with_sample_attention(x, a_t, gn, ag, w_out, g2, w_gate, w_up, w_down, gf,
                                  q, k_new, v_new, cache_k, cache_v, page_table):
    n, d = x.shape
    n_seq, a = q.shape
    n_pool, page, h, dh = cache_k.shape
    n_pages = page_table.shape[1]
    assert h * dh == a and MOBA_BLOCK % page == 0 and (n_pages * page) % MOBA_BLOCK == 0
    assert n % n_seq == 0
    tm = n // n_seq
    per_seq = a_t.shape[2] // tm
    assert tm % BF16_SUBLANES == 0 and a_t.shape[2] % tm == 0 and a_t.shape[0] * a_t.shape[2] == n
    slopes = jnp.asarray(np.broadcast_to(_alibi_slopes(N_HEADS).reshape(N_HEADS, 1), (N_HEADS, LANES)).copy())
    ck = jnp.transpose(cache_k, (0, 2, 3, 1))
    cv = jnp.transpose(cache_v, (0, 2, 3, 1))
    tok_t = lambda z: jnp.transpose(z.reshape(n_seq, h, dh), (0, 2, 1))
    row = lambda w: pl.BlockSpec((tm, w), lambda i, pt: (i, 0))
    tok = pl.BlockSpec((1, dh, h), lambda i, pt: (i, 0, 0))
    page_spec = lambda pg: pl.BlockSpec((1, h, dh, page), lambda i, pt: (pt[i, pg], 0, 0, 0))
    grid_spec = pltpu.PrefetchScalarGridSpec(
        num_scalar_prefetch=1,
        grid=(n_seq,),
        in_specs=[row(d), pl.BlockSpec((1, ATTN_WIDTH, tm), lambda i, pt: (i // per_seq, 0, i % per_seq)),
                  row(SGU_WIDTH), _const_spec((1, ATTN_WIDTH)), _const_spec(w_out.shape), _const_spec((1, d)),
                  _const_spec(w_gate.shape), _const_spec(w_up.shape), _const_spec(w_down.shape),
                  _const_spec((1, d)),
                  tok, tok, tok, pl.BlockSpec((N_HEADS, LANES), lambda i, pt: (0, 0))]
                 + [page_spec(pg) for pg in range(n_pages)] * 2,
        out_specs=(row(d), tok),
    )
    y, out = pl.pallas_call(
        functools.partial(_outffn_attn_sample_kernel, ff_chunk=_ffn_chunk(w_gate.shape[1]), n_pages=n_pages),
        grid_spec=grid_spec,
        out_shape=(jax.ShapeDtypeStruct((n, d), F32), jax.ShapeDtypeStruct((n_seq, dh, h), F32)),
        compiler_params=pltpu.CompilerParams(dimension_semantics=("arbitrary",),
                                             vmem_limit_bytes=56 * 1024 * 1024),
        name="outffn_prompt_attn_sample",
    )(page_table, x, a_t, gn, ag.reshape(1, -1), w_out, g2.reshape(1, -1), w_gate, w_up, w_down,
      gf.reshape(1, -1), tok_t(q), tok_t(k_new), tok_t(v_new), slopes, *([ck] * n_pages), *([cv] * n_pages))
    return y, jnp.transpose(out, (0, 2, 1)).reshape(n_seq, a)


def kernel(x_prompt, x_sample, cache_k, cache_v, page_table, norm1_g, w_in, attn_out_g, sgu_ln_g, sgu_ln_b,
           sgu_w, sgu_b, sgu_out_g, w_out, norm2_g, w_gate, w_up, w_down, final_g):
    depth = w_in.shape[0]
    assert depth == 1, "single-layer stack"
    l = 0
    bsz, seq, d = x_prompt.shape
    dec_b, dec_seq, _ = x_sample.shape
    assert dec_seq == 1
    xs = x_sample.reshape(dec_b, d)

    wo, wg, wu, wd = (w[l].astype(BF16) for w in (w_out, w_gate, w_up, w_down))

    qp_t, kb, kt, vt, vtb, kmean, gnp = _inproj_prompt(x_prompt, norm1_g[l], w_in[l], sgu_ln_g[l], sgu_ln_b[l],
                                                     sgu_w[l], sgu_b[l], sgu_out_g[l], tm=512)
    ap_t = _attn_prompt(qp_t, kb, vtb, kmean)

    qs, ks, vs, gns, vns = _inproj_sample(xs, norm1_g[l], w_in[l], sgu_ln_g[l], sgu_ln_b[l], sgu_w[l], sgu_b[l],
                                          sgu_out_g[l])
    yp, a_s = _outffn_with_sample_attention(
        x_prompt.reshape(bsz * seq, d), ap_t, gnp.reshape(bsz * seq, -1), attn_out_g[l], wo, norm2_g[l],
        wg, wu, wd, final_g, qs, ks, vs, cache_k[l], cache_v[l], page_table)
    ys = _outffn(xs, a_s, gns, attn_out_g[l], wo, norm2_g[l], wg, wu, wd, final_g)

    heads_last = lambda x_t: jnp.transpose(x_t.reshape(bsz, N_HEADS, HEAD_DIM, seq), (0, 3, 1, 2))[None]
    y_prompt = yp.reshape(bsz, seq, d)
    y_sample = ys.reshape(dec_b, dec_seq, d)
    k_prompt = heads_last(kt)
    v_prompt = heads_last(vt)
    k_sample = ks.reshape(depth, dec_b, dec_seq, N_HEADS, HEAD_DIM)
    v_sample = vs.reshape(depth, dec_b, dec_seq, N_HEADS, HEAD_DIM)
    sgu_v_sample = vns.reshape(depth, dec_b, dec_seq, SGU_WIDTH)
    return (y_prompt, y_sample, k_prompt, v_prompt, k_sample, v_sample, sgu_v_sample)
```

```python
import functools
import math

import numpy as np
import jax
import jax.numpy as jnp
from jax import lax
from jax.experimental import pallas as pl
from jax.experimental.pallas import tpu as pltpu

HEAD_DIM = 64
N_HEADS = 8
ATTN_WIDTH = N_HEADS * HEAD_DIM
N_SGU_GROUPS = 8
SGU_GROUP_DIM = 64
SGU_WIDTH = N_SGU_GROUPS * SGU_GROUP_DIM
SGU_CHUNK = 128
MOBA_BLOCK = 256
MOBA_TOPK = 3
RMS_EPS = 1e-6
LN_EPS = 1e-5
NEG_INF = -1e30
LOG2E = 1.4426950408889634

LANES = 128
BF16_SUBLANES = 16
HEADS_PER_TILE = LANES // HEAD_DIM
N_HEAD_PAIRS = N_HEADS // HEADS_PER_TILE

F32 = jnp.float32
BF16 = jnp.bfloat16

_NT = (((1,), (1,)), ((), ()))
_NN = (((1,), (0,)), ((), ()))


def _dot(a, b, dims=_NN, precision=None):
    return lax.dot_general(a, b, dims, precision=precision, preferred_element_type=F32)


def _rmsnorm(x, g):
    r = lax.rsqrt(jnp.mean(x * x, axis=-1, keepdims=True) + RMS_EPS)
    return (x * r) * g


def _gelu_tanh(x):
    c = math.sqrt(2.0 / math.pi)
    return 0.5 * x * (1.0 + jnp.tanh(c * (x + 0.044715 * (x * x * x))))


def _alibi_slopes(n):
    start = 2.0 ** (-8.0 / n)
    return np.asarray([start ** (i + 1) for i in range(n)], dtype=np.float32)


def _split_weights(w_ref, whi_ref, wlo_ref, n_precise):
    rows = w_ref.shape[0]
    step = 128
    for r in range(0, rows, step):
        w = w_ref[r:r + step, :]
        hi = w.astype(BF16)
        whi_ref[r:r + step, :] = hi
        wlo_ref[r:r + step, :] = (w[:, :n_precise] - hi[:, :n_precise].astype(F32)).astype(BF16)


def _project(x, g1_ref, whi_ref, wlo_ref):
    n_precise = wlo_ref.shape[1]
    h = _rmsnorm(x, g1_ref[...])
    h_hi = h.astype(BF16)
    h_lo = (h - h_hi.astype(F32)).astype(BF16)
    w_qk = whi_ref[:, :n_precise]
    z_qk = _dot(h_hi, w_qk) + (_dot(h_lo, w_qk) + _dot(h_hi, wlo_ref[...]))
    z_rest = _dot(h_hi, whi_ref[:, n_precise:])
    return z_qk, z_rest


def _layernorm(x, g, b):
    mu = jnp.mean(x, axis=-1, keepdims=True)
    xc = x - mu
    var = jnp.mean(xc * xc, axis=-1, keepdims=True)
    return (xc * lax.rsqrt(var + LN_EPS)) * g + b


def _inproj_prompt_kernel(x_ref, g1_ref, w_ref, lng_ref, lnb_ref, sw_ref, sbx_ref, sog_ref,
                          q_ref, kb_ref, kt_ref, vt_ref, vtb_ref, kmean_ref, gn_ref,
                          wb_ref, wcat_ref, s_ref):
    tm = x_ref.shape[1]
    a = ATTN_WIDTH
    blk = MOBA_BLOCK

    @pl.when((pl.program_id(0) == 0) & (pl.program_id(1) == 0))
    def _prepare():
        for r in range(0, w_ref.shape[0], LANES):
            wb_ref[r:r + LANES, :] = w_ref[r:r + LANES, :].astype(BF16)
        t = lax.broadcasted_iota(jnp.int32, (SGU_CHUNK, SGU_CHUNK), 0)
        s = lax.broadcasted_iota(jnp.int32, (SGU_CHUNK, SGU_CHUNK), 1)
        causal = t >= s
        for gp in range(N_SGU_GROUPS // 2):
            w0 = jnp.where(causal, sw_ref[2 * gp], 0.0)
            w1 = jnp.where(causal, sw_ref[2 * gp + 1], 0.0)
            wcat_ref[gp] = jnp.concatenate([w0, w1], axis=1).astype(BF16)

    z = _dot(_rmsnorm(x_ref[0], g1_ref[...]).astype(BF16), wb_ref[...])
    q_ref[0] = z[:, :a].T
    zk = z[:, a:2 * a]
    zv = z[:, 2 * a:3 * a]
    z_rest = z[:, 2 * a:]
    kt_ref[0] = zk.T
    vt = zv.T
    vt_ref[0] = vt
    for r in range(tm // blk):
        rows = slice(r * blk, (r + 1) * blk)
        kb_ref[0, r] = zk[rows, :].astype(BF16)
        vtb_ref[0, r] = vt[:, rows].astype(BF16)
        kmean_ref[r] = jnp.mean(zk[rows, :], axis=0, keepdims=True)

    u = _gelu_tanh(z_rest[:, a:a + SGU_WIDTH])
    gv = _gelu_tanh(z_rest[:, a + SGU_WIDTH:])
    vn = _layernorm(gv, lng_ref[...], lnb_ref[...])

    lane = lax.broadcasted_iota(jnp.int32, (SGU_CHUNK, LANES), 1)
    low = lane < SGU_GROUP_DIM
    for c in range(tm // SGU_CHUNK):
        rows = slice(c * SGU_CHUNK, (c + 1) * SGU_CHUNK)
        for gp in range(N_SGU_GROUPS // 2):
            cols = slice(gp * LANES, (gp + 1) * LANES)
            vp = vn[rows, cols]
            rhs = jnp.concatenate([jnp.where(low, vp, 0.0), jnp.where(low, 0.0, vp)], axis=0).astype(BF16)
            s_ref[rows, cols] = _dot(wcat_ref[gp], rhs) + sbx_ref[:, cols]
    g = u * s_ref[...]
    gn_ref[0] = _rmsnorm(g, sog_ref[...]).astype(gn_ref.dtype)


def _inproj_sample_kernel(x_ref, g1_ref, w_ref, lng_ref, lnb_ref, w00_ref, b0_ref, sog_ref,
                          q_ref, k_ref, v_ref, gn_ref, vn_ref,
                          whi_ref, wlo_ref):
    a = ATTN_WIDTH

    @pl.when(pl.program_id(0) == 0)
    def _prepare():
        _split_weights(w_ref, whi_ref, wlo_ref, 2 * a)

    z_qk, z_rest = _project(x_ref[...], g1_ref, whi_ref, wlo_ref)
    q_ref[...] = z_qk[:, :a]
    k_ref[...] = z_qk[:, a:]
    v_ref[...] = z_rest[:, :a]
    u = _gelu_tanh(z_rest[:, a:a + SGU_WIDTH])
    gv = _gelu_tanh(z_rest[:, a + SGU_WIDTH:])
    vn = _layernorm(gv, lng_ref[...], lnb_ref[...])
    vn_ref[...] = vn
    g = u * (vn * w00_ref[...] + b0_ref[...])
    gn_ref[...] = _rmsnorm(g, sog_ref[...]).astype(gn_ref.dtype)


def _const_spec(shape):
    zeros = (0,) * len(shape)
    return pl.BlockSpec(shape, lambda *_: zeros, pipeline_mode=pl.Buffered(1))


def _inproj_prompt(x, g1, w_in, ln_g, ln_b, sgu_w, sgu_b, sog, *, tm):
    b, t, d = x.shape
    a = ATTN_WIDTH
    blk = MOBA_BLOCK
    assert t % tm == 0 and tm % SGU_CHUNK == 0 and tm % blk == 0
    nb = t // blk
    per = tm // blk
    sbx = jnp.repeat(sgu_b.T, SGU_GROUP_DIM, axis=1)
    row = lambda w: pl.BlockSpec((1, tm, w), lambda bi, i: (bi, i, 0))
    col = lambda w: pl.BlockSpec((1, w, tm), lambda bi, i: (bi, 0, i))
    out_specs = (col(a),
                 pl.BlockSpec((1, per, blk, a), lambda bi, i: (bi, i, 0, 0)),
                 col(a), col(a),
                 pl.BlockSpec((1, per, a, blk), lambda bi, i: (bi, i, 0, 0)),
                 pl.BlockSpec((per, 1, a), lambda bi, i: (bi * (nb // per) + i, 0, 0)),
                 row(SGU_WIDTH))
    out_shape = (jax.ShapeDtypeStruct((b, a, t), F32),
                 jax.ShapeDtypeStruct((b, nb, blk, a), BF16),
                 jax.ShapeDtypeStruct((b, a, t), F32),
                 jax.ShapeDtypeStruct((b, a, t), F32),
                 jax.ShapeDtypeStruct((b, nb, a, blk), BF16),
                 jax.ShapeDtypeStruct((b * nb, 1, a), F32),
                 jax.ShapeDtypeStruct((b, t, SGU_WIDTH), BF16))
    return pl.pallas_call(
        _inproj_prompt_kernel,
        grid=(b, t // tm),
        in_specs=[row(d), _const_spec((1, d)), _const_spec(w_in.shape), _const_spec((1, SGU_WIDTH)),
                  _const_spec((1, SGU_WIDTH)), _const_spec(sgu_w.shape), _const_spec(sbx.shape),
                  _const_spec((1, SGU_WIDTH))],
        out_specs=out_specs,
        out_shape=out_shape,
        scratch_shapes=[pltpu.VMEM(w_in.shape, BF16),
                        pltpu.VMEM((N_SGU_GROUPS // 2, SGU_CHUNK, 2 * SGU_CHUNK), BF16),
                        pltpu.VMEM((tm, SGU_WIDTH), F32)],
        compiler_params=pltpu.CompilerParams(dimension_semantics=("arbitrary", "arbitrary"),
                                             vmem_limit_bytes=56 * 1024 * 1024),
        name="inproj_prompt",
    )(x, g1.reshape(1, d), w_in, ln_g.reshape(1, -1), ln_b.reshape(1, -1), sgu_w, sbx, sog.reshape(1, -1))


def _inproj_sample(x, g1, w_in, ln_g, ln_b, sgu_w, sgu_b, sog):
    n, d = x.shape
    a = ATTN_WIDTH
    w00 = jnp.repeat(sgu_w[:, 0, 0], SGU_GROUP_DIM).reshape(1, SGU_WIDTH)
    b0 = jnp.repeat(sgu_b[:, 0], SGU_GROUP_DIM).reshape(1, SGU_WIDTH)
    row = lambda w: pl.BlockSpec((n, w), lambda i: (0, 0))
    vec = _const_spec((1, SGU_WIDTH))
    out_shape = (jax.ShapeDtypeStruct((n, a), F32),) * 3 + (jax.ShapeDtypeStruct((n, SGU_WIDTH), BF16),
                                                           jax.ShapeDtypeStruct((n, SGU_WIDTH), F32))
    return pl.pallas_call(
        _inproj_sample_kernel,
        grid=(1,),
        in_specs=[row(d), _const_spec((1, d)), _const_spec(w_in.shape), vec, vec, vec, vec, vec],
        out_specs=(row(a), row(a), row(a), row(SGU_WIDTH), row(SGU_WIDTH)),
        out_shape=out_shape,
        scratch_shapes=[pltpu.VMEM(w_in.shape, BF16), pltpu.VMEM((d, 2 * a), BF16)],
        compiler_params=pltpu.CompilerParams(dimension_semantics=("arbitrary",),
                                             vmem_limit_bytes=56 * 1024 * 1024),
        name="inproj_sample",
    )(x, g1.reshape(1, d), w_in, ln_g.reshape(1, -1), ln_b.reshape(1, -1), w00, b0, sog.reshape(1, -1))


def _block_choice(gate_t, own):
    nb = gate_t.shape[0]
    blk = lax.broadcasted_iota(jnp.int32, gate_t.shape, 0)
    g = jnp.where(blk < own, gate_t, NEG_INF)
    picked = jnp.zeros(gate_t.shape, F32)
    for _ in range(MOBA_TOPK):
        top = jnp.max(g, axis=0, keepdims=True)
        first = jnp.min(jnp.where(g == top, blk, nb), axis=0, keepdims=True)
        hit = blk == first
        picked = jnp.where(hit, 1.0, picked)
        g = jnp.where(hit, -jnp.inf, g)
    return jnp.where(blk < own, picked, 0.0)


def _split3(x):
    hi = x.astype(BF16)
    r = x - hi.astype(F32)
    mid = r.astype(BF16)
    return hi, mid, (r - mid.astype(F32)).astype(BF16)


def _attn_prompt_kernel(q_ref, kb_ref, vtb_ref, km_ref, sl_ref, o_ref,
                        kmh_ref, kml_ref, kl_ref, qs_ref, sel_ref, acc_ref, m_ref, l_ref, s_ref, cm_ref):
    i = pl.program_id(1)
    blk = MOBA_BLOCK
    nb = kb_ref.shape[1]
    scale2 = LOG2E / math.sqrt(HEAD_DIM)

    @pl.when(i == 0)
    def _prepare():
        km = km_ref[:, 0, :]
        lane_head = lax.broadcasted_iota(jnp.int32, km.shape, 1) // HEAD_DIM
        for h in range(N_HEADS):
            mine = jnp.where(lane_head == h, km, 0.0)
            hi = mine.astype(BF16)
            kmh_ref[h * nb:(h + 1) * nb, :] = hi
            kml_ref[h * nb:(h + 1) * nb, :] = (mine - hi.astype(F32)).astype(BF16)
        lane = lax.broadcasted_iota(jnp.int32, (blk, LANES), 1)
        kl = lax.broadcasted_iota(jnp.int32, (blk, LANES), 0).astype(F32)
        kl_ref[...] = jnp.where(lane < 3, kl, 0.0).astype(BF16)
        row = lax.broadcasted_iota(jnp.int32, (LANES, blk), 0)
        for h in range(N_HEADS):
            hi, mid, lo = (x.astype(F32) for x in _split3(jnp.broadcast_to(sl_ref[h:h + 1, :], (LANES, blk))))
            terms = jnp.where(row == 0, hi, jnp.where(row == 1, mid, jnp.where(row == 2, lo, 0.0)))
            qs_ref[h, LANES:, :] = terms.astype(BF16)

    q_t = q_ref[0]
    q_hi = q_t.astype(BF16)
    q_lo = (q_t - q_hi.astype(F32)).astype(BF16)
    gate_t = _dot(kmh_ref[...], q_hi) + (_dot(kml_ref[...], q_hi) + _dot(kmh_ref[...], q_lo))
    row = lax.broadcasted_iota(jnp.int32, (LANES, blk), 0)
    for h in range(N_HEADS):
        pair, hh = divmod(h, HEADS_PER_TILE)
        q_pair = q_t[pair * LANES:(pair + 1) * LANES, :]
        mine = (row >= hh * HEAD_DIM) & (row < (hh + 1) * HEAD_DIM)
        qs_ref[h, :LANES, :] = (jnp.where(mine, q_pair, 0.0) * scale2).astype(BF16)

    def scores(j, h):
        pair = h // HEADS_PER_TILE
        keys = jnp.concatenate([kb_ref[0, j, :, pair * LANES:(pair + 1) * LANES], kl_ref[...]], axis=1)
        return _dot(keys, qs_ref[h])

    ones_rows = jnp.ones((BF16_SUBLANES, blk), BF16)

    def value_product(j, h, p):
        rows = jnp.concatenate([vtb_ref[0, j, h * HEAD_DIM:(h + 1) * HEAD_DIM, :], ones_rows], axis=0)
        pv = _dot(rows, p.astype(BF16))
        return pv[:HEAD_DIM], pv[HEAD_DIM:HEAD_DIM + 1]

    def issue_scores(j, slot, h):
        s = scores(j, h)
        s_ref[slot, h] = s
        cm_ref[slot, h] = jnp.max(s, axis=0, keepdims=True)

    for h in range(N_HEADS):
        s_ref[1, h] = scores(i, h)
    for h in range(N_HEADS):
        issue_scores(0, 0, h)
    for h in range(N_HEADS):
        sel_ref[h] = _block_choice(gate_t[h * nb:(h + 1) * nb, :], i)

    kl = lax.broadcasted_iota(jnp.int32, (blk, blk), 0)
    ql = lax.broadcasted_iota(jnp.int32, (blk, blk), 1)
    for h in range(N_HEADS):
        s = jnp.where(kl <= ql, s_ref[1, h], NEG_INF)
        m = jnp.max(s, axis=0, keepdims=True)
        m_ref[h] = m
        acc_ref[h], l_ref[h] = value_product(i, h, jnp.exp2(s - m))

    def finish_block(j, slot, h):
        m, l = m_ref[h], l_ref[h]
        c = sl_ref[h:h + 1, :] * ((j - i).astype(F32) * float(blk))
        chosen = sel_ref[h, pl.ds(j, 1), :] > 0.5
        m_new = jnp.where(chosen, jnp.maximum(m, cm_ref[slot, h] + c), m)
        alpha = jnp.exp2(m - m_new)
        shift = jnp.where(chosen, m_new - c, -NEG_INF)
        pv, p_sum = value_product(j, h, jnp.exp2(s_ref[slot, h] - shift))
        m_ref[h] = m_new
        l_ref[h] = alpha * l + p_sum
        acc_ref[h] = alpha * acc_ref[h] + pv

    def trip(t, carry):
        for slot in range(2):
            j = 2 * t + slot
            for h in range(N_HEADS):
                issue_scores(jnp.minimum(j + 1, nb - 1), 1 - slot, h)
                finish_block(j, slot, h)
        return carry

    lax.fori_loop(0, (i + 1) // 2, trip, 0)
    for h in range(N_HEADS):
        o_ref[0, h * HEAD_DIM:(h + 1) * HEAD_DIM, :] = acc_ref[h] / l_ref[h]


def _attn_prompt(q_t, kb, vtb, kmean):
    b, a, t = q_t.shape
    blk = MOBA_BLOCK
    nb = t // blk
    assert t % blk == 0 and a == ATTN_WIDTH and kb.shape == (b, nb, blk, a) and vtb.shape == (b, nb, a, blk)
    slopes2 = (_alibi_slopes(N_HEADS) * np.float32(LOG2E)).reshape(N_HEADS, 1)
    slopes2 = jnp.asarray(np.broadcast_to(slopes2, (N_HEADS, blk)).copy())
    once = pl.Buffered(1)
    return pl.pallas_call(
        _attn_prompt_kernel,
        grid=(b, nb),
        in_specs=[pl.BlockSpec((1, a, blk), lambda bi, i: (bi, 0, i)),
                  pl.BlockSpec((1, nb, blk, a), lambda bi, i: (bi, 0, 0, 0), pipeline_mode=once),
                  pl.BlockSpec((1, nb, a, blk), lambda bi, i: (bi, 0, 0, 0), pipeline_mode=once),
                  pl.BlockSpec((nb, 1, a), lambda bi, i: (bi, 0, 0)),
                  pl.BlockSpec((N_HEADS, blk), lambda bi, i: (0, 0))],
        out_specs=pl.BlockSpec((1, a, blk), lambda bi, i: (bi, 0, i)),
        out_shape=jax.ShapeDtypeStruct((b, a, t), F32),
        scratch_shapes=[pltpu.VMEM((N_HEADS * nb, a), BF16),
                        pltpu.VMEM((N_HEADS * nb, a), BF16),
                        pltpu.VMEM((blk, LANES), BF16),
                        pltpu.VMEM((N_HEADS, 2 * LANES, blk), BF16),
                        pltpu.VMEM((N_HEADS, nb, blk), F32),
                        pltpu.VMEM((N_HEADS, HEAD_DIM, blk), F32),
                        pltpu.VMEM((N_HEADS, 1, blk), F32),
                        pltpu.VMEM((N_HEADS, 1, blk), F32),
                        pltpu.VMEM((2, N_HEADS, blk, blk), F32),
                        pltpu.VMEM((2, N_HEADS, 1, blk), F32)],
        compiler_params=pltpu.CompilerParams(dimension_semantics=("arbitrary", "arbitrary"),
                                             vmem_limit_bytes=48 * 1024 * 1024),
        name="attn_prompt",
    )(q_t, kb, vtb, kmean, slopes2)


def _sample_attention(q_t, kn_t, vn_t, slopes, k_tile, v_tile, n_pages, page):
    past = n_pages * page
    pages_per_block = MOBA_BLOCK // page
    nb = past // MOBA_BLOCK
    scale = 1.0 / math.sqrt(HEAD_DIM)

    q_b = [jnp.broadcast_to(q_t[:, h:h + 1], (HEAD_DIM, page)) for h in range(N_HEADS)]

    def raw_scores(tile):
        return jnp.concatenate([jnp.sum(q_b[h] * tile(h), axis=0, keepdims=True) for h in range(N_HEADS)], axis=0)

    raw = [raw_scores(functools.partial(k_tile, pg)) for pg in range(n_pages)]
    gate = []
    for n in range(nb):
        tot = raw[n * pages_per_block]
        for r in range(1, pages_per_block):
            tot = tot + raw[n * pages_per_block + r]
        gate.append(jnp.sum(tot, axis=1, keepdims=True) * (1.0 / MOBA_BLOCK))
    chosen = []
    for n in range(nb):
        rank = jnp.where(NEG_INF > gate[n], 1.0, 0.0)
        for m in range(nb):
            if m != n:
                ahead = (gate[m] >= gate[n]) if m < n else (gate[m] > gate[n])
                rank = rank + jnp.where(ahead, 1.0, 0.0)
        chosen.append(jnp.where(rank < MOBA_TOPK, 1.0, 0.0))

    lane = lax.broadcasted_iota(jnp.int32, (N_HEADS, page), 1)
    dist0 = (past - lane).astype(F32)
    s_pages = []
    for pg in range(n_pages):
        s = raw[pg] * scale - slopes * (dist0 - float(pg * page))
        ok = jnp.broadcast_to(chosen[pg // pages_per_block], s.shape) > 0.5
        s_pages.append(jnp.where(ok, s, NEG_INF))
    own = raw_scores(lambda h: jnp.broadcast_to(kn_t[:, h:h + 1], (HEAD_DIM, page))) * scale
    s_pages.append(jnp.where(lane == 0, own, NEG_INF))
    m = jnp.max(s_pages[0], axis=1, keepdims=True)
    for s in s_pages[1:]:
        m = jnp.maximum(m, jnp.max(s, axis=1, keepdims=True))
    p_pages = [jnp.exp(s - m) for s in s_pages]
    l = jnp.sum(p_pages[0], axis=1, keepdims=True)
    for p in p_pages[1:]:
        l = l + jnp.sum(p, axis=1, keepdims=True)
    outs = []
    for h in range(N_HEADS):
        acc = jnp.broadcast_to(p_pages[n_pages][h:h + 1, :], (HEAD_DIM, page)) * \
            jnp.broadcast_to(vn_t[:, h:h + 1], (HEAD_DIM, page))
        for pg in range(n_pages):
            acc = acc + jnp.broadcast_to(p_pages[pg][h:h + 1, :], (HEAD_DIM, page)) * v_tile(pg, h)
        outs.append(jnp.sum(acc, axis=1, keepdims=True) / l[h:h + 1, :])
    return jnp.concatenate(outs, axis=1)


SEQS_PER_STEP = 2


def _out_proj(x_ref, a_ref, gn_ref, ag_ref, wo_ref, g2_ref):
    a = ATTN_WIDTH
    attn = a_ref[0].T if len(a_ref.shape) == 3 else a_ref[...]
    an = _rmsnorm(attn, ag_ref[...]).astype(BF16)
    x1 = x_ref[...] + (_dot(an, wo_ref[:a, :]) + _dot(gn_ref[...], wo_ref[a:, :]))
    return x1, _rmsnorm(x1, g2_ref[...]).astype(BF16)


def _ffn_half(hf, wg_ref, wu_ref, wd_ref, half):
    w = wg_ref.shape[1] // 2
    cols = slice(half * w, (half + 1) * w)
    gate = _dot(hf, wg_ref[:, cols])
    up = _dot(hf, wu_ref[:, cols])
    act = (gate * jax.nn.sigmoid(gate) * up).astype(BF16)
    return _dot(act, wd_ref[cols, :])


def _outffn_kernel(x_ref, a_ref, gn_ref, ag_ref, wo_ref, g2_ref, wg_ref, wu_ref, wd_ref, gf_ref, y_ref):
    x1, hf = _out_proj(x_ref, a_ref, gn_ref, ag_ref, wo_ref, g2_ref)
    ff = _ffn_half(hf, wg_ref, wu_ref, wd_ref, 0) + _ffn_half(hf, wg_ref, wu_ref, wd_ref, 1)
    y_ref[...] = _rmsnorm(x1 + ff, gf_ref[...])


def _outffn_attn_sample_kernel(pt_ref, x_ref, a_ref, gn_ref, ag_ref, wo_ref, g2_ref, wg_ref, wu_ref, wd_ref,
                               gf_ref, q_ref, kn_ref, vn_ref, sl_ref, ck_hbm, cv_hbm, y_ref, o_ref,
                               kbuf, vbuf, sem, x1_ref, hf_ref, ff_ref):
    t = pl.program_id(0)
    last_seq = SEQS_PER_STEP * pl.num_programs(0) - 1
    n_pages, page = kbuf.shape[1], kbuf.shape[-1]

    def page_copies(seq, slot):
        copies = []
        for pg in range(n_pages):
            pid = pt_ref[seq, pg]
            copies.append(pltpu.make_async_copy(ck_hbm.at[pid], kbuf.at[slot, pg], sem.at[slot, 0]))
            copies.append(pltpu.make_async_copy(cv_hbm.at[pid], vbuf.at[slot, pg], sem.at[slot, 1]))
        return copies

    def fetch(seq, slot):
        for c in page_copies(seq, slot):
            c.start()

    def wait(seq, slot):
        for c in page_copies(seq, slot):
            c.wait()

    def attend(slot):
        o_ref[slot] = _sample_attention(q_ref[slot], kn_ref[slot], vn_ref[slot], sl_ref[:, :1],
                                        lambda pg, h: kbuf[slot, pg, h], lambda pg, h: vbuf[slot, pg, h],
                                        n_pages, page)

    @pl.when(t == 0)
    def _first():
        fetch(0, 0)

    seq0 = SEQS_PER_STEP * t
    wait(seq0, 0)
    fetch(seq0 + 1, 1)
    attend(0)
    x1, hf = _out_proj(x_ref, a_ref, gn_ref, ag_ref, wo_ref, g2_ref)
    x1_ref[...] = x1
    hf_ref[...] = hf
    ff_ref[...] = _ffn_half(hf, wg_ref, wu_ref, wd_ref, 0)

    wait(seq0 + 1, 1)
    nxt = jnp.minimum(seq0 + 2, last_seq)
    fetch(nxt, 0)
    attend(1)
    ff = ff_ref[...] + _ffn_half(hf_ref[...], wg_ref, wu_ref, wd_ref, 1)
    y_ref[...] = _rmsnorm(x1_ref[...] + ff, gf_ref[...])

    @pl.when(t == pl.num_programs(0) - 1)
    def _drain():
        wait(last_seq, 0)


def _outffn(x, a, gn, ag, w_out, g2, w_gate, w_up, w_down, gf):
    n, d = x.shape
    assert (w_gate.shape[1] // 2) % LANES == 0
    row = lambda w: pl.BlockSpec((n, w), lambda i: (0, 0))
    return pl.pallas_call(
        _outffn_kernel,
        grid=(1,),
        in_specs=[row(d), row(ATTN_WIDTH), row(SGU_WIDTH), _const_spec((1, ATTN_WIDTH)), _const_spec(w_out.shape),
                  _const_spec((1, d)), _const_spec(w_gate.shape), _const_spec(w_up.shape),
                  _const_spec(w_down.shape), _const_spec((1, d))],
        out_specs=row(d),
        out_shape=jax.ShapeDtypeStruct((n, d), F32),
        compiler_params=pltpu.CompilerParams(dimension_semantics=("arbitrary",),
                                             vmem_limit_bytes=56 * 1024 * 1024),
        name="outffn_sample",
    )(x, a, gn, ag.reshape(1, -1), w_out, g2.reshape(1, -1), w_gate, w_up, w_down, gf.reshape(1, -1))


def _outffn_with_sample_attention(x, a_t, gn, ag, w_out, g2, w_gate, w_up, w_down, gf,
                                  q, k_new, v_new, cache_k, cache_v, page_table):
    n, d = x.shape
    n_seq, a = q.shape
    n_pool, page, h, dh = cache_k.shape
    n_pages = page_table.shape[1]
    assert h * dh == a and MOBA_BLOCK % page == 0 and (n_pages * page) % MOBA_BLOCK == 0
    assert n_seq % SEQS_PER_STEP == 0 and (w_gate.shape[1] // 2) % LANES == 0
    steps = n_seq // SEQS_PER_STEP
    assert n % steps == 0
    tm = n // steps
    per_seq = a_t.shape[2] // tm
    assert tm % LANES == 0 and a_t.shape[2] % tm == 0 and a_t.shape[0] * a_t.shape[2] == n
    slopes = jnp.asarray(np.broadcast_to(_alibi_slopes(N_HEADS).reshape(N_HEADS, 1), (N_HEADS, LANES)).copy())
    ck = jnp.transpose(cache_k, (0, 2, 3, 1))
    cv = jnp.transpose(cache_v, (0, 2, 3, 1))
    tok_t = lambda z: jnp.transpose(z.reshape(n_seq, h, dh), (0, 2, 1))
    row = lambda w: pl.BlockSpec((tm, w), lambda i, pt: (i, 0))
    tok = pl.BlockSpec((SEQS_PER_STEP, dh, h), lambda i, pt: (i, 0, 0))
    hbm = pl.BlockSpec(memory_space=pl.ANY)
    grid_spec = pltpu.PrefetchScalarGridSpec(
        num_scalar_prefetch=1,
        grid=(steps,),
        in_specs=[row(d), pl.BlockSpec((1, ATTN_WIDTH, tm), lambda i, pt: (i // per_seq, 0, i % per_seq)),
                  row(SGU_WIDTH), _const_spec((1, ATTN_WIDTH)), _const_spec(w_out.shape), _const_spec((1, d)),
                  _const_spec(w_gate.shape), _const_spec(w_up.shape), _const_spec(w_down.shape),
                  _const_spec((1, d)),
                  tok, tok, tok, pl.BlockSpec((N_HEADS, LANES), lambda i, pt: (0, 0)), hbm, hbm],
        out_specs=(row(d), tok),
        scratch_shapes=[pltpu.VMEM((SEQS_PER_STEP, n_pages, h, dh, page), F32),
                        pltpu.VMEM((SEQS_PER_STEP, n_pages, h, dh, page), F32),
                        pltpu.SemaphoreType.DMA((SEQS_PER_STEP, 2)),
                        pltpu.VMEM((tm, d), F32),
                        pltpu.VMEM((tm, d), BF16),
                        pltpu.VMEM((tm, d), F32)],
    )
    y, out = pl.pallas_call(
        _outffn_attn_sample_kernel,
        grid_spec=grid_spec,
        out_shape=(jax.ShapeDtypeStruct((n, d), F32), jax.ShapeDtypeStruct((n_seq, dh, h), F32)),
        compiler_params=pltpu.CompilerParams(dimension_semantics=("arbitrary",),
                                             vmem_limit_bytes=56 * 1024 * 1024),
        name="outffn_prompt_attn_sample",
    )(page_table, x, a_t, gn, ag.reshape(1, -1), w_out, g2.reshape(1, -1), w_gate, w_up, w_down,
      gf.reshape(1, -1), tok_t(q), tok_t(k_new), tok_t(v_new), slopes, ck, cv)
    return y, jnp.transpose(out, (0, 2, 1)).reshape(n_seq, a)


def kernel(x_prompt, x_sample, cache_k, cache_v, page_table, norm1_g, w_in, attn_out_g, sgu_ln_g, sgu_ln_b,
           sgu_w, sgu_b, sgu_out_g, w_out, norm2_g, w_gate, w_up, w_down, final_g):
    depth = w_in.shape[0]
    assert depth == 1, "single-layer stack"
    l = 0
    bsz, seq, d = x_prompt.shape
    dec_b, dec_seq, _ = x_sample.shape
    assert dec_seq == 1
    xs = x_sample.reshape(dec_b, d)

    wo, wg, wu, wd = (w[l].astype(BF16) for w in (w_out, w_gate, w_up, w_down))

    qp_t, kb, kt, vt, vtb, kmean, gnp = _inproj_prompt(x_prompt, norm1_g[l], w_in[l], sgu_ln_g[l], sgu_ln_b[l],
                                                     sgu_w[l], sgu_b[l], sgu_out_g[l], tm=512)
    ap_t = _attn_prompt(qp_t, kb, vtb, kmean)

    qs, ks, vs, gns, vns = _inproj_sample(xs, norm1_g[l], w_in[l], sgu_ln_g[l], sgu_ln_b[l], sgu_w[l], sgu_b[l],
                                          sgu_out_g[l])
    yp, a_s = _outffn_with_sample_attention(
        x_prompt.reshape(bsz * seq, d), ap_t, gnp.reshape(bsz * seq, -1), attn_out_g[l], wo, norm2_g[l],
        wg, wu, wd, final_g, qs, ks, vs, cache_k[l], cache_v[l], page_table)
    ys = _outffn(xs, a_s, gns, attn_out_g[l], wo, norm2_g[l], wg, wu, wd, final_g)

    heads_last = lambda x_t: jnp.transpose(x_t.reshape(bsz, N_HEADS, HEAD_DIM, seq), (0, 3, 1, 2))[None]
    y_prompt = yp.reshape(bsz, seq, d)
    y_sample = ys.reshape(dec_b, dec_seq, d)
    k_prompt = heads_last(kt)
    v_prompt = heads_last(vt)
    k_sample = ks.reshape(depth, dec_b, dec_seq, N_HEADS, HEAD_DIM)
    v_sample = vs.reshape(depth, dec_b, dec_seq, N_HEADS, HEAD_DIM)
    sgu_v_sample = vns.reshape(depth, dec_b, dec_seq, SGU_WIDTH)
    return (y_prompt, y_sample, k_prompt, v_prompt, k_sample, v_sample, sgu_v_sample)
```

```python
import functools
import math

import numpy as np
import jax
import jax.numpy as jnp
from jax import lax
from jax.experimental import pallas as pl
from jax.experimental.pallas import tpu as pltpu

HEAD_DIM = 64
N_HEADS = 8
ATTN_WIDTH = N_HEADS * HEAD_DIM
N_SGU_GROUPS = 8
SGU_GROUP_DIM = 64
SGU_WIDTH = N_SGU_GROUPS * SGU_GROUP_DIM
SGU_CHUNK = 128
MOBA_BLOCK = 256
MOBA_TOPK = 3
RMS_EPS = 1e-6
LN_EPS = 1e-5
NEG_INF = -1e30
LOG2E = 1.4426950408889634

LANES = 128
BF16_SUBLANES = 16
BIAS_LANES = BF16_SUBLANES
HEADS_PER_TILE = LANES // HEAD_DIM
N_HEAD_PAIRS = N_HEADS // HEADS_PER_TILE

F32 = jnp.float32
BF16 = jnp.bfloat16

_NT = (((1,), (1,)), ((), ()))
_NN = (((1,), (0,)), ((), ()))


def _dot(a, b, dims=_NN, precision=None):
    return lax.dot_general(a, b, dims, precision=precision, preferred_element_type=F32)


def _rmsnorm(x, g):
    r = lax.rsqrt(jnp.mean(x * x, axis=-1, keepdims=True) + RMS_EPS)
    return (x * r) * g


def _gelu_tanh(x):
    c = math.sqrt(2.0 / math.pi)
    return 0.5 * x * (1.0 + jnp.tanh(c * (x + 0.044715 * (x * x * x))))


def _alibi_slopes(n):
    start = 2.0 ** (-8.0 / n)
    return np.asarray([start ** (i + 1) for i in range(n)], dtype=np.float32)


def _split_weights(w_ref, whi_ref, wlo_ref, n_precise):
    rows = w_ref.shape[0]
    step = 128
    for r in range(0, rows, step):
        w = w_ref[r:r + step, :]
        hi = w.astype(BF16)
        whi_ref[r:r + step, :] = hi
        wlo_ref[r:r + step, :] = (w[:, :n_precise] - hi[:, :n_precise].astype(F32)).astype(BF16)


def _project(x, g1_ref, whi_ref, wlo_ref):
    n_precise = wlo_ref.shape[1]
    h = _rmsnorm(x, g1_ref[...])
    h_hi = h.astype(BF16)
    h_lo = (h - h_hi.astype(F32)).astype(BF16)
    w_qk = whi_ref[:, :n_precise]
    z_qk = _dot(h_hi, w_qk) + (_dot(h_lo, w_qk) + _dot(h_hi, wlo_ref[...]))
    z_rest = _dot(h_hi, whi_ref[:, n_precise:])
    return z_qk, z_rest


def _layernorm(x, g, b):
    mu = jnp.mean(x, axis=-1, keepdims=True)
    xc = x - mu
    var = jnp.mean(xc * xc, axis=-1, keepdims=True)
    return (xc * lax.rsqrt(var + LN_EPS)) * g + b


def _inproj_prompt_kernel(x_ref, g1_ref, w_ref, lng_ref, lnb_ref, sw_ref, sbx_ref, sog_ref,
                          q_ref, kb_ref, kt_ref, vt_ref, vtb_ref, kmean_ref, gn_ref,
                          wb_ref, wcat_ref, s_ref):
    tm = x_ref.shape[1]
    a = ATTN_WIDTH
    blk = MOBA_BLOCK

    @pl.when((pl.program_id(0) == 0) & (pl.program_id(1) == 0))
    def _prepare():
        for r in range(0, w_ref.shape[0], LANES):
            wb_ref[r:r + LANES, :] = w_ref[r:r + LANES, :].astype(BF16)
        t = lax.broadcasted_iota(jnp.int32, (SGU_CHUNK, SGU_CHUNK), 0)
        s = lax.broadcasted_iota(jnp.int32, (SGU_CHUNK, SGU_CHUNK), 1)
        causal = t >= s
        for gp in range(N_SGU_GROUPS // 2):
            w0 = jnp.where(causal, sw_ref[2 * gp], 0.0)
            w1 = jnp.where(causal, sw_ref[2 * gp + 1], 0.0)
            wcat_ref[gp] = jnp.concatenate([w0, w1], axis=1).astype(BF16)

    z = _dot(_rmsnorm(x_ref[0], g1_ref[...]).astype(BF16), wb_ref[...])
    q_ref[0] = z[:, :a].T
    zk = z[:, a:2 * a]
    zv = z[:, 2 * a:3 * a]
    z_rest = z[:, 2 * a:]
    kt_ref[0] = zk.T
    vt = zv.T
    vt_ref[0] = vt
    for r in range(tm // blk):
        rows = slice(r * blk, (r + 1) * blk)
        kb_ref[0, r] = zk[rows, :].astype(BF16)
        vtb_ref[0, r] = vt[:, rows].astype(BF16)
        kmean_ref[r] = jnp.mean(zk[rows, :], axis=0, keepdims=True)

    u = _gelu_tanh(z_rest[:, a:a + SGU_WIDTH])
    gv = _gelu_tanh(z_rest[:, a + SGU_WIDTH:])
    vn = _layernorm(gv, lng_ref[...], lnb_ref[...])

    lane = lax.broadcasted_iota(jnp.int32, (SGU_CHUNK, LANES), 1)
    low = lane < SGU_GROUP_DIM
    for c in range(tm // SGU_CHUNK):
        rows = slice(c * SGU_CHUNK, (c + 1) * SGU_CHUNK)
        for gp in range(N_SGU_GROUPS // 2):
            cols = slice(gp * LANES, (gp + 1) * LANES)
            vp = vn[rows, cols]
            rhs = jnp.concatenate([jnp.where(low, vp, 0.0), jnp.where(low, 0.0, vp)], axis=0).astype(BF16)
            s_ref[rows, cols] = _dot(wcat_ref[gp], rhs) + sbx_ref[:, cols]
    g = u * s_ref[...]
    gn_ref[0] = _rmsnorm(g, sog_ref[...]).astype(gn_ref.dtype)


def _inproj_sample_kernel(x_ref, g1_ref, w_ref, lng_ref, lnb_ref, w00_ref, b0_ref, sog_ref,
                          q_ref, k_ref, v_ref, gn_ref, vn_ref,
                          whi_ref, wlo_ref):
    a = ATTN_WIDTH

    @pl.when(pl.program_id(0) == 0)
    def _prepare():
        _split_weights(w_ref, whi_ref, wlo_ref, 2 * a)

    z_qk, z_rest = _project(x_ref[...], g1_ref, whi_ref, wlo_ref)
    q_ref[...] = z_qk[:, :a]
    k_ref[...] = z_qk[:, a:]
    v_ref[...] = z_rest[:, :a]
    u = _gelu_tanh(z_rest[:, a:a + SGU_WIDTH])
    gv = _gelu_tanh(z_rest[:, a + SGU_WIDTH:])
    vn = _layernorm(gv, lng_ref[...], lnb_ref[...])
    vn_ref[...] = vn
    g = u * (vn * w00_ref[...] + b0_ref[...])
    gn_ref[...] = _rmsnorm(g, sog_ref[...]).astype(gn_ref.dtype)


def _const_spec(shape):
    zeros = (0,) * len(shape)
    return pl.BlockSpec(shape, lambda *_: zeros, pipeline_mode=pl.Buffered(1))


def _inproj_prompt(x, g1, w_in, ln_g, ln_b, sgu_w, sgu_b, sog, *, tm):
    b, t, d = x.shape
    a = ATTN_WIDTH
    blk = MOBA_BLOCK
    assert t % tm == 0 and tm % SGU_CHUNK == 0 and tm % blk == 0
    nb = t // blk
    per = tm // blk
    sbx = jnp.repeat(sgu_b.T, SGU_GROUP_DIM, axis=1)
    row = lambda w: pl.BlockSpec((1, tm, w), lambda bi, i: (bi, i, 0))
    col = lambda w: pl.BlockSpec((1, w, tm), lambda bi, i: (bi, 0, i))
    out_specs = (col(a),
                 pl.BlockSpec((1, per, blk, a), lambda bi, i: (bi, i, 0, 0)),
                 col(a), col(a),
                 pl.BlockSpec((1, per, a, blk), lambda bi, i: (bi, i, 0, 0)),
                 pl.BlockSpec((per, 1, a), lambda bi, i: (bi * (nb // per) + i, 0, 0)),
                 row(SGU_WIDTH))
    out_shape = (jax.ShapeDtypeStruct((b, a, t), F32),
                 jax.ShapeDtypeStruct((b, nb, blk, a), BF16),
                 jax.ShapeDtypeStruct((b, a, t), F32),
                 jax.ShapeDtypeStruct((b, a, t), F32),
                 jax.ShapeDtypeStruct((b, nb, a, blk), BF16),
                 jax.ShapeDtypeStruct((b * nb, 1, a), F32),
                 jax.ShapeDtypeStruct((b, t, SGU_WIDTH), BF16))
    return pl.pallas_call(
        _inproj_prompt_kernel,
        grid=(b, t // tm),
        in_specs=[row(d), _const_spec((1, d)), _const_spec(w_in.shape), _const_spec((1, SGU_WIDTH)),
                  _const_spec((1, SGU_WIDTH)), _const_spec(sgu_w.shape), _const_spec(sbx.shape),
                  _const_spec((1, SGU_WIDTH))],
        out_specs=out_specs,
        out_shape=out_shape,
        scratch_shapes=[pltpu.VMEM(w_in.shape, BF16),
                        pltpu.VMEM((N_SGU_GROUPS // 2, SGU_CHUNK, 2 * SGU_CHUNK), BF16),
                        pltpu.VMEM((tm, SGU_WIDTH), F32)],
        compiler_params=pltpu.CompilerParams(dimension_semantics=("arbitrary", "arbitrary"),
                                             vmem_limit_bytes=56 * 1024 * 1024),
        name="inproj_prompt",
    )(x, g1.reshape(1, d), w_in, ln_g.reshape(1, -1), ln_b.reshape(1, -1), sgu_w, sbx, sog.reshape(1, -1))


def _inproj_sample(x, g1, w_in, ln_g, ln_b, sgu_w, sgu_b, sog):
    n, d = x.shape
    a = ATTN_WIDTH
    w00 = jnp.repeat(sgu_w[:, 0, 0], SGU_GROUP_DIM).reshape(1, SGU_WIDTH)
    b0 = jnp.repeat(sgu_b[:, 0], SGU_GROUP_DIM).reshape(1, SGU_WIDTH)
    row = lambda w: pl.BlockSpec((n, w), lambda i: (0, 0))
    vec = _const_spec((1, SGU_WIDTH))
    out_shape = (jax.ShapeDtypeStruct((n, a), F32),) * 3 + (jax.ShapeDtypeStruct((n, SGU_WIDTH), BF16),
                                                           jax.ShapeDtypeStruct((n, SGU_WIDTH), F32))
    return pl.pallas_call(
        _inproj_sample_kernel,
        grid=(1,),
        in_specs=[row(d), _const_spec((1, d)), _const_spec(w_in.shape), vec, vec, vec, vec, vec],
        out_specs=(row(a), row(a), row(a), row(SGU_WIDTH), row(SGU_WIDTH)),
        out_shape=out_shape,
        scratch_shapes=[pltpu.VMEM(w_in.shape, BF16), pltpu.VMEM((d, 2 * a), BF16)],
        compiler_params=pltpu.CompilerParams(dimension_semantics=("arbitrary",),
                                             vmem_limit_bytes=56 * 1024 * 1024),
        name="inproj_sample",
    )(x, g1.reshape(1, d), w_in, ln_g.reshape(1, -1), ln_b.reshape(1, -1), w00, b0, sog.reshape(1, -1))


def _block_choice(gate_t, own):
    nb = gate_t.shape[0]
    blk = lax.broadcasted_iota(jnp.int32, gate_t.shape, 0)
    g = jnp.where(blk < own, gate_t, NEG_INF)
    picked = jnp.zeros(gate_t.shape, F32)
    for _ in range(MOBA_TOPK):
        top = jnp.max(g, axis=0, keepdims=True)
        first = jnp.min(jnp.where(g == top, blk, nb), axis=0, keepdims=True)
        hit = blk == first
        picked = jnp.where(hit, 1.0, picked)
        g = jnp.where(hit, -jnp.inf, g)
    return jnp.where(blk < own, picked, 0.0)


def _split3(x):
    hi = x.astype(BF16)
    r = x - hi.astype(F32)
    mid = r.astype(BF16)
    return hi, mid, (r - mid.astype(F32)).astype(BF16)


def _attn_prompt_kernel(q_ref, kb_ref, vtb_ref, km_ref, sl_ref, o_ref,
                        kmh_ref, kml_ref, kl_ref, qs_ref, sel_ref, acc_ref, m_ref, l_ref, s_ref, cm_ref):
    i = pl.program_id(1)
    blk = MOBA_BLOCK
    nb = kb_ref.shape[1]
    scale2 = LOG2E / math.sqrt(HEAD_DIM)

    @pl.when(i == 0)
    def _prepare():
        km = km_ref[:, 0, :]
        lane_head = lax.broadcasted_iota(jnp.int32, km.shape, 1) // HEAD_DIM
        for h in range(N_HEADS):
            mine = jnp.where(lane_head == h, km, 0.0)
            hi = mine.astype(BF16)
            kmh_ref[h * nb:(h + 1) * nb, :] = hi
            kml_ref[h * nb:(h + 1) * nb, :] = (mine - hi.astype(F32)).astype(BF16)
        lane = lax.broadcasted_iota(jnp.int32, (blk, BIAS_LANES), 1)
        kl = lax.broadcasted_iota(jnp.int32, (blk, BIAS_LANES), 0).astype(F32)
        kl_ref[...] = jnp.where(lane < 3, kl, 0.0).astype(BF16)
        row = lax.broadcasted_iota(jnp.int32, (BIAS_LANES, blk), 0)
        for h in range(N_HEADS):
            hi, mid, lo = (x.astype(F32) for x in _split3(jnp.broadcast_to(sl_ref[h:h + 1, :], (BIAS_LANES, blk))))
            terms = jnp.where(row == 0, hi, jnp.where(row == 1, mid, jnp.where(row == 2, lo, 0.0)))
            qs_ref[h, LANES:, :] = terms.astype(BF16)

    q_t = q_ref[0]
    q_hi = q_t.astype(BF16)
    q_lo = (q_t - q_hi.astype(F32)).astype(BF16)
    gate_t = _dot(kmh_ref[...], q_hi) + (_dot(kml_ref[...], q_hi) + _dot(kmh_ref[...], q_lo))
    row = lax.broadcasted_iota(jnp.int32, (LANES, blk), 0)
    for h in range(N_HEADS):
        pair, hh = divmod(h, HEADS_PER_TILE)
        q_pair = q_t[pair * LANES:(pair + 1) * LANES, :]
        mine = (row >= hh * HEAD_DIM) & (row < (hh + 1) * HEAD_DIM)
        qs_ref[h, :LANES, :] = (jnp.where(mine, q_pair, 0.0) * scale2).astype(BF16)

    def scores(j, h):
        pair = h // HEADS_PER_TILE
        keys = jnp.concatenate([kb_ref[0, j, :, pair * LANES:(pair + 1) * LANES], kl_ref[...]], axis=1)
        return _dot(keys, qs_ref[h])

    ones_rows = jnp.ones((BF16_SUBLANES, blk), BF16)

    def value_product(j, h, p):
        rows = jnp.concatenate([vtb_ref[0, j, h * HEAD_DIM:(h + 1) * HEAD_DIM, :], ones_rows], axis=0)
        pv = _dot(rows, p.astype(BF16))
        return pv[:HEAD_DIM], pv[HEAD_DIM:HEAD_DIM + 1]

    def issue_scores(j, slot, h):
        s = scores(j, h)
        s_ref[slot, h] = s
        cm_ref[slot, h] = jnp.max(s, axis=0, keepdims=True)

    for h in range(N_HEADS):
        s_ref[1, h] = scores(i, h)
    for h in range(N_HEADS):
        issue_scores(0, 0, h)
    for h in range(N_HEADS):
        sel_ref[h] = _block_choice(gate_t[h * nb:(h + 1) * nb, :], i)

    kl = lax.broadcasted_iota(jnp.int32, (blk, blk), 0)
    ql = lax.broadcasted_iota(jnp.int32, (blk, blk), 1)
    for h in range(N_HEADS):
        s = jnp.where(kl <= ql, s_ref[1, h], NEG_INF)
        m = jnp.max(s, axis=0, keepdims=True)
        m_ref[h] = m
        acc_ref[h], l_ref[h] = value_product(i, h, jnp.exp2(s - m))

    def finish_block(j, slot, h):
        m, l = m_ref[h], l_ref[h]
        c = sl_ref[h:h + 1, :] * ((j - i).astype(F32) * float(blk))
        chosen = sel_ref[h, pl.ds(j, 1), :] > 0.5
        m_new = jnp.where(chosen, jnp.maximum(m, cm_ref[slot, h] + c), m)
        alpha = jnp.exp2(m - m_new)
        shift = jnp.where(chosen, m_new - c, -NEG_INF)
        pv, p_sum = value_product(j, h, jnp.exp2(s_ref[slot, h] - shift))
        m_ref[h] = m_new
        l_ref[h] = alpha * l + p_sum
        acc_ref[h] = alpha * acc_ref[h] + pv

    def trip(t, carry):
        for slot in range(2):
            j = 2 * t + slot
            for h in range(N_HEADS):
                issue_scores(jnp.minimum(j + 1, nb - 1), 1 - slot, h)
                finish_block(j, slot, h)
        return carry

    lax.fori_loop(0, (i + 1) // 2, trip, 0)
    for h in range(N_HEADS):
        o_ref[0, h * HEAD_DIM:(h + 1) * HEAD_DIM, :] = acc_ref[h] / l_ref[h]


def _attn_prompt(q_t, kb, vtb, kmean):
    b, a, t = q_t.shape
    blk = MOBA_BLOCK
    nb = t // blk
    assert t % blk == 0 and a == ATTN_WIDTH and kb.shape == (b, nb, blk, a) and vtb.shape == (b, nb, a, blk)
    slopes2 = (_alibi_slopes(N_HEADS) * np.float32(LOG2E)).reshape(N_HEADS, 1)
    slopes2 = jnp.asarray(np.broadcast_to(slopes2, (N_HEADS, blk)).copy())
    once = pl.Buffered(1)
    return pl.pallas_call(
        _attn_prompt_kernel,
        grid=(b, nb),
        in_specs=[pl.BlockSpec((1, a, blk), lambda bi, i: (bi, 0, i)),
                  pl.BlockSpec((1, nb, blk, a), lambda bi, i: (bi, 0, 0, 0), pipeline_mode=once),
                  pl.BlockSpec((1, nb, a, blk), lambda bi, i: (bi, 0, 0, 0), pipeline_mode=once),
                  pl.BlockSpec((nb, 1, a), lambda bi, i: (bi, 0, 0)),
                  pl.BlockSpec((N_HEADS, blk), lambda bi, i: (0, 0))],
        out_specs=pl.BlockSpec((1, a, blk), lambda bi, i: (bi, 0, i)),
        out_shape=jax.ShapeDtypeStruct((b, a, t), F32),
        scratch_shapes=[pltpu.VMEM((N_HEADS * nb, a), BF16),
                        pltpu.VMEM((N_HEADS * nb, a), BF16),
                        pltpu.VMEM((blk, BIAS_LANES), BF16),
                        pltpu.VMEM((N_HEADS, LANES + BIAS_LANES, blk), BF16),
                        pltpu.VMEM((N_HEADS, nb, blk), F32),
                        pltpu.VMEM((N_HEADS, HEAD_DIM, blk), F32),
                        pltpu.VMEM((N_HEADS, 1, blk), F32),
                        pltpu.VMEM((N_HEADS, 1, blk), F32),
                        pltpu.VMEM((2, N_HEADS, blk, blk), F32),
                        pltpu.VMEM((2, N_HEADS, 1, blk), F32)],
        compiler_params=pltpu.CompilerParams(dimension_semantics=("arbitrary", "arbitrary"),
                                             vmem_limit_bytes=48 * 1024 * 1024),
        name="attn_prompt",
    )(q_t, kb, vtb, kmean, slopes2)


def _sample_attention(q_t, kn_t, vn_t, slopes, k_tile, v_tile, n_pages, page):
    past = n_pages * page
    pages_per_block = MOBA_BLOCK // page
    nb = past // MOBA_BLOCK
    scale = 1.0 / math.sqrt(HEAD_DIM)

    q_b = [jnp.broadcast_to(q_t[:, h:h + 1], (HEAD_DIM, page)) for h in range(N_HEADS)]

    def raw_scores(tile):
        return jnp.concatenate([jnp.sum(q_b[h] * tile(h), axis=0, keepdims=True) for h in range(N_HEADS)], axis=0)

    raw = [raw_scores(functools.partial(k_tile, pg)) for pg in range(n_pages)]
    gate = []
    for n in range(nb):
        tot = raw[n * pages_per_block]
        for r in range(1, pages_per_block):
            tot = tot + raw[n * pages_per_block + r]
        gate.append(jnp.sum(tot, axis=1, keepdims=True) * (1.0 / MOBA_BLOCK))
    chosen = []
    for n in range(nb):
        rank = jnp.where(NEG_INF > gate[n], 1.0, 0.0)
        for m in range(nb):
            if m != n:
                ahead = (gate[m] >= gate[n]) if m < n else (gate[m] > gate[n])
                rank = rank + jnp.where(ahead, 1.0, 0.0)
        chosen.append(jnp.where(rank < MOBA_TOPK, 1.0, 0.0))

    lane = lax.broadcasted_iota(jnp.int32, (N_HEADS, page), 1)
    dist0 = (past - lane).astype(F32)
    s_pages = []
    for pg in range(n_pages):
        s = raw[pg] * scale - slopes * (dist0 - float(pg * page))
        ok = jnp.broadcast_to(chosen[pg // pages_per_block], s.shape) > 0.5
        s_pages.append(jnp.where(ok, s, NEG_INF))
    own = raw_scores(lambda h: jnp.broadcast_to(kn_t[:, h:h + 1], (HEAD_DIM, page))) * scale
    s_pages.append(jnp.where(lane == 0, own, NEG_INF))
    m = jnp.max(s_pages[0], axis=1, keepdims=True)
    for s in s_pages[1:]:
        m = jnp.maximum(m, jnp.max(s, axis=1, keepdims=True))
    p_pages = [jnp.exp(s - m) for s in s_pages]
    l = jnp.sum(p_pages[0], axis=1, keepdims=True)
    for p in p_pages[1:]:
        l = l + jnp.sum(p, axis=1, keepdims=True)
    outs = []
    for h in range(N_HEADS):
        acc = jnp.broadcast_to(p_pages[n_pages][h:h + 1, :], (HEAD_DIM, page)) * \
            jnp.broadcast_to(vn_t[:, h:h + 1], (HEAD_DIM, page))
        for pg in range(n_pages):
            acc = acc + jnp.broadcast_to(p_pages[pg][h:h + 1, :], (HEAD_DIM, page)) * v_tile(pg, h)
        outs.append(jnp.sum(acc, axis=1, keepdims=True) / l[h:h + 1, :])
    return jnp.concatenate(outs, axis=1)


SEQS_PER_STEP = 2


def _out_proj(x_ref, a_ref, gn_ref, ag_ref, wo_ref, g2_ref):
    a = ATTN_WIDTH
    attn = a_ref[0].T if len(a_ref.shape) == 3 else a_ref[...]
    an = _rmsnorm(attn, ag_ref[...]).astype(BF16)
    x1 = x_ref[...] + (_dot(an, wo_ref[:a, :]) + _dot(gn_ref[...], wo_ref[a:, :]))
    return x1, _rmsnorm(x1, g2_ref[...]).astype(BF16)


def _ffn_half(hf, wg_ref, wu_ref, wd_ref, half):
    w = wg_ref.shape[1] // 2
    cols = slice(half * w, (half + 1) * w)
    gate = _dot(hf, wg_ref[:, cols])
    up = _dot(hf, wu_ref[:, cols])
    act = (gate * jax.nn.sigmoid(gate) * up).astype(BF16)
    return _dot(act, wd_ref[cols, :])


def _outffn_kernel(x_ref, a_ref, gn_ref, ag_ref, wo_ref, g2_ref, wg_ref, wu_ref, wd_ref, gf_ref, y_ref):
    x1, hf = _out_proj(x_ref, a_ref, gn_ref, ag_ref, wo_ref, g2_ref)
    ff = _ffn_half(hf, wg_ref, wu_ref, wd_ref, 0) + _ffn_half(hf, wg_ref, wu_ref, wd_ref, 1)
    y_ref[...] = _rmsnorm(x1 + ff, gf_ref[...])


def _outffn_attn_sample_kernel(pt_ref, x_ref, a_ref, gn_ref, ag_ref, wo_ref, g2_ref, wg_ref, wu_ref, wd_ref,
                               gf_ref, q_ref, kn_ref, vn_ref, sl_ref, ck_hbm, cv_hbm, y_ref, o_ref,
                               kbuf, vbuf, sem, x1_ref, hf_ref, ff_ref):
    t = pl.program_id(0)
    last_seq = SEQS_PER_STEP * pl.num_programs(0) - 1
    n_pages, page = kbuf.shape[1], kbuf.shape[-1]

    def page_copies(seq, slot):
        copies = []
        for pg in range(n_pages):
            pid = pt_ref[seq, pg]
            copies.append(pltpu.make_async_copy(ck_hbm.at[pid], kbuf.at[slot, pg], sem.at[slot, 0]))
            copies.append(pltpu.make_async_copy(cv_hbm.at[pid], vbuf.at[slot, pg], sem.at[slot, 1]))
        return copies

    def fetch(seq, slot):
        for c in page_copies(seq, slot):
            c.start()

    def wait(seq, slot):
        for c in page_copies(seq, slot):
            c.wait()

    def attend(slot):
        o_ref[slot] = _sample_attention(q_ref[slot], kn_ref[slot], vn_ref[slot], sl_ref[:, :1],
                                        lambda pg, h: kbuf[slot, pg, h], lambda pg, h: vbuf[slot, pg, h],
                                        n_pages, page)

    @pl.when(t == 0)
    def _first():
        fetch(0, 0)

    seq0 = SEQS_PER_STEP * t
    wait(seq0, 0)
    fetch(seq0 + 1, 1)
    attend(0)
    x1, hf = _out_proj(x_ref, a_ref, gn_ref, ag_ref, wo_ref, g2_ref)
    x1_ref[...] = x1
    hf_ref[...] = hf
    ff_ref[...] = _ffn_half(hf, wg_ref, wu_ref, wd_ref, 0)

    wait(seq0 + 1, 1)
    nxt = jnp.minimum(seq0 + 2, last_seq)
    fetch(nxt, 0)
    attend(1)
    ff = ff_ref[...] + _ffn_half(hf_ref[...], wg_ref, wu_ref, wd_ref, 1)
    y_ref[...] = _rmsnorm(x1_ref[...] + ff, gf_ref[...])

    @pl.when(t == pl.num_programs(0) - 1)
    def _drain():
        wait(last_seq, 0)


def _outffn(x, a, gn, ag, w_out, g2, w_gate, w_up, w_down, gf):
    n, d = x.shape
    assert (w_gate.shape[1] // 2) % LANES == 0
    row = lambda w: pl.BlockSpec((n, w), lambda i: (0, 0))
    return pl.pallas_call(
        _outffn_kernel,
        grid=(1,),
        in_specs=[row(d), row(ATTN_WIDTH), row(SGU_WIDTH), _const_spec((1, ATTN_WIDTH)), _const_spec(w_out.shape),
                  _const_spec((1, d)), _const_spec(w_gate.shape), _const_spec(w_up.shape),
                  _const_spec(w_down.shape), _const_spec((1, d))],
        out_specs=row(d),
        out_shape=jax.ShapeDtypeStruct((n, d), F32),
        compiler_params=pltpu.CompilerParams(dimension_semantics=("arbitrary",),
                                             vmem_limit_bytes=56 * 1024 * 1024),
        name="outffn_sample",
    )(x, a, gn, ag.reshape(1, -1), w_out, g2.reshape(1, -1), w_gate, w_up, w_down, gf.reshape(1, -1))


def _outffn_with_sample_attention(x, a_t, gn, ag, w_out, g2, w_gate, w_up, w_down, gf,
                                  q, k_new, v_new, cache_k, cache_v, page_table):
    n, d = x.shape
    n_seq, a = q.shape
    n_pool, page, h, dh = cache_k.shape
    n_pages = page_table.shape[1]
    assert h * dh == a and MOBA_BLOCK % page == 0 and (n_pages * page) % MOBA_BLOCK == 0
    assert n_seq % SEQS_PER_STEP == 0 and (w_gate.shape[1] // 2) % LANES == 0
    steps = n_seq // SEQS_PER_STEP
    assert n % steps == 0
    tm = n // steps
    per_seq = a_t.shape[2] // tm
    assert tm % LANES == 0 and a_t.shape[2] % tm == 0 and a_t.shape[0] * a_t.shape[2] == n
    slopes = jnp.asarray(np.broadcast_to(_alibi_slopes(N_HEADS).reshape(N_HEADS, 1), (N_HEADS, LANES)).copy())
    ck = jnp.transpose(cache_k, (0, 2, 3, 1))
    cv = jnp.transpose(cache_v, (0, 2, 3, 1))
    tok_t = lambda z: jnp.transpose(z.reshape(n_seq, h, dh), (0, 2, 1))
    row = lambda w: pl.BlockSpec((tm, w), lambda i, pt: (i, 0))
    tok = pl.BlockSpec((SEQS_PER_STEP, dh, h), lambda i, pt: (i, 0, 0))
    hbm = pl.BlockSpec(memory_space=pl.ANY)
    grid_spec = pltpu.PrefetchScalarGridSpec(
        num_scalar_prefetch=1,
        grid=(steps,),
        in_specs=[row(d), pl.BlockSpec((1, ATTN_WIDTH, tm), lambda i, pt: (i // per_seq, 0, i % per_seq)),
                  row(SGU_WIDTH), _const_spec((1, ATTN_WIDTH)), _const_spec(w_out.shape), _const_spec((1, d)),
                  _const_spec(w_gate.shape), _const_spec(w_up.shape), _const_spec(w_down.shape),
                  _const_spec((1, d)),
                  tok, tok, tok, pl.BlockSpec((N_HEADS, LANES), lambda i, pt: (0, 0)), hbm, hbm],
        out_specs=(row(d), tok),
        scratch_shapes=[pltpu.VMEM((SEQS_PER_STEP, n_pages, h, dh, page), F32),
                        pltpu.VMEM((SEQS_PER_STEP, n_pages, h, dh, page), F32),
                        pltpu.SemaphoreType.DMA((SEQS_PER_STEP, 2)),
                        pltpu.VMEM((tm, d), F32),
                        pltpu.VMEM((tm, d), BF16),
                        pltpu.VMEM((tm, d), F32)],
    )
    y, out = pl.pallas_call(
        _outffn_attn_sample_kernel,
        grid_spec=grid_spec,
        out_shape=(jax.ShapeDtypeStruct((n, d), F32), jax.ShapeDtypeStruct((n_seq, dh, h), F32)),
        compiler_params=pltpu.CompilerParams(dimension_semantics=("arbitrary",),
                                             vmem_limit_bytes=56 * 1024 * 1024),
        name="outffn_prompt_attn_sample",
    )(page_table, x, a_t, gn, ag.reshape(1, -1), w_out, g2.reshape(1, -1), w_gate, w_up, w_down,
      gf.reshape(1, -1), tok_t(q), tok_t(k_new), tok_t(v_new), slopes, ck, cv)
    return y, jnp.transpose(out, (0, 2, 1)).reshape(n_seq, a)


def kernel(x_prompt, x_sample, cache_k, cache_v, page_table, norm1_g, w_in, attn_out_g, sgu_ln_g, sgu_ln_b,
           sgu_w, sgu_b, sgu_out_g, w_out, norm2_g, w_gate, w_up, w_down, final_g):
    depth = w_in.shape[0]
    assert depth == 1, "single-layer stack"
    l = 0
    bsz, seq, d = x_prompt.shape
    dec_b, dec_seq, _ = x_sample.shape
    assert dec_seq == 1
    xs = x_sample.reshape(dec_b, d)

    wo, wg, wu, wd = (w[l].astype(BF16) for w in (w_out, w_gate, w_up, w_down))

    qp_t, kb, kt, vt, vtb, kmean, gnp = _inproj_prompt(x_prompt, norm1_g[l], w_in[l], sgu_ln_g[l], sgu_ln_b[l],
                                                     sgu_w[l], sgu_b[l], sgu_out_g[l], tm=512)
    ap_t = _attn_prompt(qp_t, kb, vtb, kmean)

    qs, ks, vs, gns, vns = _inproj_sample(xs, norm1_g[l], w_in[l], sgu_ln_g[l], sgu_ln_b[l], sgu_w[l], sgu_b[l],
                                          sgu_out_g[l])
    yp, a_s = _outffn_with_sample_attention(
        x_prompt.reshape(bsz * seq, d), ap_t, gnp.reshape(bsz * seq, -1), attn_out_g[l], wo, norm2_g[l],
        wg, wu, wd, final_g, qs, ks, vs, cache_k[l], cache_v[l], page_table)
    ys = _outffn(xs, a_s, gns, attn_out_g[l], wo, norm2_g[l], wg, wu, wd, final_g)

    heads_last = lambda x_t: jnp.transpose(x_t.reshape(bsz, N_HEADS, HEAD_DIM, seq), (0, 3, 1, 2))[None]
    y_prompt = yp.reshape(bsz, seq, d)
    y_sample = ys.reshape(dec_b, dec_seq, d)
    k_prompt = heads_last(kt)
    v_prompt = heads_last(vt)
    k_sample = ks.reshape(depth, dec_b, dec_seq, N_HEADS, HEAD_DIM)
    v_sample = vs.reshape(depth, dec_b, dec_seq, N_HEADS, HEAD_DIM)
    sgu_v_sample = vns.reshape(depth, dec_b, dec_seq, SGU_WIDTH)
    return (y_prompt, y_sample, k_prompt, v_prompt, k_sample, v_sample, sgu_v_sample)
```

```python
import functools
import math

import numpy as np
import jax
import jax.numpy as jnp
from jax import lax
from jax.experimental import pallas as pl
from jax.experimental.pallas import tpu as pltpu

HEAD_DIM = 64
N_HEADS = 8
ATTN_WIDTH = N_HEADS * HEAD_DIM
N_SGU_GROUPS = 8
SGU_GROUP_DIM = 64
SGU_WIDTH = N_SGU_GROUPS * SGU_GROUP_DIM
SGU_CHUNK = 128
MOBA_BLOCK = 256
MOBA_TOPK = 3
RMS_EPS = 1e-6
LN_EPS = 1e-5
NEG_INF = -1e30
LOG2E = 1.4426950408889634

LANES = 128
BF16_SUBLANES = 16
BIAS_LANES = BF16_SUBLANES
HEADS_PER_TILE = LANES // HEAD_DIM
N_HEAD_PAIRS = N_HEADS // HEADS_PER_TILE

F32 = jnp.float32
BF16 = jnp.bfloat16

_NT = (((1,), (1,)), ((), ()))
_NN = (((1,), (0,)), ((), ()))


def _dot(a, b, dims=_NN, precision=None):
    return lax.dot_general(a, b, dims, precision=precision, preferred_element_type=F32)


def _rmsnorm(x, g):
    r = lax.rsqrt(jnp.mean(x * x, axis=-1, keepdims=True) + RMS_EPS)
    return (x * r) * g


def _gelu_tanh(x):
    c = math.sqrt(2.0 / math.pi)
    return 0.5 * x * (1.0 + jnp.tanh(c * (x + 0.044715 * (x * x * x))))


def _alibi_slopes(n):
    start = 2.0 ** (-8.0 / n)
    return np.asarray([start ** (i + 1) for i in range(n)], dtype=np.float32)


def _split_weights(w_ref, whi_ref, wlo_ref, n_precise):
    rows = w_ref.shape[0]
    step = 128
    for r in range(0, rows, step):
        w = w_ref[r:r + step, :]
        hi = w.astype(BF16)
        whi_ref[r:r + step, :] = hi
        wlo_ref[r:r + step, :] = (w[:, :n_precise] - hi[:, :n_precise].astype(F32)).astype(BF16)


def _project(x, g1_ref, whi_ref, wlo_ref):
    n_precise = wlo_ref.shape[1]
    h = _rmsnorm(x, g1_ref[...])
    h_hi = h.astype(BF16)
    h_lo = (h - h_hi.astype(F32)).astype(BF16)
    w_qk = whi_ref[:, :n_precise]
    z_qk = _dot(h_hi, w_qk) + (_dot(h_lo, w_qk) + _dot(h_hi, wlo_ref[...]))
    z_rest = _dot(h_hi, whi_ref[:, n_precise:])
    return z_qk, z_rest


def _layernorm(x, g, b):
    mu = jnp.mean(x, axis=-1, keepdims=True)
    xc = x - mu
    var = jnp.mean(xc * xc, axis=-1, keepdims=True)
    return (xc * lax.rsqrt(var + LN_EPS)) * g + b


def _inproj_prompt_kernel(x_ref, g1_ref, w_ref, lng_ref, lnb_ref, sw_ref, sbx_ref, sog_ref, *refs):
    later_w = refs[:N_LATER_WEIGHTS]
    q_ref, kb_ref, kt_ref, vt_ref, vtb_ref, kmean_ref, gn_ref = refs[N_LATER_WEIGHTS:N_LATER_WEIGHTS + 7]
    later_wb = refs[N_LATER_WEIGHTS + 7:2 * N_LATER_WEIGHTS + 7]
    wb_ref, wcat_ref, s_ref = refs[2 * N_LATER_WEIGHTS + 7:]
    tm = x_ref.shape[1]
    a = ATTN_WIDTH
    blk = MOBA_BLOCK

    for src, dst in zip(later_w, later_wb):
        dst[...] = src[...].astype(BF16)

    @pl.when((pl.program_id(0) == 0) & (pl.program_id(1) == 0))
    def _prepare():
        for r in range(0, w_ref.shape[0], LANES):
            wb_ref[r:r + LANES, :] = w_ref[r:r + LANES, :].astype(BF16)
        t = lax.broadcasted_iota(jnp.int32, (SGU_CHUNK, SGU_CHUNK), 0)
        s = lax.broadcasted_iota(jnp.int32, (SGU_CHUNK, SGU_CHUNK), 1)
        causal = t >= s
        for gp in range(N_SGU_GROUPS // 2):
            w0 = jnp.where(causal, sw_ref[2 * gp], 0.0)
            w1 = jnp.where(causal, sw_ref[2 * gp + 1], 0.0)
            wcat_ref[gp] = jnp.concatenate([w0, w1], axis=1).astype(BF16)

    z = _dot(_rmsnorm(x_ref[0], g1_ref[...]).astype(BF16), wb_ref[...])
    q_ref[0] = z[:, :a].T
    zk = z[:, a:2 * a]
    zv = z[:, 2 * a:3 * a]
    z_rest = z[:, 2 * a:]
    kt_ref[0] = zk.T
    vt = zv.T
    vt_ref[0] = vt
    for r in range(tm // blk):
        rows = slice(r * blk, (r + 1) * blk)
        kb_ref[0, r] = zk[rows, :].astype(BF16)
        vtb_ref[0, r] = vt[:, rows].astype(BF16)
        kmean_ref[r] = jnp.mean(zk[rows, :], axis=0, keepdims=True)

    u = _gelu_tanh(z_rest[:, a:a + SGU_WIDTH])
    gv = _gelu_tanh(z_rest[:, a + SGU_WIDTH:])
    vn = _layernorm(gv, lng_ref[...], lnb_ref[...])

    lane = lax.broadcasted_iota(jnp.int32, (SGU_CHUNK, LANES), 1)
    low = lane < SGU_GROUP_DIM
    for c in range(tm // SGU_CHUNK):
        rows = slice(c * SGU_CHUNK, (c + 1) * SGU_CHUNK)
        for gp in range(N_SGU_GROUPS // 2):
            cols = slice(gp * LANES, (gp + 1) * LANES)
            vp = vn[rows, cols]
            rhs = jnp.concatenate([jnp.where(low, vp, 0.0), jnp.where(low, 0.0, vp)], axis=0).astype(BF16)
            s_ref[rows, cols] = _dot(wcat_ref[gp], rhs) + sbx_ref[:, cols]
    g = u * s_ref[...]
    gn_ref[0] = _rmsnorm(g, sog_ref[...]).astype(gn_ref.dtype)


def _inproj_sample_kernel(x_ref, g1_ref, w_ref, lng_ref, lnb_ref, w00_ref, b0_ref, sog_ref,
                          q_ref, k_ref, v_ref, gn_ref, vn_ref,
                          whi_ref, wlo_ref):
    a = ATTN_WIDTH

    @pl.when(pl.program_id(0) == 0)
    def _prepare():
        _split_weights(w_ref, whi_ref, wlo_ref, 2 * a)

    z_qk, z_rest = _project(x_ref[...], g1_ref, whi_ref, wlo_ref)
    q_ref[...] = z_qk[:, :a]
    k_ref[...] = z_qk[:, a:]
    v_ref[...] = z_rest[:, :a]
    u = _gelu_tanh(z_rest[:, a:a + SGU_WIDTH])
    gv = _gelu_tanh(z_rest[:, a + SGU_WIDTH:])
    vn = _layernorm(gv, lng_ref[...], lnb_ref[...])
    vn_ref[...] = vn
    g = u * (vn * w00_ref[...] + b0_ref[...])
    gn_ref[...] = _rmsnorm(g, sog_ref[...]).astype(gn_ref.dtype)


def _const_spec(shape):
    zeros = (0,) * len(shape)
    return pl.BlockSpec(shape, lambda *_: zeros, pipeline_mode=pl.Buffered(1))


N_LATER_WEIGHTS = 4


def _inproj_prompt(x, g1, w_in, ln_g, ln_b, sgu_w, sgu_b, sog, later_weights, *, tm):
    b, t, d = x.shape
    a = ATTN_WIDTH
    blk = MOBA_BLOCK
    assert t % tm == 0 and tm % SGU_CHUNK == 0 and tm % blk == 0 and len(later_weights) == N_LATER_WEIGHTS
    nb = t // blk
    per = tm // blk
    steps = t // tm
    assert all(w.shape[0] % (steps * BF16_SUBLANES) == 0 and w.shape[1] % (b * LANES) == 0 for w in later_weights)
    w_spec = lambda w: pl.BlockSpec((w.shape[0] // steps, w.shape[1] // b), lambda bi, i: (i, bi))
    sbx = jnp.repeat(sgu_b.T, SGU_GROUP_DIM, axis=1)
    row = lambda w: pl.BlockSpec((1, tm, w), lambda bi, i: (bi, i, 0))
    col = lambda w: pl.BlockSpec((1, w, tm), lambda bi, i: (bi, 0, i))
    out_specs = (col(a),
                 pl.BlockSpec((1, per, blk, a), lambda bi, i: (bi, i, 0, 0)),
                 col(a), col(a),
                 pl.BlockSpec((1, per, a, blk), lambda bi, i: (bi, i, 0, 0)),
                 pl.BlockSpec((per, 1, a), lambda bi, i: (bi * (nb // per) + i, 0, 0)),
                 row(SGU_WIDTH)) + tuple(w_spec(w) for w in later_weights)
    out_shape = (jax.ShapeDtypeStruct((b, a, t), F32),
                 jax.ShapeDtypeStruct((b, nb, blk, a), BF16),
                 jax.ShapeDtypeStruct((b, a, t), F32),
                 jax.ShapeDtypeStruct((b, a, t), F32),
                 jax.ShapeDtypeStruct((b, nb, a, blk), BF16),
                 jax.ShapeDtypeStruct((b * nb, 1, a), F32),
                 jax.ShapeDtypeStruct((b, t, SGU_WIDTH), BF16)) + tuple(
                     jax.ShapeDtypeStruct(w.shape, BF16) for w in later_weights)
    return pl.pallas_call(
        _inproj_prompt_kernel,
        grid=(b, steps),
        in_specs=[row(d), _const_spec((1, d)), _const_spec(w_in.shape), _const_spec((1, SGU_WIDTH)),
                  _const_spec((1, SGU_WIDTH)), _const_spec(sgu_w.shape), _const_spec(sbx.shape),
                  _const_spec((1, SGU_WIDTH))] + [w_spec(w) for w in later_weights],
        out_specs=out_specs,
        out_shape=out_shape,
        scratch_shapes=[pltpu.VMEM(w_in.shape, BF16),
                        pltpu.VMEM((N_SGU_GROUPS // 2, SGU_CHUNK, 2 * SGU_CHUNK), BF16),
                        pltpu.VMEM((tm, SGU_WIDTH), F32)],
        compiler_params=pltpu.CompilerParams(dimension_semantics=("arbitrary", "arbitrary"),
                                             vmem_limit_bytes=56 * 1024 * 1024),
        name="inproj_prompt",
    )(x, g1.reshape(1, d), w_in, ln_g.reshape(1, -1), ln_b.reshape(1, -1), sgu_w, sbx, sog.reshape(1, -1),
      *later_weights)


def _inproj_sample(x, g1, w_in, ln_g, ln_b, sgu_w, sgu_b, sog):
    n, d = x.shape
    a = ATTN_WIDTH
    w00 = jnp.repeat(sgu_w[:, 0, 0], SGU_GROUP_DIM).reshape(1, SGU_WIDTH)
    b0 = jnp.repeat(sgu_b[:, 0], SGU_GROUP_DIM).reshape(1, SGU_WIDTH)
    row = lambda w: pl.BlockSpec((n, w), lambda i: (0, 0))
    vec = _const_spec((1, SGU_WIDTH))
    out_shape = (jax.ShapeDtypeStruct((n, a), F32),) * 3 + (jax.ShapeDtypeStruct((n, SGU_WIDTH), BF16),
                                                           jax.ShapeDtypeStruct((n, SGU_WIDTH), F32))
    return pl.pallas_call(
        _inproj_sample_kernel,
        grid=(1,),
        in_specs=[row(d), _const_spec((1, d)), _const_spec(w_in.shape), vec, vec, vec, vec, vec],
        out_specs=(row(a), row(a), row(a), row(SGU_WIDTH), row(SGU_WIDTH)),
        out_shape=out_shape,
        scratch_shapes=[pltpu.VMEM(w_in.shape, BF16), pltpu.VMEM((d, 2 * a), BF16)],
        compiler_params=pltpu.CompilerParams(dimension_semantics=("arbitrary",),
                                             vmem_limit_bytes=56 * 1024 * 1024),
        name="inproj_sample",
    )(x, g1.reshape(1, d), w_in, ln_g.reshape(1, -1), ln_b.reshape(1, -1), w00, b0, sog.reshape(1, -1))


def _block_choice(gate_t, own):
    nb = gate_t.shape[0]
    blk = lax.broadcasted_iota(jnp.int32, gate_t.shape, 0)
    g = jnp.where(blk < own, gate_t, NEG_INF)
    picked = jnp.zeros(gate_t.shape, F32)
    for _ in range(MOBA_TOPK):
        top = jnp.max(g, axis=0, keepdims=True)
        first = jnp.min(jnp.where(g == top, blk, nb), axis=0, keepdims=True)
        hit = blk == first
        picked = jnp.where(hit, 1.0, picked)
        g = jnp.where(hit, -jnp.inf, g)
    return jnp.where(blk < own, picked, 0.0)


def _split3(x):
    hi = x.astype(BF16)
    r = x - hi.astype(F32)
    mid = r.astype(BF16)
    return hi, mid, (r - mid.astype(F32)).astype(BF16)


def _attn_prompt_kernel(q_ref, kb_ref, vtb_ref, km_ref, sl_ref, o_ref,
                        kmh_ref, kml_ref, kl_ref, qs_ref, sel_ref, acc_ref, m_ref, l_ref, s_ref, cm_ref):
    i = pl.program_id(1)
    blk = MOBA_BLOCK
    nb = kb_ref.shape[1]
    scale2 = LOG2E / math.sqrt(HEAD_DIM)

    @pl.when(i == 0)
    def _prepare():
        km = km_ref[:, 0, :]
        lane_head = lax.broadcasted_iota(jnp.int32, km.shape, 1) // HEAD_DIM
        for h in range(N_HEADS):
            mine = jnp.where(lane_head == h, km, 0.0)
            hi = mine.astype(BF16)
            kmh_ref[h * nb:(h + 1) * nb, :] = hi
            kml_ref[h * nb:(h + 1) * nb, :] = (mine - hi.astype(F32)).astype(BF16)
        lane = lax.broadcasted_iota(jnp.int32, (blk, BIAS_LANES), 1)
        kl = lax.broadcasted_iota(jnp.int32, (blk, BIAS_LANES), 0).astype(F32)
        kl_ref[...] = jnp.where(lane < 3, kl, 0.0).astype(BF16)
        row = lax.broadcasted_iota(jnp.int32, (BIAS_LANES, blk), 0)
        for h in range(N_HEADS):
            hi, mid, lo = (x.astype(F32) for x in _split3(jnp.broadcast_to(sl_ref[h:h + 1, :], (BIAS_LANES, blk))))
            terms = jnp.where(row == 0, hi, jnp.where(row == 1, mid, jnp.where(row == 2, lo, 0.0)))
            qs_ref[h, LANES:, :] = terms.astype(BF16)

    q_t = q_ref[0]
    q_hi = q_t.astype(BF16)
    q_lo = (q_t - q_hi.astype(F32)).astype(BF16)
    gate_t = _dot(kmh_ref[...], q_hi) + (_dot(kml_ref[...], q_hi) + _dot(kmh_ref[...], q_lo))
    row = lax.broadcasted_iota(jnp.int32, (LANES, blk), 0)
    for h in range(N_HEADS):
        pair, hh = divmod(h, HEADS_PER_TILE)
        q_pair = q_t[pair * LANES:(pair + 1) * LANES, :]
        mine = (row >= hh * HEAD_DIM) & (row < (hh + 1) * HEAD_DIM)
        qs_ref[h, :LANES, :] = (jnp.where(mine, q_pair, 0.0) * scale2).astype(BF16)

    def scores(j, h):
        pair = h // HEADS_PER_TILE
        keys = jnp.concatenate([kb_ref[0, j, :, pair * LANES:(pair + 1) * LANES], kl_ref[...]], axis=1)
        return _dot(keys, qs_ref[h])

    ones_rows = jnp.ones((BF16_SUBLANES, blk), BF16)

    def value_product(j, h, p):
        rows = jnp.concatenate([vtb_ref[0, j, h * HEAD_DIM:(h + 1) * HEAD_DIM, :], ones_rows], axis=0)
        pv = _dot(rows, p.astype(BF16))
        return pv[:HEAD_DIM], pv[HEAD_DIM:HEAD_DIM + 1]

    def issue_scores(j, slot, h):
        s = scores(j, h)
        s_ref[slot, h] = s
        cm_ref[slot, h] = jnp.max(s, axis=0, keepdims=True)

    for h in range(N_HEADS):
        s_ref[1, h] = scores(i, h)
    for h in range(N_HEADS):
        issue_scores(0, 0, h)
    for h in range(N_HEADS):
        sel_ref[h] = _block_choice(gate_t[h * nb:(h + 1) * nb, :], i)

    kl = lax.broadcasted_iota(jnp.int32, (blk, blk), 0)
    ql = lax.broadcasted_iota(jnp.int32, (blk, blk), 1)
    for h in range(N_HEADS):
        s = jnp.where(kl <= ql, s_ref[1, h], NEG_INF)
        m = jnp.max(s, axis=0, keepdims=True)
        m_ref[h] = m
        acc_ref[h], l_ref[h] = value_product(i, h, jnp.exp2(s - m))

    def finish_block(j, slot, h):
        m, l = m_ref[h], l_ref[h]
        c = sl_ref[h:h + 1, :] * ((j - i).astype(F32) * float(blk))
        chosen = sel_ref[h, pl.ds(j, 1), :] > 0.5
        m_new = jnp.where(chosen, jnp.maximum(m, cm_ref[slot, h] + c), m)
        alpha = jnp.exp2(m - m_new)
        shift = jnp.where(chosen, m_new - c, -NEG_INF)
        pv, p_sum = value_product(j, h, jnp.exp2(s_ref[slot, h] - shift))
        m_ref[h] = m_new
        l_ref[h] = alpha * l + p_sum
        acc_ref[h] = alpha * acc_ref[h] + pv

    def trip(t, carry):
        for slot in range(2):
            j = 2 * t + slot
            for h in range(N_HEADS):
                issue_scores(jnp.minimum(j + 1, nb - 1), 1 - slot, h)
                finish_block(j, slot, h)
        return carry

    lax.fori_loop(0, (i + 1) // 2, trip, 0)
    for h in range(N_HEADS):
        o_ref[0, h * HEAD_DIM:(h + 1) * HEAD_DIM, :] = acc_ref[h] / l_ref[h]


def _attn_prompt(q_t, kb, vtb, kmean):
    b, a, t = q_t.shape
    blk = MOBA_BLOCK
    nb = t // blk
    assert t % blk == 0 and a == ATTN_WIDTH and kb.shape == (b, nb, blk, a) and vtb.shape == (b, nb, a, blk)
    slopes2 = (_alibi_slopes(N_HEADS) * np.float32(LOG2E)).reshape(N_HEADS, 1)
    slopes2 = jnp.asarray(np.broadcast_to(slopes2, (N_HEADS, blk)).copy())
    once = pl.Buffered(1)
    return pl.pallas_call(
        _attn_prompt_kernel,
        grid=(b, nb),
        in_specs=[pl.BlockSpec((1, a, blk), lambda bi, i: (bi, 0, i)),
                  pl.BlockSpec((1, nb, blk, a), lambda bi, i: (bi, 0, 0, 0), pipeline_mode=once),
                  pl.BlockSpec((1, nb, a, blk), lambda bi, i: (bi, 0, 0, 0), pipeline_mode=once),
                  pl.BlockSpec((nb, 1, a), lambda bi, i: (bi, 0, 0)),
                  pl.BlockSpec((N_HEADS, blk), lambda bi, i: (0, 0))],
        out_specs=pl.BlockSpec((1, a, blk), lambda bi, i: (bi, 0, i)),
        out_shape=jax.ShapeDtypeStruct((b, a, t), F32),
        scratch_shapes=[pltpu.VMEM((N_HEADS * nb, a), BF16),
                        pltpu.VMEM((N_HEADS * nb, a), BF16),
                        pltpu.VMEM((blk, BIAS_LANES), BF16),
                        pltpu.VMEM((N_HEADS, LANES + BIAS_LANES, blk), BF16),
                        pltpu.VMEM((N_HEADS, nb, blk), F32),
                        pltpu.VMEM((N_HEADS, HEAD_DIM, blk), F32),
                        pltpu.VMEM((N_HEADS, 1, blk), F32),
                        pltpu.VMEM((N_HEADS, 1, blk), F32),
                        pltpu.VMEM((2, N_HEADS, blk, blk), F32),
                        pltpu.VMEM((2, N_HEADS, 1, blk), F32)],
        compiler_params=pltpu.CompilerParams(dimension_semantics=("arbitrary", "arbitrary"),
                                             vmem_limit_bytes=48 * 1024 * 1024),
        name="attn_prompt",
    )(q_t, kb, vtb, kmean, slopes2)


def _sample_attention(q_t, kn_t, vn_t, slopes, k_tile, v_tile, n_pages, page):
    past = n_pages * page
    pages_per_block = MOBA_BLOCK // page
    nb = past // MOBA_BLOCK
    scale = 1.0 / math.sqrt(HEAD_DIM)

    head_row = lax.broadcasted_iota(jnp.int32, (N_HEADS, page), 0)
    raw = [jnp.zeros((N_HEADS, page), F32) for _ in range(n_pages + 1)]
    for h in range(N_HEADS):
        q_b = jnp.broadcast_to(q_t[:, h:h + 1], (HEAD_DIM, page))
        tiles = [k_tile(pg, h) for pg in range(n_pages)] + [jnp.broadcast_to(kn_t[:, h:h + 1], (HEAD_DIM, page))]
        for pg, tile in enumerate(tiles):
            raw[pg] = jnp.where(head_row == h, jnp.sum(q_b * tile, axis=0, keepdims=True), raw[pg])
    gate = []
    for n in range(nb):
        tot = raw[n * pages_per_block]
        for r in range(1, pages_per_block):
            tot = tot + raw[n * pages_per_block + r]
        gate.append(jnp.sum(tot, axis=1, keepdims=True) * (1.0 / MOBA_BLOCK))
    chosen = []
    for n in range(nb):
        rank = jnp.where(NEG_INF > gate[n], 1.0, 0.0)
        for m in range(nb):
            if m != n:
                ahead = (gate[m] >= gate[n]) if m < n else (gate[m] > gate[n])
                rank = rank + jnp.where(ahead, 1.0, 0.0)
        chosen.append(jnp.where(rank < MOBA_TOPK, 1.0, 0.0))

    lane = lax.broadcasted_iota(jnp.int32, (N_HEADS, page), 1)
    dist0 = (past - lane).astype(F32)
    s_pages = []
    for pg in range(n_pages):
        s = raw[pg] * scale - slopes * (dist0 - float(pg * page))
        ok = jnp.broadcast_to(chosen[pg // pages_per_block], s.shape) > 0.5
        s_pages.append(jnp.where(ok, s, NEG_INF))
    s_pages.append(jnp.where(lane == 0, raw[n_pages] * scale, NEG_INF))
    m = jnp.max(s_pages[0], axis=1, keepdims=True)
    for s in s_pages[1:]:
        m = jnp.maximum(m, jnp.max(s, axis=1, keepdims=True))
    p_pages = [jnp.exp(s - m) for s in s_pages]
    l = jnp.sum(p_pages[0], axis=1, keepdims=True)
    for p in p_pages[1:]:
        l = l + jnp.sum(p, axis=1, keepdims=True)
    outs = []
    for h in range(N_HEADS):
        acc = jnp.broadcast_to(p_pages[n_pages][h:h + 1, :], (HEAD_DIM, page)) * \
            jnp.broadcast_to(vn_t[:, h:h + 1], (HEAD_DIM, page))
        for pg in range(n_pages):
            acc = acc + jnp.broadcast_to(p_pages[pg][h:h + 1, :], (HEAD_DIM, page)) * v_tile(pg, h)
        outs.append(jnp.sum(acc, axis=1, keepdims=True) / l[h:h + 1, :])
    return jnp.concatenate(outs, axis=1)


SEQS_PER_STEP = 2


def _out_proj(x_ref, a_ref, gn_ref, ag_ref, wo_ref, g2_ref):
    a = ATTN_WIDTH
    attn = a_ref[0].T if len(a_ref.shape) == 3 else a_ref[...]
    an = _rmsnorm(attn, ag_ref[...]).astype(BF16)
    x1 = x_ref[...] + (_dot(an, wo_ref[:a, :]) + _dot(gn_ref[...], wo_ref[a:, :]))
    return x1, _rmsnorm(x1, g2_ref[...]).astype(BF16)


def _ffn_half(hf, wg_ref, wu_ref, wd_ref, half):
    w = wg_ref.shape[1] // 2
    cols = slice(half * w, (half + 1) * w)
    gate = _dot(hf, wg_ref[:, cols])
    up = _dot(hf, wu_ref[:, cols])
    act = (gate * jax.nn.sigmoid(gate) * up).astype(BF16)
    return _dot(act, wd_ref[cols, :])


def _outffn_kernel(x_ref, a_ref, gn_ref, ag_ref, wo_ref, g2_ref, wg_ref, wu_ref, wd_ref, gf_ref, y_ref):
    x1, hf = _out_proj(x_ref, a_ref, gn_ref, ag_ref, wo_ref, g2_ref)
    ff = _ffn_half(hf, wg_ref, wu_ref, wd_ref, 0) + _ffn_half(hf, wg_ref, wu_ref, wd_ref, 1)
    y_ref[...] = _rmsnorm(x1 + ff, gf_ref[...])


def _outffn_attn_sample_kernel(pt_ref, x_ref, a_ref, gn_ref, ag_ref, wo_ref, g2_ref, wg_ref, wu_ref, wd_ref,
                               gf_ref, q_ref, kn_ref, vn_ref, sl_ref, ck_hbm, cv_hbm, y_ref, o_ref,
                               kbuf, vbuf, sem, x1_ref, hf_ref, ff_ref):
    t = pl.program_id(0)
    last_seq = SEQS_PER_STEP * pl.num_programs(0) - 1
    n_pages, page = kbuf.shape[1], kbuf.shape[-1]

    def page_copies(seq, slot):
        copies = []
        for pg in range(n_pages):
            pid = pt_ref[seq, pg]
            copies.append(pltpu.make_async_copy(ck_hbm.at[pid], kbuf.at[slot, pg], sem.at[slot, 0]))
            copies.append(pltpu.make_async_copy(cv_hbm.at[pid], vbuf.at[slot, pg], sem.at[slot, 1]))
        return copies

    def fetch(seq, slot):
        for c in page_copies(seq, slot):
            c.start()

    def wait(seq, slot):
        for c in page_copies(seq, slot):
            c.wait()

    def attend(slot):
        o_ref[slot] = _sample_attention(q_ref[slot], kn_ref[slot], vn_ref[slot], sl_ref[:, :1],
                                        lambda pg, h: kbuf[slot, pg, h], lambda pg, h: vbuf[slot, pg, h],
                                        n_pages, page)

    @pl.when(t == 0)
    def _first():
        fetch(0, 0)

    seq0 = SEQS_PER_STEP * t
    wait(seq0, 0)
    fetch(seq0 + 1, 1)
    x1, hf = _out_proj(x_ref, a_ref, gn_ref, ag_ref, wo_ref, g2_ref)
    x1_ref[...] = x1
    hf_ref[...] = hf
    ff_ref[...] = _ffn_half(hf, wg_ref, wu_ref, wd_ref, 0)
    attend(0)

    wait(seq0 + 1, 1)
    nxt = jnp.minimum(seq0 + 2, last_seq)
    fetch(nxt, 0)
    ff = ff_ref[...] + _ffn_half(hf_ref[...], wg_ref, wu_ref, wd_ref, 1)
    y_ref[...] = _rmsnorm(x1_ref[...] + ff, gf_ref[...])
    attend(1)

    @pl.when(t == pl.num_programs(0) - 1)
    def _drain():
        wait(last_seq, 0)


def _outffn(x, a, gn, ag, w_out, g2, w_gate, w_up, w_down, gf):
    n, d = x.shape
    assert (w_gate.shape[1] // 2) % LANES == 0
    row = lambda w: pl.BlockSpec((n, w), lambda i: (0, 0))
    return pl.pallas_call(
        _outffn_kernel,
        grid=(1,),
        in_specs=[row(d), row(ATTN_WIDTH), row(SGU_WIDTH), _const_spec((1, ATTN_WIDTH)), _const_spec(w_out.shape),
                  _const_spec((1, d)), _const_spec(w_gate.shape), _const_spec(w_up.shape),
                  _const_spec(w_down.shape), _const_spec((1, d))],
        out_specs=row(d),
        out_shape=jax.ShapeDtypeStruct((n, d), F32),
        compiler_params=pltpu.CompilerParams(dimension_semantics=("arbitrary",),
                                             vmem_limit_bytes=56 * 1024 * 1024),
        name="outffn_sample",
    )(x, a, gn, ag.reshape(1, -1), w_out, g2.reshape(1, -1), w_gate, w_up, w_down, gf.reshape(1, -1))


def _outffn_with_sample_attention(x, a_t, gn, ag, w_out, g2, w_gate, w_up, w_down, gf,
                                  q, k_new, v_new, cache_k, cache_v, page_table):
    n, d = x.shape
    n_seq, a = q.shape
    n_pool, page, h, dh = cache_k.shape
    n_pages = page_table.shape[1]
    assert h * dh == a and MOBA_BLOCK % page == 0 and (n_pages * page) % MOBA_BLOCK == 0
    assert n_seq % SEQS_PER_STEP == 0 and (w_gate.shape[1] // 2) % LANES == 0
    steps = n_seq // SEQS_PER_STEP
    assert n % steps == 0
    tm = n // steps
    per_seq = a_t.shape[2] // tm
    assert tm % LANES == 0 and a_t.shape[2] % tm == 0 and a_t.shape[0] * a_t.shape[2] == n
    slopes = jnp.asarray(np.broadcast_to(_alibi_slopes(N_HEADS).reshape(N_HEADS, 1), (N_HEADS, LANES)).copy())
    ck = jnp.transpose(cache_k, (0, 2, 3, 1))
    cv = jnp.transpose(cache_v, (0, 2, 3, 1))
    tok_t = lambda z: jnp.transpose(z.reshape(n_seq, h, dh), (0, 2, 1))
    row = lambda w: pl.BlockSpec((tm, w), lambda i, pt: (i, 0))
    tok = pl.BlockSpec((SEQS_PER_STEP, dh, h), lambda i, pt: (i, 0, 0))
    hbm = pl.BlockSpec(memory_space=pl.ANY)
    grid_spec = pltpu.PrefetchScalarGridSpec(
        num_scalar_prefetch=1,
        grid=(steps,),
        in_specs=[row(d), pl.BlockSpec((1, ATTN_WIDTH, tm), lambda i, pt: (i // per_seq, 0, i % per_seq)),
                  row(SGU_WIDTH), _const_spec((1, ATTN_WIDTH)), _const_spec(w_out.shape), _const_spec((1, d)),
                  _const_spec(w_gate.shape), _const_spec(w_up.shape), _const_spec(w_down.shape),
                  _const_spec((1, d)),
                  tok, tok, tok, pl.BlockSpec((N_HEADS, LANES), lambda i, pt: (0, 0)), hbm, hbm],
        out_specs=(row(d), tok),
        scratch_shapes=[pltpu.VMEM((SEQS_PER_STEP, n_pages, h, dh, page), F32),
                        pltpu.VMEM((SEQS_PER_STEP, n_pages, h, dh, page), F32),
                        pltpu.SemaphoreType.DMA((SEQS_PER_STEP, 2)),
                        pltpu.VMEM((tm, d), F32),
                        pltpu.VMEM((tm, d), BF16),
                        pltpu.VMEM((tm, d), F32)],
    )
    y, out = pl.pallas_call(
        _outffn_attn_sample_kernel,
        grid_spec=grid_spec,
        out_shape=(jax.ShapeDtypeStruct((n, d), F32), jax.ShapeDtypeStruct((n_seq, dh, h), F32)),
        compiler_params=pltpu.CompilerParams(dimension_semantics=("arbitrary",),
                                             vmem_limit_bytes=56 * 1024 * 1024),
        name="outffn_prompt_attn_sample",
    )(page_table, x, a_t, gn, ag.reshape(1, -1), w_out, g2.reshape(1, -1), w_gate, w_up, w_down,
      gf.reshape(1, -1), tok_t(q), tok_t(k_new), tok_t(v_new), slopes, ck, cv)
    return y, jnp.transpose(out, (0, 2, 1)).reshape(n_seq, a)


def kernel(x_prompt, x_sample, cache_k, cache_v, page_table, norm1_g, w_in, attn_out_g, sgu_ln_g, sgu_ln_b,
           sgu_w, sgu_b, sgu_out_g, w_out, norm2_g, w_gate, w_up, w_down, final_g):
    depth = w_in.shape[0]
    assert depth == 1, "single-layer stack"
    l = 0
    bsz, seq, d = x_prompt.shape
    dec_b, dec_seq, _ = x_sample.shape
    assert dec_seq == 1
    xs = x_sample.reshape(dec_b, d)

    qp_t, kb, kt, vt, vtb, kmean, gnp, wo, wg, wu, wd = _inproj_prompt(
        x_prompt, norm1_g[l], w_in[l], sgu_ln_g[l], sgu_ln_b[l], sgu_w[l], sgu_b[l], sgu_out_g[l],
        (w_out[l], w_gate[l], w_up[l], w_down[l]), tm=512)
    ap_t = _attn_prompt(qp_t, kb, vtb, kmean)

    qs, ks, vs, gns, vns = _inproj_sample(xs, norm1_g[l], w_in[l], sgu_ln_g[l], sgu_ln_b[l], sgu_w[l], sgu_b[l],
                                          sgu_out_g[l])
    yp, a_s = _outffn_with_sample_attention(
        x_prompt.reshape(bsz * seq, d), ap_t, gnp.reshape(bsz * seq, -1), attn_out_g[l], wo, norm2_g[l],
        wg, wu, wd, final_g, qs, ks, vs, cache_k[l], cache_v[l], page_table)
    ys = _outffn(xs, a_s, gns, attn_out_g[l], wo, norm2_g[l], wg, wu, wd, final_g)

    heads_last = lambda x_t: jnp.transpose(x_t.reshape(bsz, N_HEADS, HEAD_DIM, seq), (0, 3, 1, 2))[None]
    y_prompt = yp.reshape(bsz, seq, d)
    y_sample = ys.reshape(dec_b, dec_seq, d)
    k_prompt = heads_last(kt)
    v_prompt = heads_last(vt)
    k_sample = ks.reshape(depth, dec_b, dec_seq, N_HEADS, HEAD_DIM)
    v_sample = vs.reshape(depth, dec_b, dec_seq, N_HEADS, HEAD_DIM)
    sgu_v_sample = vns.reshape(depth, dec_b, dec_seq, SGU_WIDTH)
    return (y_prompt, y_sample, k_prompt, v_prompt, k_sample, v_sample, sgu_v_sample)
```

```python
import functools
import math

import numpy as np
import jax
import jax.numpy as jnp
from jax import lax
from jax.experimental import pallas as pl
from jax.experimental.pallas import tpu as pltpu

HEAD_DIM = 64
N_HEADS = 8
ATTN_WIDTH = N_HEADS * HEAD_DIM
N_SGU_GROUPS = 8
SGU_GROUP_DIM = 64
SGU_WIDTH = N_SGU_GROUPS * SGU_GROUP_DIM
SGU_CHUNK = 128
MOBA_BLOCK = 256
MOBA_TOPK = 3
RMS_EPS = 1e-6
LN_EPS = 1e-5
NEG_INF = -1e30
LOG2E = 1.4426950408889634

LANES = 128
BF16_SUBLANES = 16
BIAS_LANES = BF16_SUBLANES
Q_BLOCKS = 2
HEADS_PER_TILE = LANES // HEAD_DIM
N_HEAD_PAIRS = N_HEADS // HEADS_PER_TILE

F32 = jnp.float32
BF16 = jnp.bfloat16

_NT = (((1,), (1,)), ((), ()))
_NN = (((1,), (0,)), ((), ()))


def _dot(a, b, dims=_NN, precision=None):
    return lax.dot_general(a, b, dims, precision=precision, preferred_element_type=F32)


def _rmsnorm(x, g):
    r = lax.rsqrt(jnp.mean(x * x, axis=-1, keepdims=True) + RMS_EPS)
    return (x * r) * g


def _gelu_tanh(x):
    c = math.sqrt(2.0 / math.pi)
    return 0.5 * x * (1.0 + jnp.tanh(c * (x + 0.044715 * (x * x * x))))


def _alibi_slopes(n):
    start = 2.0 ** (-8.0 / n)
    return np.asarray([start ** (i + 1) for i in range(n)], dtype=np.float32)


def _split_weights(w_ref, whi_ref, wlo_ref, n_precise):
    rows = w_ref.shape[0]
    step = 128
    for r in range(0, rows, step):
        w = w_ref[r:r + step, :]
        hi = w.astype(BF16)
        whi_ref[r:r + step, :] = hi
        wlo_ref[r:r + step, :] = (w[:, :n_precise] - hi[:, :n_precise].astype(F32)).astype(BF16)


def _project(x, g1_ref, whi_ref, wlo_ref):
    n_precise = wlo_ref.shape[1]
    h = _rmsnorm(x, g1_ref[...])
    h_hi = h.astype(BF16)
    h_lo = (h - h_hi.astype(F32)).astype(BF16)
    w_qk = whi_ref[:, :n_precise]
    z_qk = _dot(h_hi, w_qk) + (_dot(h_lo, w_qk) + _dot(h_hi, wlo_ref[...]))
    z_rest = _dot(h_hi, whi_ref[:, n_precise:])
    return z_qk, z_rest


def _layernorm(x, g, b):
    mu = jnp.mean(x, axis=-1, keepdims=True)
    xc = x - mu
    var = jnp.mean(xc * xc, axis=-1, keepdims=True)
    return (xc * lax.rsqrt(var + LN_EPS)) * g + b


def _inproj_prompt_kernel(x_ref, g1_ref, w_ref, lng_ref, lnb_ref, sw_ref, sbx_ref, sog_ref, *refs):
    later_w = refs[:N_LATER_WEIGHTS]
    q_ref, kb_ref, kt_ref, vt_ref, vtb_ref, kmean_ref, gn_ref = refs[N_LATER_WEIGHTS:N_LATER_WEIGHTS + 7]
    later_wb = refs[N_LATER_WEIGHTS + 7:2 * N_LATER_WEIGHTS + 7]
    wb_ref, wcat_ref, s_ref = refs[2 * N_LATER_WEIGHTS + 7:]
    tm = x_ref.shape[1]
    a = ATTN_WIDTH
    blk = MOBA_BLOCK

    for src, dst in zip(later_w, later_wb):
        dst[...] = src[...].astype(BF16)

    @pl.when((pl.program_id(0) == 0) & (pl.program_id(1) == 0))
    def _prepare():
        for r in range(0, w_ref.shape[0], LANES):
            wb_ref[r:r + LANES, :] = w_ref[r:r + LANES, :].astype(BF16)
        t = lax.broadcasted_iota(jnp.int32, (SGU_CHUNK, SGU_CHUNK), 0)
        s = lax.broadcasted_iota(jnp.int32, (SGU_CHUNK, SGU_CHUNK), 1)
        causal = t >= s
        for gp in range(N_SGU_GROUPS // 2):
            w0 = jnp.where(causal, sw_ref[2 * gp], 0.0)
            w1 = jnp.where(causal, sw_ref[2 * gp + 1], 0.0)
            wcat_ref[gp] = jnp.concatenate([w0, w1], axis=1).astype(BF16)

    z = _dot(_rmsnorm(x_ref[0], g1_ref[...]).astype(BF16), wb_ref[...])
    q_ref[0] = z[:, :a].T
    zk = z[:, a:2 * a]
    zv = z[:, 2 * a:3 * a]
    z_rest = z[:, 2 * a:]
    kt_ref[0] = zk.T
    vt = zv.T
    vt_ref[0] = vt
    for r in range(tm // blk):
        rows = slice(r * blk, (r + 1) * blk)
        kb_ref[0, r] = zk[rows, :].astype(BF16)
        vtb_ref[0, r] = vt[:, rows].astype(BF16)
        kmean_ref[r] = jnp.mean(zk[rows, :], axis=0, keepdims=True)

    u = _gelu_tanh(z_rest[:, a:a + SGU_WIDTH])
    gv = _gelu_tanh(z_rest[:, a + SGU_WIDTH:])
    vn = _layernorm(gv, lng_ref[...], lnb_ref[...])

    lane = lax.broadcasted_iota(jnp.int32, (SGU_CHUNK, LANES), 1)
    low = lane < SGU_GROUP_DIM
    n_chunks = tm // SGU_CHUNK
    for c in range(0, n_chunks, 2):
        both = range(c, min(c + 2, n_chunks))
        for gp in range(N_SGU_GROUPS // 2):
            cols = slice(gp * LANES, (gp + 1) * LANES)
            rhs = []
            for cc in both:
                vp = vn[cc * SGU_CHUNK:(cc + 1) * SGU_CHUNK, cols]
                rhs.append(jnp.concatenate([jnp.where(low, vp, 0.0), jnp.where(low, 0.0, vp)], axis=0))
            s = _dot(wcat_ref[gp], jnp.concatenate(rhs, axis=1).astype(BF16))
            for k, cc in enumerate(both):
                s_ref[cc * SGU_CHUNK:(cc + 1) * SGU_CHUNK, cols] = s[:, k * LANES:(k + 1) * LANES] + sbx_ref[:, cols]
    g = u * s_ref[...]
    gn_ref[0] = _rmsnorm(g, sog_ref[...]).astype(gn_ref.dtype)


def _inproj_sample_kernel(x_ref, g1_ref, w_ref, lng_ref, lnb_ref, w00_ref, b0_ref, sog_ref,
                          q_ref, k_ref, v_ref, gn_ref, vn_ref,
                          whi_ref, wlo_ref):
    a = ATTN_WIDTH

    @pl.when(pl.program_id(0) == 0)
    def _prepare():
        _split_weights(w_ref, whi_ref, wlo_ref, 2 * a)

    z_qk, z_rest = _project(x_ref[...], g1_ref, whi_ref, wlo_ref)
    q_ref[...] = z_qk[:, :a]
    k_ref[...] = z_qk[:, a:]
    v_ref[...] = z_rest[:, :a]
    u = _gelu_tanh(z_rest[:, a:a + SGU_WIDTH])
    gv = _gelu_tanh(z_rest[:, a + SGU_WIDTH:])
    vn = _layernorm(gv, lng_ref[...], lnb_ref[...])
    vn_ref[...] = vn
    g = u * (vn * w00_ref[...] + b0_ref[...])
    gn_ref[...] = _rmsnorm(g, sog_ref[...]).astype(gn_ref.dtype)


def _const_spec(shape):
    zeros = (0,) * len(shape)
    return pl.BlockSpec(shape, lambda *_: zeros, pipeline_mode=pl.Buffered(1))


N_LATER_WEIGHTS = 4


def _inproj_prompt(x, g1, w_in, ln_g, ln_b, sgu_w, sgu_b, sog, later_weights, *, tm):
    b, t, d = x.shape
    a = ATTN_WIDTH
    blk = MOBA_BLOCK
    assert t % tm == 0 and tm % SGU_CHUNK == 0 and tm % blk == 0 and len(later_weights) == N_LATER_WEIGHTS
    nb = t // blk
    per = tm // blk
    steps = t // tm
    assert all(w.shape[0] % (steps * BF16_SUBLANES) == 0 and w.shape[1] % (b * LANES) == 0 for w in later_weights)
    w_spec = lambda w: pl.BlockSpec((w.shape[0] // steps, w.shape[1] // b), lambda bi, i: (i, bi))
    sbx = jnp.repeat(sgu_b.T, SGU_GROUP_DIM, axis=1)
    row = lambda w: pl.BlockSpec((1, tm, w), lambda bi, i: (bi, i, 0))
    col = lambda w: pl.BlockSpec((1, w, tm), lambda bi, i: (bi, 0, i))
    out_specs = (col(a),
                 pl.BlockSpec((1, per, blk, a), lambda bi, i: (bi, i, 0, 0)),
                 col(a), col(a),
                 pl.BlockSpec((1, per, a, blk), lambda bi, i: (bi, i, 0, 0)),
                 pl.BlockSpec((per, 1, a), lambda bi, i: (bi * (nb // per) + i, 0, 0)),
                 row(SGU_WIDTH)) + tuple(w_spec(w) for w in later_weights)
    out_shape = (jax.ShapeDtypeStruct((b, a, t), F32),
                 jax.ShapeDtypeStruct((b, nb, blk, a), BF16),
                 jax.ShapeDtypeStruct((b, a, t), F32),
                 jax.ShapeDtypeStruct((b, a, t), F32),
                 jax.ShapeDtypeStruct((b, nb, a, blk), BF16),
                 jax.ShapeDtypeStruct((b * nb, 1, a), F32),
                 jax.ShapeDtypeStruct((b, t, SGU_WIDTH), BF16)) + tuple(
                     jax.ShapeDtypeStruct(w.shape, BF16) for w in later_weights)
    return pl.pallas_call(
        _inproj_prompt_kernel,
        grid=(b, steps),
        in_specs=[row(d), _const_spec((1, d)), _const_spec(w_in.shape), _const_spec((1, SGU_WIDTH)),
                  _const_spec((1, SGU_WIDTH)), _const_spec(sgu_w.shape), _const_spec(sbx.shape),
                  _const_spec((1, SGU_WIDTH))] + [w_spec(w) for w in later_weights],
        out_specs=out_specs,
        out_shape=out_shape,
        scratch_shapes=[pltpu.VMEM(w_in.shape, BF16),
                        pltpu.VMEM((N_SGU_GROUPS // 2, SGU_CHUNK, 2 * SGU_CHUNK), BF16),
                        pltpu.VMEM((tm, SGU_WIDTH), F32)],
        compiler_params=pltpu.CompilerParams(dimension_semantics=("arbitrary", "arbitrary"),
                                             vmem_limit_bytes=56 * 1024 * 1024),
        name="inproj_prompt",
    )(x, g1.reshape(1, d), w_in, ln_g.reshape(1, -1), ln_b.reshape(1, -1), sgu_w, sbx, sog.reshape(1, -1),
      *later_weights)


def _inproj_sample(x, g1, w_in, ln_g, ln_b, sgu_w, sgu_b, sog):
    n, d = x.shape
    a = ATTN_WIDTH
    w00 = jnp.repeat(sgu_w[:, 0, 0], SGU_GROUP_DIM).reshape(1, SGU_WIDTH)
    b0 = jnp.repeat(sgu_b[:, 0], SGU_GROUP_DIM).reshape(1, SGU_WIDTH)
    row = lambda w: pl.BlockSpec((n, w), lambda i: (0, 0))
    vec = _const_spec((1, SGU_WIDTH))
    out_shape = (jax.ShapeDtypeStruct((n, a), F32),) * 3 + (jax.ShapeDtypeStruct((n, SGU_WIDTH), BF16),
                                                           jax.ShapeDtypeStruct((n, SGU_WIDTH), F32))
    return pl.pallas_call(
        _inproj_sample_kernel,
        grid=(1,),
        in_specs=[row(d), _const_spec((1, d)), _const_spec(w_in.shape), vec, vec, vec, vec, vec],
        out_specs=(row(a), row(a), row(a), row(SGU_WIDTH), row(SGU_WIDTH)),
        out_shape=out_shape,
        scratch_shapes=[pltpu.VMEM(w_in.shape, BF16), pltpu.VMEM((d, 2 * a), BF16)],
        compiler_params=pltpu.CompilerParams(dimension_semantics=("arbitrary",),
                                             vmem_limit_bytes=56 * 1024 * 1024),
        name="inproj_sample",
    )(x, g1.reshape(1, d), w_in, ln_g.reshape(1, -1), ln_b.reshape(1, -1), w00, b0, sog.reshape(1, -1))


def _block_choice(gate_t, own):
    nb = gate_t.shape[0]
    blk = lax.broadcasted_iota(jnp.int32, gate_t.shape, 0)
    g = jnp.where(blk < own, gate_t, NEG_INF)
    picked = jnp.zeros(gate_t.shape, F32)
    for _ in range(MOBA_TOPK):
        top = jnp.max(g, axis=0, keepdims=True)
        first = jnp.min(jnp.where(g == top, blk, nb), axis=0, keepdims=True)
        hit = blk == first
        picked = jnp.where(hit, 1.0, picked)
        g = jnp.where(hit, -jnp.inf, g)
    return jnp.where(blk < own, picked, 0.0)


def _split3(x):
    hi = x.astype(BF16)
    r = x - hi.astype(F32)
    mid = r.astype(BF16)
    return hi, mid, (r - mid.astype(F32)).astype(BF16)


def _attn_prompt_kernel(q_ref, kb_ref, vtb_ref, km_ref, sl_ref, o_ref,
                        kmh_ref, kml_ref, kl_ref, qs_ref, sel_ref, acc_ref, m_ref, l_ref, s_ref, cm_ref, d_ref):
    i = pl.program_id(1)
    blk = MOBA_BLOCK
    nb = kb_ref.shape[1]
    tq = q_ref.shape[2]
    scale2 = LOG2E / math.sqrt(HEAD_DIM)

    @pl.when(i == 0)
    def _prepare():
        km = km_ref[:, 0, :]
        lane_head = lax.broadcasted_iota(jnp.int32, km.shape, 1) // HEAD_DIM
        for h in range(N_HEADS):
            mine = jnp.where(lane_head == h, km, 0.0)
            hi = mine.astype(BF16)
            kmh_ref[h * nb:(h + 1) * nb, :] = hi
            kml_ref[h * nb:(h + 1) * nb, :] = (mine - hi.astype(F32)).astype(BF16)
        lane = lax.broadcasted_iota(jnp.int32, (blk, BIAS_LANES), 1)
        kl = lax.broadcasted_iota(jnp.int32, (blk, BIAS_LANES), 0).astype(F32)
        kl_ref[...] = jnp.where(lane < 3, kl, 0.0).astype(BF16)
        row = lax.broadcasted_iota(jnp.int32, (BIAS_LANES, tq), 0)
        for h in range(N_HEADS):
            hi, mid, lo = (x.astype(F32) for x in _split3(jnp.broadcast_to(sl_ref[h:h + 1, :], (BIAS_LANES, tq))))
            terms = jnp.where(row == 0, hi, jnp.where(row == 1, mid, jnp.where(row == 2, lo, 0.0)))
            qs_ref[h, LANES:, :] = terms.astype(BF16)

    q_t = q_ref[0]
    first_own = Q_BLOCKS * i
    own = first_own + lax.broadcasted_iota(jnp.int32, (1, tq), 1) // blk
    q_hi = q_t.astype(BF16)
    q_lo = (q_t - q_hi.astype(F32)).astype(BF16)
    gate_t = _dot(kmh_ref[...], q_hi) + (_dot(kml_ref[...], q_hi) + _dot(kmh_ref[...], q_lo))
    row = lax.broadcasted_iota(jnp.int32, (LANES, tq), 0)
    for h in range(N_HEADS):
        pair, hh = divmod(h, HEADS_PER_TILE)
        q_pair = q_t[pair * LANES:(pair + 1) * LANES, :]
        mine = (row >= hh * HEAD_DIM) & (row < (hh + 1) * HEAD_DIM)
        qs_ref[h, :LANES, :] = (jnp.where(mine, q_pair, 0.0) * scale2).astype(BF16)

    def scores(j, h, first_query=0):
        pair = h // HEADS_PER_TILE
        keys = jnp.concatenate([kb_ref[0, j, :, pair * LANES:(pair + 1) * LANES], kl_ref[...]], axis=1)
        return _dot(keys, qs_ref[h, :, first_query:])

    ones_rows = jnp.ones((BF16_SUBLANES, blk), BF16)

    def value_product(j, h, p):
        rows = jnp.concatenate([vtb_ref[0, j, h * HEAD_DIM:(h + 1) * HEAD_DIM, :], ones_rows], axis=0)
        pv = _dot(rows, p.astype(BF16))
        return pv[:HEAD_DIM], pv[HEAD_DIM:HEAD_DIM + 1]

    def issue_scores(j, slot, h):
        s = scores(j, h)
        s_ref[slot, h] = s
        cm_ref[slot, h] = jnp.max(s, axis=0, keepdims=True)

    def block_bias(j, h):
        return sl_ref[h:h + 1, :] * ((j - own).astype(F32) * float(blk))

    for d in range(Q_BLOCKS):
        for h in range(N_HEADS):
            d_ref[d, h, :, d * blk:] = scores(first_own + d, h, first_query=d * blk)
    for h in range(N_HEADS):
        issue_scores(0, 0, h)
    for h in range(N_HEADS):
        sel_ref[h] = _block_choice(gate_t[h * nb:(h + 1) * nb, :], own)

    causal = lax.broadcasted_iota(jnp.int32, (blk, blk), 0) <= lax.broadcasted_iota(jnp.int32, (blk, blk), 1)
    for h in range(N_HEADS):
        for a in range(Q_BLOCKS):
            lanes = slice(a * blk, (a + 1) * blk)
            s = jnp.where(causal, d_ref[a, h, :, lanes], NEG_INF)
            m = jnp.max(s, axis=0, keepdims=True)
            acc, l = value_product(first_own + a, h, jnp.exp2(s - m))
            for d in range(a):
                j = first_own + d
                c = block_bias(j, h)[:, lanes]
                chosen = sel_ref[h, pl.ds(j, 1), lanes] > 0.5
                s = d_ref[d, h, :, lanes]
                m_new = jnp.where(chosen, jnp.maximum(m, jnp.max(s, axis=0, keepdims=True) + c), m)
                alpha = jnp.exp2(m - m_new)
                pv, p_sum = value_product(j, h, jnp.exp2(s - jnp.where(chosen, m_new - c, -NEG_INF)))
                m, l, acc = m_new, alpha * l + p_sum, alpha * acc + pv
            m_ref[h, :, lanes] = m
            l_ref[h, :, lanes] = l
            acc_ref[h, :, lanes] = acc

    def finish_block(j, slot, h):
        m, l = m_ref[h], l_ref[h]
        c = block_bias(j, h)
        chosen = sel_ref[h, pl.ds(j, 1), :] > 0.5
        m_new = jnp.where(chosen, jnp.maximum(m, cm_ref[slot, h] + c), m)
        alpha = jnp.exp2(m - m_new)
        shift = jnp.where(chosen, m_new - c, -NEG_INF)
        pv, p_sum = value_product(j, h, jnp.exp2(s_ref[slot, h] - shift))
        m_ref[h] = m_new
        l_ref[h] = alpha * l + p_sum
        acc_ref[h] = alpha * acc_ref[h] + pv

    def trip(t, carry):
        for slot in range(2):
            j = 2 * t + slot
            for h in range(N_HEADS):
                issue_scores(jnp.minimum(j + 1, nb - 1), 1 - slot, h)
                finish_block(j, slot, h)
        return carry

    lax.fori_loop(0, first_own // 2, trip, 0)
    for h in range(N_HEADS):
        o_ref[0, h * HEAD_DIM:(h + 1) * HEAD_DIM, :] = acc_ref[h] / l_ref[h]


def _attn_prompt(q_t, kb, vtb, kmean):
    b, a, t = q_t.shape
    blk = MOBA_BLOCK
    nb = t // blk
    tq = Q_BLOCKS * blk
    assert t % tq == 0 and a == ATTN_WIDTH and kb.shape == (b, nb, blk, a) and vtb.shape == (b, nb, a, blk)
    slopes2 = (_alibi_slopes(N_HEADS) * np.float32(LOG2E)).reshape(N_HEADS, 1)
    slopes2 = jnp.asarray(np.broadcast_to(slopes2, (N_HEADS, tq)).copy())
    once = pl.Buffered(1)
    return pl.pallas_call(
        _attn_prompt_kernel,
        grid=(b, t // tq),
        in_specs=[pl.BlockSpec((1, a, tq), lambda bi, i: (bi, 0, i)),
                  pl.BlockSpec((1, nb, blk, a), lambda bi, i: (bi, 0, 0, 0), pipeline_mode=once),
                  pl.BlockSpec((1, nb, a, blk), lambda bi, i: (bi, 0, 0, 0), pipeline_mode=once),
                  pl.BlockSpec((nb, 1, a), lambda bi, i: (bi, 0, 0)),
                  pl.BlockSpec((N_HEADS, tq), lambda bi, i: (0, 0))],
        out_specs=pl.BlockSpec((1, a, tq), lambda bi, i: (bi, 0, i)),
        out_shape=jax.ShapeDtypeStruct((b, a, t), F32),
        scratch_shapes=[pltpu.VMEM((N_HEADS * nb, a), BF16),
                        pltpu.VMEM((N_HEADS * nb, a), BF16),
                        pltpu.VMEM((blk, BIAS_LANES), BF16),
                        pltpu.VMEM((N_HEADS, LANES + BIAS_LANES, tq), BF16),
                        pltpu.VMEM((N_HEADS, nb, tq), F32),
                        pltpu.VMEM((N_HEADS, HEAD_DIM, tq), F32),
                        pltpu.VMEM((N_HEADS, 1, tq), F32),
                        pltpu.VMEM((N_HEADS, 1, tq), F32),
                        pltpu.VMEM((2, N_HEADS, blk, tq), F32),
                        pltpu.VMEM((2, N_HEADS, 1, tq), F32),
                        pltpu.VMEM((Q_BLOCKS, N_HEADS, blk, tq), F32)],
        compiler_params=pltpu.CompilerParams(dimension_semantics=("arbitrary", "arbitrary"),
                                             vmem_limit_bytes=48 * 1024 * 1024),
        name="attn_prompt",
    )(q_t, kb, vtb, kmean, slopes2)


def _sample_attention(q_t, kn_t, vn_t, slopes, k_tile, v_tile, n_pages, page):
    past = n_pages * page
    pages_per_block = MOBA_BLOCK // page
    nb = past // MOBA_BLOCK
    scale = 1.0 / math.sqrt(HEAD_DIM)

    head_row = lax.broadcasted_iota(jnp.int32, (N_HEADS, page), 0)
    raw = [jnp.zeros((N_HEADS, page), F32) for _ in range(n_pages + 1)]
    for h in range(N_HEADS):
        q_b = jnp.broadcast_to(q_t[:, h:h + 1], (HEAD_DIM, page))
        tiles = [k_tile(pg, h) for pg in range(n_pages)] + [jnp.broadcast_to(kn_t[:, h:h + 1], (HEAD_DIM, page))]
        for pg, tile in enumerate(tiles):
            raw[pg] = jnp.where(head_row == h, jnp.sum(q_b * tile, axis=0, keepdims=True), raw[pg])
    gate = []
    for n in range(nb):
        tot = raw[n * pages_per_block]
        for r in range(1, pages_per_block):
            tot = tot + raw[n * pages_per_block + r]
        gate.append(jnp.sum(tot, axis=1, keepdims=True) * (1.0 / MOBA_BLOCK))
    chosen = []
    for n in range(nb):
        rank = jnp.where(NEG_INF > gate[n], 1.0, 0.0)
        for m in range(nb):
            if m != n:
                ahead = (gate[m] >= gate[n]) if m < n else (gate[m] > gate[n])
                rank = rank + jnp.where(ahead, 1.0, 0.0)
        chosen.append(jnp.where(rank < MOBA_TOPK, 1.0, 0.0))

    lane = lax.broadcasted_iota(jnp.int32, (N_HEADS, page), 1)
    dist0 = (past - lane).astype(F32)
    s_pages = []
    for pg in range(n_pages):
        s = raw[pg] * scale - slopes * (dist0 - float(pg * page))
        ok = jnp.broadcast_to(chosen[pg // pages_per_block], s.shape) > 0.5
        s_pages.append(jnp.where(ok, s, NEG_INF))
    s_pages.append(jnp.where(lane == 0, raw[n_pages] * scale, NEG_INF))
    m = jnp.max(s_pages[0], axis=1, keepdims=True)
    for s in s_pages[1:]:
        m = jnp.maximum(m, jnp.max(s, axis=1, keepdims=True))
    p_pages = [jnp.exp(s - m) for s in s_pages]
    l = jnp.sum(p_pages[0], axis=1, keepdims=True)
    for p in p_pages[1:]:
        l = l + jnp.sum(p, axis=1, keepdims=True)
    outs = []
    for h in range(N_HEADS):
        acc = jnp.broadcast_to(p_pages[n_pages][h:h + 1, :], (HEAD_DIM, page)) * \
            jnp.broadcast_to(vn_t[:, h:h + 1], (HEAD_DIM, page))
        for pg in range(n_pages):
            acc = acc + jnp.broadcast_to(p_pages[pg][h:h + 1, :], (HEAD_DIM, page)) * v_tile(pg, h)
        outs.append(jnp.sum(acc, axis=1, keepdims=True) / l[h:h + 1, :])
    return jnp.concatenate(outs, axis=1)


SEQS_PER_STEP = 2


def _out_proj(x_ref, a_ref, gn_ref, ag_ref, wo_ref, g2_ref):
    a = ATTN_WIDTH
    attn = a_ref[0].T if len(a_ref.shape) == 3 else a_ref[...]
    an = _rmsnorm(attn, ag_ref[...]).astype(BF16)
    x1 = x_ref[...] + (_dot(an, wo_ref[:a, :]) + _dot(gn_ref[...], wo_ref[a:, :]))
    return x1, _rmsnorm(x1, g2_ref[...]).astype(BF16)


def _ffn_half(hf, wg_ref, wu_ref, wd_ref, half):
    w = wg_ref.shape[1] // 2
    cols = slice(half * w, (half + 1) * w)
    gate = _dot(hf, wg_ref[:, cols])
    up = _dot(hf, wu_ref[:, cols])
    act = (gate * jax.nn.sigmoid(gate) * up).astype(BF16)
    return _dot(act, wd_ref[cols, :])


def _outffn_kernel(x_ref, a_ref, gn_ref, ag_ref, wo_ref, g2_ref, wg_ref, wu_ref, wd_ref, gf_ref, y_ref):
    x1, hf = _out_proj(x_ref, a_ref, gn_ref, ag_ref, wo_ref, g2_ref)
    ff = _ffn_half(hf, wg_ref, wu_ref, wd_ref, 0) + _ffn_half(hf, wg_ref, wu_ref, wd_ref, 1)
    y_ref[...] = _rmsnorm(x1 + ff, gf_ref[...])


def _outffn_attn_sample_kernel(pt_ref, x_ref, a_ref, gn_ref, ag_ref, wo_ref, g2_ref, wg_ref, wu_ref, wd_ref,
                               gf_ref, q_ref, kn_ref, vn_ref, sl_ref, ck_hbm, cv_hbm, y_ref, o_ref,
                               kbuf, vbuf, sem, x1_ref, hf_ref, ff_ref):
    t = pl.program_id(0)
    last_seq = SEQS_PER_STEP * pl.num_programs(0) - 1
    n_pages, page = kbuf.shape[1], kbuf.shape[-1]

    def page_copies(seq, slot):
        copies = []
        for pg in range(n_pages):
            pid = pt_ref[seq, pg]
            copies.append(pltpu.make_async_copy(ck_hbm.at[pid], kbuf.at[slot, pg], sem.at[slot, 0]))
            copies.append(pltpu.make_async_copy(cv_hbm.at[pid], vbuf.at[slot, pg], sem.at[slot, 1]))
        return copies

    def fetch(seq, slot):
        for c in page_copies(seq, slot):
            c.start()

    def wait(seq, slot):
        for c in page_copies(seq, slot):
            c.wait()

    def attend(slot):
        o_ref[slot] = _sample_attention(q_ref[slot], kn_ref[slot], vn_ref[slot], sl_ref[:, :1],
                                        lambda pg, h: kbuf[slot, pg, h], lambda pg, h: vbuf[slot, pg, h],
                                        n_pages, page)

    @pl.when(t == 0)
    def _first():
        fetch(0, 0)

    seq0 = SEQS_PER_STEP * t
    wait(seq0, 0)
    fetch(seq0 + 1, 1)
    x1, hf = _out_proj(x_ref, a_ref, gn_ref, ag_ref, wo_ref, g2_ref)
    x1_ref[...] = x1
    hf_ref[...] = hf
    ff_ref[...] = _ffn_half(hf, wg_ref, wu_ref, wd_ref, 0)
    attend(0)

    wait(seq0 + 1, 1)
    nxt = jnp.minimum(seq0 + 2, last_seq)
    fetch(nxt, 0)
    ff = ff_ref[...] + _ffn_half(hf_ref[...], wg_ref, wu_ref, wd_ref, 1)
    y_ref[...] = _rmsnorm(x1_ref[...] + ff, gf_ref[...])
    attend(1)

    @pl.when(t == pl.num_programs(0) - 1)
    def _drain():
        wait(last_seq, 0)


def _outffn(x, a, gn, ag, w_out, g2, w_gate, w_up, w_down, gf):
    n, d = x.shape
    assert (w_gate.shape[1] // 2) % LANES == 0
    row = lambda w: pl.BlockSpec((n, w), lambda i: (0, 0))
    return pl.pallas_call(
        _outffn_kernel,
        grid=(1,),
        in_specs=[row(d), row(ATTN_WIDTH), row(SGU_WIDTH), _const_spec((1, ATTN_WIDTH)), _const_spec(w_out.shape),
                  _const_spec((1, d)), _const_spec(w_gate.shape), _const_spec(w_up.shape),
                  _const_spec(w_down.shape), _const_spec((1, d))],
        out_specs=row(d),
        out_shape=jax.ShapeDtypeStruct((n, d), F32),
        compiler_params=pltpu.CompilerParams(dimension_semantics=("arbitrary",),
                                             vmem_limit_bytes=56 * 1024 * 1024),
        name="outffn_sample",
    )(x, a, gn, ag.reshape(1, -1), w_out, g2.reshape(1, -1), w_gate, w_up, w_down, gf.reshape(1, -1))


def _outffn_with_sample_attention(x, a_t, gn, ag, w_out, g2, w_gate, w_up, w_down, gf,
                                  q, k_new, v_new, cache_k, cache_v, page_table):
    n, d = x.shape
    n_seq, a = q.shape
    n_pool, page, h, dh = cache_k.shape
    n_pages = page_table.shape[1]
    assert h * dh == a and MOBA_BLOCK % page == 0 and (n_pages * page) % MOBA_BLOCK == 0
    assert n_seq % SEQS_PER_STEP == 0 and (w_gate.shape[1] // 2) % LANES == 0
    steps = n_seq // SEQS_PER_STEP
    assert n % steps == 0
    tm = n // steps
    per_seq = a_t.shape[2] // tm
    assert tm % LANES == 0 and a_t.shape[2] % tm == 0 and a_t.shape[0] * a_t.shape[2] == n
    slopes = jnp.asarray(np.broadcast_to(_alibi_slopes(N_HEADS).reshape(N_HEADS, 1), (N_HEADS, LANES)).copy())
    ck = jnp.transpose(cache_k, (0, 2, 3, 1))
    cv = jnp.transpose(cache_v, (0, 2, 3, 1))
    tok_t = lambda z: jnp.transpose(z.reshape(n_seq, h, dh), (0, 2, 1))
    row = lambda w: pl.BlockSpec((tm, w), lambda i, pt: (i, 0))
    tok = pl.BlockSpec((SEQS_PER_STEP, dh, h), lambda i, pt: (i, 0, 0))
    hbm = pl.BlockSpec(memory_space=pl.ANY)
    grid_spec = pltpu.PrefetchScalarGridSpec(
        num_scalar_prefetch=1,
        grid=(steps,),
        in_specs=[row(d), pl.BlockSpec((1, ATTN_WIDTH, tm), lambda i, pt: (i // per_seq, 0, i % per_seq)),
                  row(SGU_WIDTH), _const_spec((1, ATTN_WIDTH)), _const_spec(w_out.shape), _const_spec((1, d)),
                  _const_spec(w_gate.shape), _const_spec(w_up.shape), _const_spec(w_down.shape),
                  _const_spec((1, d)),
                  tok, tok, tok, pl.BlockSpec((N_HEADS, LANES), lambda i, pt: (0, 0)), hbm, hbm],
        out_specs=(row(d), tok),
        scratch_shapes=[pltpu.VMEM((SEQS_PER_STEP, n_pages, h, dh, page), F32),
                        pltpu.VMEM((SEQS_PER_STEP, n_pages, h, dh, page), F32),
                        pltpu.SemaphoreType.DMA((SEQS_PER_STEP, 2)),
                        pltpu.VMEM((tm, d), F32),
                        pltpu.VMEM((tm, d), BF16),
                        pltpu.VMEM((tm, d), F32)],
    )
    y, out = pl.pallas_call(
        _outffn_attn_sample_kernel,
        grid_spec=grid_spec,
        out_shape=(jax.ShapeDtypeStruct((n, d), F32), jax.ShapeDtypeStruct((n_seq, dh, h), F32)),
        compiler_params=pltpu.CompilerParams(dimension_semantics=("arbitrary",),
                                             vmem_limit_bytes=56 * 1024 * 1024),
        name="outffn_prompt_attn_sample",
    )(page_table, x, a_t, gn, ag.reshape(1, -1), w_out, g2.reshape(1, -1), w_gate, w_up, w_down,
      gf.reshape(1, -1), tok_t(q), tok_t(k_new), tok_t(v_new), slopes, ck, cv)
    return y, jnp.transpose(out, (0, 2, 1)).reshape(n_seq, a)


def kernel(x_prompt, x_sample, cache_k, cache_v, page_table, norm1_g, w_in, attn_out_g, sgu_ln_g, sgu_ln_b,
           sgu_w, sgu_b, sgu_out_g, w_out, norm2_g, w_gate, w_up, w_down, final_g):
    depth = w_in.shape[0]
    assert depth == 1, "single-layer stack"
    l = 0
    bsz, seq, d = x_prompt.shape
    dec_b, dec_seq, _ = x_sample.shape
    assert dec_seq == 1
    xs = x_sample.reshape(dec_b, d)

    qp_t, kb, kt, vt, vtb, kmean, gnp, wo, wg, wu, wd = _inproj_prompt(
        x_prompt, norm1_g[l], w_in[l], sgu_ln_g[l], sgu_ln_b[l], sgu_w[l], sgu_b[l], sgu_out_g[l],
        (w_out[l], w_gate[l], w_up[l], w_down[l]), tm=512)
    ap_t = _attn_prompt(qp_t, kb, vtb, kmean)

    qs, ks, vs, gns, vns = _inproj_sample(xs, norm1_g[l], w_in[l], sgu_ln_g[l], sgu_ln_b[l], sgu_w[l], sgu_b[l],
                                          sgu_out_g[l])
    yp, a_s = _outffn_with_sample_attention(
        x_prompt.reshape(bsz * seq, d), ap_t, gnp.reshape(bsz * seq, -1), attn_out_g[l], wo, norm2_g[l],
        wg, wu, wd, final_g, qs, ks, vs, cache_k[l], cache_v[l], page_table)
    ys = _outffn(xs, a_s, gns, attn_out_g[l], wo, norm2_g[l], wg, wu, wd, final_g)

    heads_last = lambda x_t: jnp.transpose(x_t.reshape(bsz, N_HEADS, HEAD_DIM, seq), (0, 3, 1, 2))[None]
    y_prompt = yp.reshape(bsz, seq, d)
    y_sample = ys.reshape(dec_b, dec_seq, d)
    k_prompt = heads_last(kt)
    v_prompt = heads_last(vt)
    k_sample = ks.reshape(depth, dec_b, dec_seq, N_HEADS, HEAD_DIM)
    v_sample = vs.reshape(depth, dec_b, dec_seq, N_HEADS, HEAD_DIM)
    sgu_v_sample = vns.reshape(depth, dec_b, dec_seq, SGU_WIDTH)
    return (y_prompt, y_sample, k_prompt, v_prompt, k_sample, v_sample, sgu_v_sample)
```

```python
import functools
import math

import numpy as np
import jax
import jax.numpy as jnp
from jax import lax
from jax.experimental import pallas as pl
from jax.experimental.pallas import tpu as pltpu

HEAD_DIM = 64
N_HEADS = 8
ATTN_WIDTH = N_HEADS * HEAD_DIM
N_SGU_GROUPS = 8
SGU_GROUP_DIM = 64
SGU_WIDTH = N_SGU_GROUPS * SGU_GROUP_DIM
SGU_CHUNK = 128
MOBA_BLOCK = 256
MOBA_TOPK = 3
RMS_EPS = 1e-6
LN_EPS = 1e-5
NEG_INF = -1e30
LOG2E = 1.4426950408889634

LANES = 128
BF16_SUBLANES = 16
BIAS_LANES = BF16_SUBLANES
Q_BLOCKS = 2
HEADS_PER_TILE = LANES // HEAD_DIM

V7X_VMEM_BYTES = 64 * 1024 * 1024
VMEM_LIMIT_WEIGHT_RESIDENT = V7X_VMEM_BYTES * 7 // 8
VMEM_LIMIT_ATTENTION = V7X_VMEM_BYTES * 3 // 4

F32 = jnp.float32
BF16 = jnp.bfloat16

_NN = (((1,), (0,)), ((), ()))


def _dot(a, b, dims=_NN, precision=None):
    return lax.dot_general(a, b, dims, precision=precision, preferred_element_type=F32)


def _rmsnorm(x, g):
    r = lax.rsqrt(jnp.mean(x * x, axis=-1, keepdims=True) + RMS_EPS)
    return (x * r) * g


def _gelu_tanh(x):
    c = math.sqrt(2.0 / math.pi)
    return 0.5 * x * (1.0 + jnp.tanh(c * (x + 0.044715 * (x * x * x))))


def _alibi_slopes(n):
    start = 2.0 ** (-8.0 / n)
    return np.asarray([start ** (i + 1) for i in range(n)], dtype=np.float32)


def _split_weights(w_ref, whi_ref, wlo_ref, n_precise):
    rows = w_ref.shape[0]
    step = 128
    for r in range(0, rows, step):
        w = w_ref[r:r + step, :]
        hi = w.astype(BF16)
        whi_ref[r:r + step, :] = hi
        wlo_ref[r:r + step, :] = (w[:, :n_precise] - hi[:, :n_precise].astype(F32)).astype(BF16)


def _project(x, g1_ref, whi_ref, wlo_ref):
    n_precise = wlo_ref.shape[1]
    h = _rmsnorm(x, g1_ref[...])
    h_hi = h.astype(BF16)
    h_lo = (h - h_hi.astype(F32)).astype(BF16)
    w_qk = whi_ref[:, :n_precise]
    z_qk = _dot(h_hi, w_qk) + (_dot(h_lo, w_qk) + _dot(h_hi, wlo_ref[...]))
    z_rest = _dot(h_hi, whi_ref[:, n_precise:])
    return z_qk, z_rest


def _layernorm(x, g, b):
    mu = jnp.mean(x, axis=-1, keepdims=True)
    xc = x - mu
    var = jnp.mean(xc * xc, axis=-1, keepdims=True)
    return (xc * lax.rsqrt(var + LN_EPS)) * g + b


def _inproj_prompt_kernel(x_ref, g1_ref, w_ref, lng_ref, lnb_ref, sw_ref, sbx_ref, sog_ref, *refs):
    later_w = refs[:N_LATER_WEIGHTS]
    q_ref, kb_ref, kt_ref, vt_ref, vtb_ref, kmean_ref, gn_ref = refs[N_LATER_WEIGHTS:N_LATER_WEIGHTS + 7]
    later_wb = refs[N_LATER_WEIGHTS + 7:2 * N_LATER_WEIGHTS + 7]
    wb_ref, wcat_ref, s_ref = refs[2 * N_LATER_WEIGHTS + 7:]
    tm = x_ref.shape[1]
    a = ATTN_WIDTH
    blk = MOBA_BLOCK

    for src, dst in zip(later_w, later_wb):
        dst[...] = src[...].astype(BF16)

    @pl.when((pl.program_id(0) == 0) & (pl.program_id(1) == 0))
    def _prepare():
        for r in range(0, w_ref.shape[0], LANES):
            wb_ref[r:r + LANES, :] = w_ref[r:r + LANES, :].astype(BF16)
        t = lax.broadcasted_iota(jnp.int32, (SGU_CHUNK, SGU_CHUNK), 0)
        s = lax.broadcasted_iota(jnp.int32, (SGU_CHUNK, SGU_CHUNK), 1)
        causal = t >= s
        for gp in range(N_SGU_GROUPS // 2):
            w0 = jnp.where(causal, sw_ref[2 * gp], 0.0)
            w1 = jnp.where(causal, sw_ref[2 * gp + 1], 0.0)
            wcat_ref[gp] = jnp.concatenate([w0, w1], axis=1).astype(BF16)

    z = _dot(_rmsnorm(x_ref[0], g1_ref[...]).astype(BF16), wb_ref[...])
    q_ref[0] = z[:, :a].T
    zk = z[:, a:2 * a]
    zv = z[:, 2 * a:3 * a]
    z_rest = z[:, 2 * a:]
    kt_ref[0] = zk.T
    vt = zv.T
    vt_ref[0] = vt
    for r in range(tm // blk):
        rows = slice(r * blk, (r + 1) * blk)
        kb_ref[0, r] = zk[rows, :].astype(BF16)
        vtb_ref[0, r] = vt[:, rows].astype(BF16)
        kmean_ref[r] = jnp.mean(zk[rows, :], axis=0, keepdims=True)

    u = _gelu_tanh(z_rest[:, a:a + SGU_WIDTH])
    gv = _gelu_tanh(z_rest[:, a + SGU_WIDTH:])
    vn = _layernorm(gv, lng_ref[...], lnb_ref[...])

    lane = lax.broadcasted_iota(jnp.int32, (SGU_CHUNK, LANES), 1)
    low = lane < SGU_GROUP_DIM
    n_chunks = tm // SGU_CHUNK
    for c in range(0, n_chunks, 2):
        both = range(c, min(c + 2, n_chunks))
        for gp in range(N_SGU_GROUPS // 2):
            cols = slice(gp * LANES, (gp + 1) * LANES)
            rhs = []
            for cc in both:
                vp = vn[cc * SGU_CHUNK:(cc + 1) * SGU_CHUNK, cols]
                rhs.append(jnp.concatenate([jnp.where(low, vp, 0.0), jnp.where(low, 0.0, vp)], axis=0))
            s = _dot(wcat_ref[gp], jnp.concatenate(rhs, axis=1).astype(BF16))
            for k, cc in enumerate(both):
                s_ref[cc * SGU_CHUNK:(cc + 1) * SGU_CHUNK, cols] = s[:, k * LANES:(k + 1) * LANES] + sbx_ref[:, cols]
    g = u * s_ref[...]
    gn_ref[0] = _rmsnorm(g, sog_ref[...]).astype(gn_ref.dtype)


def _inproj_sample_kernel(x_ref, g1_ref, w_ref, lng_ref, lnb_ref, w00_ref, b0_ref, sog_ref,
                          q_ref, k_ref, v_ref, gn_ref, vn_ref,
                          whi_ref, wlo_ref):
    a = ATTN_WIDTH

    @pl.when(pl.program_id(0) == 0)
    def _prepare():
        _split_weights(w_ref, whi_ref, wlo_ref, 2 * a)

    z_qk, z_rest = _project(x_ref[...], g1_ref, whi_ref, wlo_ref)
    q_ref[...] = z_qk[:, :a]
    k_ref[...] = z_qk[:, a:]
    v_ref[...] = z_rest[:, :a]
    u = _gelu_tanh(z_rest[:, a:a + SGU_WIDTH])
    gv = _gelu_tanh(z_rest[:, a + SGU_WIDTH:])
    vn = _layernorm(gv, lng_ref[...], lnb_ref[...])
    vn_ref[...] = vn
    g = u * (vn * w00_ref[...] + b0_ref[...])
    gn_ref[...] = _rmsnorm(g, sog_ref[...]).astype(gn_ref.dtype)


def _const_spec(shape):
    zeros = (0,) * len(shape)
    return pl.BlockSpec(shape, lambda *_: zeros, pipeline_mode=pl.Buffered(1))


N_LATER_WEIGHTS = 4


def _inproj_prompt(x, g1, w_in, ln_g, ln_b, sgu_w, sgu_b, sog, later_weights, *, tm):
    b, t, d = x.shape
    a = ATTN_WIDTH
    blk = MOBA_BLOCK
    assert t % tm == 0 and tm % SGU_CHUNK == 0 and tm % blk == 0 and len(later_weights) == N_LATER_WEIGHTS
    nb = t // blk
    per = tm // blk
    steps = t // tm
    assert all(w.shape[0] % (steps * BF16_SUBLANES) == 0 and w.shape[1] % (b * LANES) == 0 for w in later_weights)
    w_spec = lambda w: pl.BlockSpec((w.shape[0] // steps, w.shape[1] // b), lambda bi, i: (i, bi))
    sbx = jnp.repeat(sgu_b.T, SGU_GROUP_DIM, axis=1)
    row = lambda w: pl.BlockSpec((1, tm, w), lambda bi, i: (bi, i, 0))
    col = lambda w: pl.BlockSpec((1, w, tm), lambda bi, i: (bi, 0, i))
    out_specs = (col(a),
                 pl.BlockSpec((1, per, blk, a), lambda bi, i: (bi, i, 0, 0)),
                 col(a), col(a),
                 pl.BlockSpec((1, per, a, blk), lambda bi, i: (bi, i, 0, 0)),
                 pl.BlockSpec((per, 1, a), lambda bi, i: (bi * (nb // per) + i, 0, 0)),
                 row(SGU_WIDTH)) + tuple(w_spec(w) for w in later_weights)
    out_shape = (jax.ShapeDtypeStruct((b, a, t), F32),
                 jax.ShapeDtypeStruct((b, nb, blk, a), BF16),
                 jax.ShapeDtypeStruct((b, a, t), F32),
                 jax.ShapeDtypeStruct((b, a, t), F32),
                 jax.ShapeDtypeStruct((b, nb, a, blk), BF16),
                 jax.ShapeDtypeStruct((b * nb, 1, a), F32),
                 jax.ShapeDtypeStruct((b, t, SGU_WIDTH), BF16)) + tuple(
                     jax.ShapeDtypeStruct(w.shape, BF16) for w in later_weights)
    return pl.pallas_call(
        _inproj_prompt_kernel,
        grid=(b, steps),
        in_specs=[row(d), _const_spec((1, d)), _const_spec(w_in.shape), _const_spec((1, SGU_WIDTH)),
                  _const_spec((1, SGU_WIDTH)), _const_spec(sgu_w.shape), _const_spec(sbx.shape),
                  _const_spec((1, SGU_WIDTH))] + [w_spec(w) for w in later_weights],
        out_specs=out_specs,
        out_shape=out_shape,
        scratch_shapes=[pltpu.VMEM(w_in.shape, BF16),
                        pltpu.VMEM((N_SGU_GROUPS // 2, SGU_CHUNK, 2 * SGU_CHUNK), BF16),
                        pltpu.VMEM((tm, SGU_WIDTH), F32)],
        compiler_params=pltpu.CompilerParams(dimension_semantics=("arbitrary", "arbitrary"),
                                             vmem_limit_bytes=VMEM_LIMIT_WEIGHT_RESIDENT),
        name="inproj_prompt",
    )(x, g1.reshape(1, d), w_in, ln_g.reshape(1, -1), ln_b.reshape(1, -1), sgu_w, sbx, sog.reshape(1, -1),
      *later_weights)


def _inproj_sample(x, g1, w_in, ln_g, ln_b, sgu_w, sgu_b, sog):
    n, d = x.shape
    a = ATTN_WIDTH
    w00 = jnp.repeat(sgu_w[:, 0, 0], SGU_GROUP_DIM).reshape(1, SGU_WIDTH)
    b0 = jnp.repeat(sgu_b[:, 0], SGU_GROUP_DIM).reshape(1, SGU_WIDTH)
    row = lambda w: pl.BlockSpec((n, w), lambda i: (0, 0))
    vec = _const_spec((1, SGU_WIDTH))
    out_shape = (jax.ShapeDtypeStruct((n, a), F32),) * 3 + (jax.ShapeDtypeStruct((n, SGU_WIDTH), BF16),
                                                           jax.ShapeDtypeStruct((n, SGU_WIDTH), F32))
    return pl.pallas_call(
        _inproj_sample_kernel,
        grid=(1,),
        in_specs=[row(d), _const_spec((1, d)), _const_spec(w_in.shape), vec, vec, vec, vec, vec],
        out_specs=(row(a), row(a), row(a), row(SGU_WIDTH), row(SGU_WIDTH)),
        out_shape=out_shape,
        scratch_shapes=[pltpu.VMEM(w_in.shape, BF16), pltpu.VMEM((d, 2 * a), BF16)],
        compiler_params=pltpu.CompilerParams(dimension_semantics=("arbitrary",),
                                             vmem_limit_bytes=VMEM_LIMIT_WEIGHT_RESIDENT),
        name="inproj_sample",
    )(x, g1.reshape(1, d), w_in, ln_g.reshape(1, -1), ln_b.reshape(1, -1), w00, b0, sog.reshape(1, -1))


def _block_choice(gate_t, own):
    nb = gate_t.shape[0]
    blk = lax.broadcasted_iota(jnp.int32, gate_t.shape, 0)
    g = jnp.where(blk < own, gate_t, NEG_INF)
    picked = jnp.zeros(gate_t.shape, F32)
    for _ in range(MOBA_TOPK):
        top = jnp.max(g, axis=0, keepdims=True)
        first = jnp.min(jnp.where(g == top, blk, nb), axis=0, keepdims=True)
        hit = blk == first
        picked = jnp.where(hit, 1.0, picked)
        g = jnp.where(hit, -jnp.inf, g)
    return jnp.where(blk < own, picked, 0.0)


def _split3(x):
    hi = x.astype(BF16)
    r = x - hi.astype(F32)
    mid = r.astype(BF16)
    return hi, mid, (r - mid.astype(F32)).astype(BF16)


def _attn_prompt_kernel(q_ref, kb_ref, vtb_ref, km_ref, sl_ref, o_ref,
                        kmh_ref, kml_ref, kl_ref, qs_ref, sel_ref, acc_ref, m_ref, l_ref, s_ref, cm_ref, d_ref):
    i = pl.program_id(1)
    blk = MOBA_BLOCK
    nb = kb_ref.shape[1]
    tq = q_ref.shape[2]
    scale2 = LOG2E / math.sqrt(HEAD_DIM)

    @pl.when(i == 0)
    def _prepare():
        km = km_ref[:, 0, :]
        lane_head = lax.broadcasted_iota(jnp.int32, km.shape, 1) // HEAD_DIM
        for h in range(N_HEADS):
            mine = jnp.where(lane_head == h, km, 0.0)
            hi = mine.astype(BF16)
            kmh_ref[h * nb:(h + 1) * nb, :] = hi
            kml_ref[h * nb:(h + 1) * nb, :] = (mine - hi.astype(F32)).astype(BF16)
        lane = lax.broadcasted_iota(jnp.int32, (blk, BIAS_LANES), 1)
        kl = lax.broadcasted_iota(jnp.int32, (blk, BIAS_LANES), 0).astype(F32)
        kl_ref[...] = jnp.where(lane < 3, kl, 0.0).astype(BF16)
        row = lax.broadcasted_iota(jnp.int32, (BIAS_LANES, tq), 0)
        for h in range(N_HEADS):
            hi, mid, lo = (x.astype(F32) for x in _split3(jnp.broadcast_to(sl_ref[h:h + 1, :], (BIAS_LANES, tq))))
            terms = jnp.where(row == 0, hi, jnp.where(row == 1, mid, jnp.where(row == 2, lo, 0.0)))
            qs_ref[h, LANES:, :] = terms.astype(BF16)

    q_t = q_ref[0]
    first_own = Q_BLOCKS * i
    own = first_own + lax.broadcasted_iota(jnp.int32, (1, tq), 1) // blk
    q_hi = q_t.astype(BF16)
    q_lo = (q_t - q_hi.astype(F32)).astype(BF16)
    gate_t = _dot(kmh_ref[...], q_hi) + (_dot(kml_ref[...], q_hi) + _dot(kmh_ref[...], q_lo))
    row = lax.broadcasted_iota(jnp.int32, (LANES, tq), 0)
    for h in range(N_HEADS):
        pair, hh = divmod(h, HEADS_PER_TILE)
        q_pair = q_t[pair * LANES:(pair + 1) * LANES, :]
        mine = (row >= hh * HEAD_DIM) & (row < (hh + 1) * HEAD_DIM)
        qs_ref[h, :LANES, :] = (jnp.where(mine, q_pair, 0.0) * scale2).astype(BF16)

    def scores(j, h, first_query=0):
        pair = h // HEADS_PER_TILE
        keys = jnp.concatenate([kb_ref[0, j, :, pair * LANES:(pair + 1) * LANES], kl_ref[...]], axis=1)
        return _dot(keys, qs_ref[h, :, first_query:])

    ones_rows = jnp.ones((BF16_SUBLANES, blk), BF16)

    def value_product(j, h, p):
        rows = jnp.concatenate([vtb_ref[0, j, h * HEAD_DIM:(h + 1) * HEAD_DIM, :], ones_rows], axis=0)
        pv = _dot(rows, p.astype(BF16))
        return pv[:HEAD_DIM], pv[HEAD_DIM:HEAD_DIM + 1]

    def issue_scores(j, slot, h):
        s = scores(j, h)
        s_ref[slot, h] = s
        cm_ref[slot, h] = jnp.max(s, axis=0, keepdims=True)

    def block_bias(j, h):
        return sl_ref[h:h + 1, :] * ((j - own).astype(F32) * float(blk))

    for d in range(Q_BLOCKS):
        for h in range(N_HEADS):
            d_ref[d, h, :, d * blk:] = scores(first_own + d, h, first_query=d * blk)
    for h in range(N_HEADS):
        issue_scores(0, 0, h)
    for h in range(N_HEADS):
        sel_ref[h] = _block_choice(gate_t[h * nb:(h + 1) * nb, :], own)

    causal = lax.broadcasted_iota(jnp.int32, (blk, blk), 0) <= lax.broadcasted_iota(jnp.int32, (blk, blk), 1)
    for h in range(N_HEADS):
        for a in range(Q_BLOCKS):
            lanes = slice(a * blk, (a + 1) * blk)
            s = jnp.where(causal, d_ref[a, h, :, lanes], NEG_INF)
            m = jnp.max(s, axis=0, keepdims=True)
            acc, l = value_product(first_own + a, h, jnp.exp2(s - m))
            for d in range(a):
                j = first_own + d
                c = block_bias(j, h)[:, lanes]
                chosen = sel_ref[h, pl.ds(j, 1), lanes] > 0.5
                s = d_ref[d, h, :, lanes]
                m_new = jnp.where(chosen, jnp.maximum(m, jnp.max(s, axis=0, keepdims=True) + c), m)
                alpha = jnp.exp2(m - m_new)
                pv, p_sum = value_product(j, h, jnp.exp2(s - jnp.where(chosen, m_new - c, -NEG_INF)))
                m, l, acc = m_new, alpha * l + p_sum, alpha * acc + pv
            m_ref[h, :, lanes] = m
            l_ref[h, :, lanes] = l
            acc_ref[h, :, lanes] = acc

    def finish_block(j, slot, h):
        m, l = m_ref[h], l_ref[h]
        c = block_bias(j, h)
        chosen = sel_ref[h, pl.ds(j, 1), :] > 0.5
        m_new = jnp.where(chosen, jnp.maximum(m, cm_ref[slot, h] + c), m)
        alpha = jnp.exp2(m - m_new)
        shift = jnp.where(chosen, m_new - c, -NEG_INF)
        pv, p_sum = value_product(j, h, jnp.exp2(s_ref[slot, h] - shift))
        m_ref[h] = m_new
        l_ref[h] = alpha * l + p_sum
        acc_ref[h] = alpha * acc_ref[h] + pv

    def trip(t, carry):
        for slot in range(2):
            j = 2 * t + slot
            for h in range(N_HEADS):
                issue_scores(jnp.minimum(j + 1, nb - 1), 1 - slot, h)
                finish_block(j, slot, h)
        return carry

    lax.fori_loop(0, first_own // 2, trip, 0)
    for h in range(N_HEADS):
        o_ref[0, h * HEAD_DIM:(h + 1) * HEAD_DIM, :] = acc_ref[h] / l_ref[h]


def _attn_prompt(q_t, kb, vtb, kmean):
    b, a, t = q_t.shape
    blk = MOBA_BLOCK
    nb = t // blk
    tq = Q_BLOCKS * blk
    assert Q_BLOCKS % 2 == 0 and t % tq == 0 and a == ATTN_WIDTH and kb.shape == (b, nb, blk, a) and vtb.shape == (b, nb, a, blk)
    slopes2 = (_alibi_slopes(N_HEADS) * np.float32(LOG2E)).reshape(N_HEADS, 1)
    slopes2 = jnp.asarray(np.broadcast_to(slopes2, (N_HEADS, tq)).copy())
    once = pl.Buffered(1)
    return pl.pallas_call(
        _attn_prompt_kernel,
        grid=(b, t // tq),
        in_specs=[pl.BlockSpec((1, a, tq), lambda bi, i: (bi, 0, i)),
                  pl.BlockSpec((1, nb, blk, a), lambda bi, i: (bi, 0, 0, 0), pipeline_mode=once),
                  pl.BlockSpec((1, nb, a, blk), lambda bi, i: (bi, 0, 0, 0), pipeline_mode=once),
                  pl.BlockSpec((nb, 1, a), lambda bi, i: (bi, 0, 0)),
                  pl.BlockSpec((N_HEADS, tq), lambda bi, i: (0, 0))],
        out_specs=pl.BlockSpec((1, a, tq), lambda bi, i: (bi, 0, i)),
        out_shape=jax.ShapeDtypeStruct((b, a, t), F32),
        scratch_shapes=[pltpu.VMEM((N_HEADS * nb, a), BF16),
                        pltpu.VMEM((N_HEADS * nb, a), BF16),
                        pltpu.VMEM((blk, BIAS_LANES), BF16),
                        pltpu.VMEM((N_HEADS, LANES + BIAS_LANES, tq), BF16),
                        pltpu.VMEM((N_HEADS, nb, tq), F32),
                        pltpu.VMEM((N_HEADS, HEAD_DIM, tq), F32),
                        pltpu.VMEM((N_HEADS, 1, tq), F32),
                        pltpu.VMEM((N_HEADS, 1, tq), F32),
                        pltpu.VMEM((2, N_HEADS, blk, tq), F32),
                        pltpu.VMEM((2, N_HEADS, 1, tq), F32),
                        pltpu.VMEM((Q_BLOCKS, N_HEADS, blk, tq), F32)],
        compiler_params=pltpu.CompilerParams(dimension_semantics=("arbitrary", "arbitrary"),
                                             vmem_limit_bytes=VMEM_LIMIT_ATTENTION),
        name="attn_prompt",
    )(q_t, kb, vtb, kmean, slopes2)


def _sample_attention(q_t, kn_t, vn_t, slopes, k_tile, v_tile, n_pages, page):
    past = n_pages * page
    pages_per_block = MOBA_BLOCK // page
    nb = past // MOBA_BLOCK
    scale = 1.0 / math.sqrt(HEAD_DIM)

    head_row = lax.broadcasted_iota(jnp.int32, (N_HEADS, page), 0)
    raw = [jnp.zeros((N_HEADS, page), F32) for _ in range(n_pages + 1)]
    for h in range(N_HEADS):
        q_b = jnp.broadcast_to(q_t[:, h:h + 1], (HEAD_DIM, page))
        tiles = [k_tile(pg, h) for pg in range(n_pages)] + [jnp.broadcast_to(kn_t[:, h:h + 1], (HEAD_DIM, page))]
        for pg, tile in enumerate(tiles):
            raw[pg] = jnp.where(head_row == h, jnp.sum(q_b * tile, axis=0, keepdims=True), raw[pg])
    gate = []
    for n in range(nb):
        tot = raw[n * pages_per_block]
        for r in range(1, pages_per_block):
            tot = tot + raw[n * pages_per_block + r]
        gate.append(jnp.sum(tot, axis=1, keepdims=True) * (1.0 / MOBA_BLOCK))
    chosen = []
    for n in range(nb):
        rank = jnp.where(NEG_INF > gate[n], 1.0, 0.0)
        for m in range(nb):
            if m != n:
                ahead = (gate[m] >= gate[n]) if m < n else (gate[m] > gate[n])
                rank = rank + jnp.where(ahead, 1.0, 0.0)
        chosen.append(jnp.where(rank < MOBA_TOPK, 1.0, 0.0))

    lane = lax.broadcasted_iota(jnp.int32, (N_HEADS, page), 1)
    dist0 = (past - lane).astype(F32)
    s_pages = []
    for pg in range(n_pages):
        s = raw[pg] * scale - slopes * (dist0 - float(pg * page))
        ok = jnp.broadcast_to(chosen[pg // pages_per_block], s.shape) > 0.5
        s_pages.append(jnp.where(ok, s, NEG_INF))
    s_pages.append(jnp.where(lane == 0, raw[n_pages] * scale, NEG_INF))
    m = jnp.max(s_pages[0], axis=1, keepdims=True)
    for s in s_pages[1:]:
        m = jnp.maximum(m, jnp.max(s, axis=1, keepdims=True))
    p_pages = [jnp.exp(s - m) for s in s_pages]
    l = jnp.sum(p_pages[0], axis=1, keepdims=True)
    for p in p_pages[1:]:
        l = l + jnp.sum(p, axis=1, keepdims=True)
    outs = []
    for h in range(N_HEADS):
        acc = jnp.broadcast_to(p_pages[n_pages][h:h + 1, :], (HEAD_DIM, page)) * \
            jnp.broadcast_to(vn_t[:, h:h + 1], (HEAD_DIM, page))
        for pg in range(n_pages):
            acc = acc + jnp.broadcast_to(p_pages[pg][h:h + 1, :], (HEAD_DIM, page)) * v_tile(pg, h)
        outs.append(jnp.sum(acc, axis=1, keepdims=True) / l[h:h + 1, :])
    return jnp.concatenate(outs, axis=1)


SEQS_PER_STEP = 2


def _out_proj(x_ref, a_ref, gn_ref, ag_ref, wo_ref, g2_ref):
    a = ATTN_WIDTH
    attn = a_ref[0].T if len(a_ref.shape) == 3 else a_ref[...]
    an = _rmsnorm(attn, ag_ref[...]).astype(BF16)
    x1 = x_ref[...] + (_dot(an, wo_ref[:a, :]) + _dot(gn_ref[...], wo_ref[a:, :]))
    return x1, _rmsnorm(x1, g2_ref[...]).astype(BF16)


def _ffn_half(hf, wg_ref, wu_ref, wd_ref, half):
    w = wg_ref.shape[1] // 2
    cols = slice(half * w, (half + 1) * w)
    gate = _dot(hf, wg_ref[:, cols])
    up = _dot(hf, wu_ref[:, cols])
    act = (gate * jax.nn.sigmoid(gate) * up).astype(BF16)
    return _dot(act, wd_ref[cols, :])


def _outffn_kernel(x_ref, a_ref, gn_ref, ag_ref, wo_ref, g2_ref, wg_ref, wu_ref, wd_ref, gf_ref, y_ref):
    x1, hf = _out_proj(x_ref, a_ref, gn_ref, ag_ref, wo_ref, g2_ref)
    ff = _ffn_half(hf, wg_ref, wu_ref, wd_ref, 0) + _ffn_half(hf, wg_ref, wu_ref, wd_ref, 1)
    y_ref[...] = _rmsnorm(x1 + ff, gf_ref[...])


def _outffn_attn_sample_kernel(pt_ref, x_ref, a_ref, gn_ref, ag_ref, wo_ref, g2_ref, wg_ref, wu_ref, wd_ref,
                               gf_ref, q_ref, kn_ref, vn_ref, sl_ref, ck_hbm, cv_hbm, y_ref, o_ref,
                               kbuf, vbuf, sem, x1_ref, hf_ref, ff_ref):
    t = pl.program_id(0)
    last_seq = SEQS_PER_STEP * pl.num_programs(0) - 1
    n_pages, page = kbuf.shape[1], kbuf.shape[-1]

    def page_copies(seq, slot):
        copies = []
        for pg in range(n_pages):
            pid = pt_ref[seq, pg]
            copies.append(pltpu.make_async_copy(ck_hbm.at[pid], kbuf.at[slot, pg], sem.at[slot, 0]))
            copies.append(pltpu.make_async_copy(cv_hbm.at[pid], vbuf.at[slot, pg], sem.at[slot, 1]))
        return copies

    def fetch(seq, slot):
        for c in page_copies(seq, slot):
            c.start()

    def wait(seq, slot):
        for c in page_copies(seq, slot):
            c.wait()

    def attend(slot):
        o_ref[slot] = _sample_attention(q_ref[slot], kn_ref[slot], vn_ref[slot], sl_ref[:, :1],
                                        lambda pg, h: kbuf[slot, pg, h], lambda pg, h: vbuf[slot, pg, h],
                                        n_pages, page)

    @pl.when(t == 0)
    def _first():
        fetch(0, 0)

    seq0 = SEQS_PER_STEP * t
    wait(seq0, 0)
    fetch(seq0 + 1, 1)
    x1, hf = _out_proj(x_ref, a_ref, gn_ref, ag_ref, wo_ref, g2_ref)
    x1_ref[...] = x1
    hf_ref[...] = hf
    ff_ref[...] = _ffn_half(hf, wg_ref, wu_ref, wd_ref, 0)
    attend(0)

    wait(seq0 + 1, 1)
    nxt = jnp.minimum(seq0 + 2, last_seq)
    fetch(nxt, 0)
    ff = ff_ref[...] + _ffn_half(hf_ref[...], wg_ref, wu_ref, wd_ref, 1)
    y_ref[...] = _rmsnorm(x1_ref[...] + ff, gf_ref[...])
    attend(1)

    @pl.when(t == pl.num_programs(0) - 1)
    def _drain():
        wait(last_seq, 0)


def _outffn(x, a, gn, ag, w_out, g2, w_gate, w_up, w_down, gf):
    n, d = x.shape
    assert (w_gate.shape[1] // 2) % LANES == 0
    row = lambda w: pl.BlockSpec((n, w), lambda i: (0, 0))
    return pl.pallas_call(
        _outffn_kernel,
        grid=(1,),
        in_specs=[row(d), row(ATTN_WIDTH), row(SGU_WIDTH), _const_spec((1, ATTN_WIDTH)), _const_spec(w_out.shape),
                  _const_spec((1, d)), _const_spec(w_gate.shape), _const_spec(w_up.shape),
                  _const_spec(w_down.shape), _const_spec((1, d))],
        out_specs=row(d),
        out_shape=jax.ShapeDtypeStruct((n, d), F32),
        compiler_params=pltpu.CompilerParams(dimension_semantics=("arbitrary",),
                                             vmem_limit_bytes=VMEM_LIMIT_WEIGHT_RESIDENT),
        name="outffn_sample",
    )(x, a, gn, ag.reshape(1, -1), w_out, g2.reshape(1, -1), w_gate, w_up, w_down, gf.reshape(1, -1))


def _outffn_with_sample_attention(x, a_t, gn, ag, w_out, g2, w_gate, w_up, w_down, gf,
                                  q, k_new, v_new, cache_k, cache_v, page_table):
    n, d = x.shape
    n_seq, a = q.shape
    n_pool, page, h, dh = cache_k.shape
    n_pages = page_table.shape[1]
    assert h * dh == a and MOBA_BLOCK % page == 0 and (n_pages * page) % MOBA_BLOCK == 0
    assert n_seq % SEQS_PER_STEP == 0 and (w_gate.shape[1] // 2) % LANES == 0
    steps = n_seq // SEQS_PER_STEP
    assert n % steps == 0
    tm = n // steps
    per_seq = a_t.shape[2] // tm
    assert tm % LANES == 0 and a_t.shape[2] % tm == 0 and a_t.shape[0] * a_t.shape[2] == n
    slopes = jnp.asarray(np.broadcast_to(_alibi_slopes(N_HEADS).reshape(N_HEADS, 1), (N_HEADS, LANES)).copy())
    ck = jnp.transpose(cache_k, (0, 2, 3, 1))
    cv = jnp.transpose(cache_v, (0, 2, 3, 1))
    tok_t = lambda z: jnp.transpose(z.reshape(n_seq, h, dh), (0, 2, 1))
    row = lambda w: pl.BlockSpec((tm, w), lambda i, pt: (i, 0))
    tok = pl.BlockSpec((SEQS_PER_STEP, dh, h), lambda i, pt: (i, 0, 0))
    hbm = pl.BlockSpec(memory_space=pl.ANY)
    grid_spec = pltpu.PrefetchScalarGridSpec(
        num_scalar_prefetch=1,
        grid=(steps,),
        in_specs=[row(d), pl.BlockSpec((1, ATTN_WIDTH, tm), lambda i, pt: (i // per_seq, 0, i % per_seq)),
                  row(SGU_WIDTH), _const_spec((1, ATTN_WIDTH)), _const_spec(w_out.shape), _const_spec((1, d)),
                  _const_spec(w_gate.shape), _const_spec(w_up.shape), _const_spec(w_down.shape),
                  _const_spec((1, d)),
                  tok, tok, tok, pl.BlockSpec((N_HEADS, LANES), lambda i, pt: (0, 0)), hbm, hbm],
        out_specs=(row(d), tok),
        scratch_shapes=[pltpu.VMEM((SEQS_PER_STEP, n_pages, h, dh, page), F32),
                        pltpu.VMEM((SEQS_PER_STEP, n_pages, h, dh, page), F32),
                        pltpu.SemaphoreType.DMA((SEQS_PER_STEP, 2)),
                        pltpu.VMEM((tm, d), F32),
                        pltpu.VMEM((tm, d), BF16),
                        pltpu.VMEM((tm, d), F32)],
    )
    y, out = pl.pallas_call(
        _outffn_attn_sample_kernel,
        grid_spec=grid_spec,
        out_shape=(jax.ShapeDtypeStruct((n, d), F32), jax.ShapeDtypeStruct((n_seq, dh, h), F32)),
        compiler_params=pltpu.CompilerParams(dimension_semantics=("arbitrary",),
                                             vmem_limit_bytes=VMEM_LIMIT_WEIGHT_RESIDENT),
        name="outffn_prompt_attn_sample",
    )(page_table, x, a_t, gn, ag.reshape(1, -1), w_out, g2.reshape(1, -1), w_gate, w_up, w_down,
      gf.reshape(1, -1), tok_t(q), tok_t(k_new), tok_t(v_new), slopes, ck, cv)
    return y, jnp.transpose(out, (0, 2, 1)).reshape(n_seq, a)


def kernel(x_prompt, x_sample, cache_k, cache_v, page_table, norm1_g, w_in, attn_out_g, sgu_ln_g, sgu_ln_b,
           sgu_w, sgu_b, sgu_out_g, w_out, norm2_g, w_gate, w_up, w_down, final_g):
    depth = w_in.shape[0]
    assert depth == 1, "single-layer stack"
    l = 0
    bsz, seq, d = x_prompt.shape
    dec_b, dec_seq, _ = x_sample.shape
    assert dec_seq == 1
    xs = x_sample.reshape(dec_b, d)

    qp_t, kb, kt, vt, vtb, kmean, gnp, wo, wg, wu, wd = _inproj_prompt(
        x_prompt, norm1_g[l], w_in[l], sgu_ln_g[l], sgu_ln_b[l], sgu_w[l], sgu_b[l], sgu_out_g[l],
        (w_out[l], w_gate[l], w_up[l], w_down[l]), tm=512)
    ap_t = _attn_prompt(qp_t, kb, vtb, kmean)

    qs, ks, vs, gns, vns = _inproj_sample(xs, norm1_g[l], w_in[l], sgu_ln_g[l], sgu_ln_b[l], sgu_w[l], sgu_b[l],
                                          sgu_out_g[l])
    yp, a_s = _outffn_with_sample_attention(
        x_prompt.reshape(bsz * seq, d), ap_t, gnp.reshape(bsz * seq, -1), attn_out_g[l], wo, norm2_g[l],
        wg, wu, wd, final_g, qs, ks, vs, cache_k[l], cache_v[l], page_table)
    ys = _outffn(xs, a_s, gns, attn_out_g[l], wo, norm2_g[l], wg, wu, wd, final_g)

    heads_last = lambda x_t: jnp.transpose(x_t.reshape(bsz, N_HEADS, HEAD_DIM, seq), (0, 3, 1, 2))[None]
    y_prompt = yp.reshape(bsz, seq, d)
    y_sample = ys.reshape(dec_b, dec_seq, d)
    k_prompt = heads_last(kt)
    v_prompt = heads_last(vt)
    k_sample = ks.reshape(depth, dec_b, dec_seq, N_HEADS, HEAD_DIM)
    v_sample = vs.reshape(depth, dec_b, dec_seq, N_HEADS, HEAD_DIM)
    sgu_v_sample = vns.reshape(depth, dec_b, dec_seq, SGU_WIDTH)
    return (y_prompt, y_sample, k_prompt, v_prompt, k_sample, v_sample, sgu_v_sample)
```

```python
import functools
import math

import numpy as np
import jax
import jax.numpy as jnp
from jax import lax
from jax.experimental import pallas as pl
from jax.experimental.pallas import tpu as pltpu

HEAD_DIM = 64
N_HEADS = 8
ATTN_WIDTH = N_HEADS * HEAD_DIM
N_SGU_GROUPS = 8
SGU_GROUP_DIM = 64
SGU_WIDTH = N_SGU_GROUPS * SGU_GROUP_DIM
SGU_CHUNK = 128
MOBA_BLOCK = 256
MOBA_TOPK = 3
RMS_EPS = 1e-6
LN_EPS = 1e-5
NEG_INF = -1e30
LOG2E = 1.4426950408889634

LANES = 128
BF16_SUBLANES = 16
BIAS_LANES = BF16_SUBLANES
Q_BLOCKS = 2
HEADS_PER_TILE = LANES // HEAD_DIM

V7X_VMEM_BYTES = 64 * 1024 * 1024
VMEM_LIMIT_WEIGHT_RESIDENT = V7X_VMEM_BYTES * 7 // 8
VMEM_LIMIT_ATTENTION = V7X_VMEM_BYTES * 3 // 4

F32 = jnp.float32
BF16 = jnp.bfloat16

_NN = (((1,), (0,)), ((), ()))


def _dot(a, b, dims=_NN, precision=None):
    return lax.dot_general(a, b, dims, precision=precision, preferred_element_type=F32)


def _rmsnorm(x, g):
    r = lax.rsqrt(jnp.mean(x * x, axis=-1, keepdims=True) + RMS_EPS)
    return (x * r) * g


def _gelu_tanh(x):
    c = math.sqrt(2.0 / math.pi)
    return 0.5 * x * (1.0 + jnp.tanh(c * (x + 0.044715 * (x * x * x))))


def _alibi_slopes(n):
    start = 2.0 ** (-8.0 / n)
    return np.asarray([start ** (i + 1) for i in range(n)], dtype=np.float32)


def _split_weights(w_ref, whi_ref, wlo_ref, n_precise):
    rows = w_ref.shape[0]
    step = 128
    for r in range(0, rows, step):
        w = w_ref[r:r + step, :]
        hi = w.astype(BF16)
        whi_ref[r:r + step, :] = hi
        wlo_ref[r:r + step, :] = (w[:, :n_precise] - hi[:, :n_precise].astype(F32)).astype(BF16)


def _project(x, g1_ref, whi_ref, wlo_ref):
    n_precise = wlo_ref.shape[1]
    h = _rmsnorm(x, g1_ref[...])
    h_hi = h.astype(BF16)
    h_lo = (h - h_hi.astype(F32)).astype(BF16)
    w_qk = whi_ref[:, :n_precise]
    z_qk = _dot(h_hi, w_qk) + (_dot(h_lo, w_qk) + _dot(h_hi, wlo_ref[...]))
    z_rest = _dot(h_hi, whi_ref[:, n_precise:])
    return z_qk, z_rest


def _layernorm(x, g, b):
    mu = jnp.mean(x, axis=-1, keepdims=True)
    xc = x - mu
    var = jnp.mean(xc * xc, axis=-1, keepdims=True)
    return (xc * lax.rsqrt(var + LN_EPS)) * g + b


def _inproj_prompt_kernel(x_ref, g1_ref, w_ref, lng_ref, lnb_ref, sw_ref, sbx_ref, sog_ref, *refs):
    later_w = refs[:N_LATER_WEIGHTS]
    q_ref, kb_ref, kt_ref, vt_ref, vtb_ref, kmean_ref, gn_ref = refs[N_LATER_WEIGHTS:N_LATER_WEIGHTS + 7]
    later_wb = refs[N_LATER_WEIGHTS + 7:2 * N_LATER_WEIGHTS + 7]
    wb_ref, wcat_ref, s_ref = refs[2 * N_LATER_WEIGHTS + 7:]
    tm = x_ref.shape[1]
    a = ATTN_WIDTH
    blk = MOBA_BLOCK

    for src, dst in zip(later_w, later_wb):
        dst[...] = src[...].astype(BF16)

    @pl.when((pl.program_id(0) == 0) & (pl.program_id(1) == 0))
    def _prepare():
        for r in range(0, w_ref.shape[0], LANES):
            wb_ref[r:r + LANES, :] = w_ref[r:r + LANES, :].astype(BF16)
        t = lax.broadcasted_iota(jnp.int32, (SGU_CHUNK, SGU_CHUNK), 0)
        s = lax.broadcasted_iota(jnp.int32, (SGU_CHUNK, SGU_CHUNK), 1)
        causal = t >= s
        for gp in range(N_SGU_GROUPS // 2):
            w0 = jnp.where(causal, sw_ref[2 * gp], 0.0)
            w1 = jnp.where(causal, sw_ref[2 * gp + 1], 0.0)
            wcat_ref[gp] = jnp.concatenate([w0, w1], axis=1).astype(BF16)

    hb = _rmsnorm(x_ref[0], g1_ref[...]).astype(BF16)
    z_gate = _dot(hb, wb_ref[:, 3 * a:])
    z = _dot(hb, wb_ref[:, :3 * a])
    q_ref[0] = z[:, :a].T
    zk = z[:, a:2 * a]
    zv = z[:, 2 * a:]
    kt_ref[0] = zk.T
    vt = zv.T
    vt_ref[0] = vt
    for r in range(tm // blk):
        rows = slice(r * blk, (r + 1) * blk)
        kb_ref[0, r] = zk[rows, :].astype(BF16)
        vtb_ref[0, r] = vt[:, rows].astype(BF16)
        kmean_ref[r] = jnp.mean(zk[rows, :], axis=0, keepdims=True)

    u = _gelu_tanh(z_gate[:, :SGU_WIDTH])
    gv = _gelu_tanh(z_gate[:, SGU_WIDTH:])
    vn = _layernorm(gv, lng_ref[...], lnb_ref[...])

    lane = lax.broadcasted_iota(jnp.int32, (SGU_CHUNK, LANES), 1)
    low = lane < SGU_GROUP_DIM
    n_chunks = tm // SGU_CHUNK
    for c in range(0, n_chunks, 2):
        both = range(c, min(c + 2, n_chunks))
        for gp in range(N_SGU_GROUPS // 2):
            cols = slice(gp * LANES, (gp + 1) * LANES)
            rhs = []
            for cc in both:
                vp = vn[cc * SGU_CHUNK:(cc + 1) * SGU_CHUNK, cols]
                rhs.append(jnp.concatenate([jnp.where(low, vp, 0.0), jnp.where(low, 0.0, vp)], axis=0))
            s = _dot(wcat_ref[gp], jnp.concatenate(rhs, axis=1).astype(BF16))
            for k, cc in enumerate(both):
                s_ref[cc * SGU_CHUNK:(cc + 1) * SGU_CHUNK, cols] = s[:, k * LANES:(k + 1) * LANES] + sbx_ref[:, cols]
    g = u * s_ref[...]
    gn_ref[0] = _rmsnorm(g, sog_ref[...]).astype(gn_ref.dtype)


def _inproj_sample_kernel(x_ref, g1_ref, w_ref, lng_ref, lnb_ref, w00_ref, b0_ref, sog_ref,
                          q_ref, k_ref, v_ref, gn_ref, vn_ref,
                          whi_ref, wlo_ref):
    a = ATTN_WIDTH

    @pl.when(pl.program_id(0) == 0)
    def _prepare():
        _split_weights(w_ref, whi_ref, wlo_ref, 2 * a)

    z_qk, z_rest = _project(x_ref[...], g1_ref, whi_ref, wlo_ref)
    q_ref[...] = z_qk[:, :a]
    k_ref[...] = z_qk[:, a:]
    v_ref[...] = z_rest[:, :a]
    u = _gelu_tanh(z_rest[:, a:a + SGU_WIDTH])
    gv = _gelu_tanh(z_rest[:, a + SGU_WIDTH:])
    vn = _layernorm(gv, lng_ref[...], lnb_ref[...])
    vn_ref[...] = vn
    g = u * (vn * w00_ref[...] + b0_ref[...])
    gn_ref[...] = _rmsnorm(g, sog_ref[...]).astype(gn_ref.dtype)


def _const_spec(shape):
    zeros = (0,) * len(shape)
    return pl.BlockSpec(shape, lambda *_: zeros, pipeline_mode=pl.Buffered(1))


N_LATER_WEIGHTS = 4


def _inproj_prompt(x, g1, w_in, ln_g, ln_b, sgu_w, sgu_b, sog, later_weights, *, tm):
    b, t, d = x.shape
    a = ATTN_WIDTH
    blk = MOBA_BLOCK
    assert t % tm == 0 and tm % SGU_CHUNK == 0 and tm % blk == 0 and len(later_weights) == N_LATER_WEIGHTS
    nb = t // blk
    per = tm // blk
    steps = t // tm
    assert all(w.shape[0] % (steps * BF16_SUBLANES) == 0 and w.shape[1] % (b * LANES) == 0 for w in later_weights)
    w_spec = lambda w: pl.BlockSpec((w.shape[0] // steps, w.shape[1] // b), lambda bi, i: (i, bi))
    sbx = jnp.repeat(sgu_b.T, SGU_GROUP_DIM, axis=1)
    row = lambda w: pl.BlockSpec((1, tm, w), lambda bi, i: (bi, i, 0))
    col = lambda w: pl.BlockSpec((1, w, tm), lambda bi, i: (bi, 0, i))
    out_specs = (col(a),
                 pl.BlockSpec((1, per, blk, a), lambda bi, i: (bi, i, 0, 0)),
                 col(a), col(a),
                 pl.BlockSpec((1, per, a, blk), lambda bi, i: (bi, i, 0, 0)),
                 pl.BlockSpec((per, 1, a), lambda bi, i: (bi * (nb // per) + i, 0, 0)),
                 row(SGU_WIDTH)) + tuple(w_spec(w) for w in later_weights)
    out_shape = (jax.ShapeDtypeStruct((b, a, t), F32),
                 jax.ShapeDtypeStruct((b, nb, blk, a), BF16),
                 jax.ShapeDtypeStruct((b, a, t), F32),
                 jax.ShapeDtypeStruct((b, a, t), F32),
                 jax.ShapeDtypeStruct((b, nb, a, blk), BF16),
                 jax.ShapeDtypeStruct((b * nb, 1, a), F32),
                 jax.ShapeDtypeStruct((b, t, SGU_WIDTH), BF16)) + tuple(
                     jax.ShapeDtypeStruct(w.shape, BF16) for w in later_weights)
    return pl.pallas_call(
        _inproj_prompt_kernel,
        grid=(b, steps),
        in_specs=[row(d), _const_spec((1, d)), _const_spec(w_in.shape), _const_spec((1, SGU_WIDTH)),
                  _const_spec((1, SGU_WIDTH)), _const_spec(sgu_w.shape), _const_spec(sbx.shape),
                  _const_spec((1, SGU_WIDTH))] + [w_spec(w) for w in later_weights],
        out_specs=out_specs,
        out_shape=out_shape,
        scratch_shapes=[pltpu.VMEM(w_in.shape, BF16),
                        pltpu.VMEM((N_SGU_GROUPS // 2, SGU_CHUNK, 2 * SGU_CHUNK), BF16),
                        pltpu.VMEM((tm, SGU_WIDTH), F32)],
        compiler_params=pltpu.CompilerParams(dimension_semantics=("arbitrary", "arbitrary"),
                                             vmem_limit_bytes=VMEM_LIMIT_WEIGHT_RESIDENT),
        name="inproj_prompt",
    )(x, g1.reshape(1, d), w_in, ln_g.reshape(1, -1), ln_b.reshape(1, -1), sgu_w, sbx, sog.reshape(1, -1),
      *later_weights)


def _inproj_sample(x, g1, w_in, ln_g, ln_b, sgu_w, sgu_b, sog):
    n, d = x.shape
    a = ATTN_WIDTH
    w00 = jnp.repeat(sgu_w[:, 0, 0], SGU_GROUP_DIM).reshape(1, SGU_WIDTH)
    b0 = jnp.repeat(sgu_b[:, 0], SGU_GROUP_DIM).reshape(1, SGU_WIDTH)
    row = lambda w: pl.BlockSpec((n, w), lambda i: (0, 0))
    vec = _const_spec((1, SGU_WIDTH))
    out_shape = (jax.ShapeDtypeStruct((n, a), F32),) * 3 + (jax.ShapeDtypeStruct((n, SGU_WIDTH), BF16),
                                                           jax.ShapeDtypeStruct((n, SGU_WIDTH), F32))
    return pl.pallas_call(
        _inproj_sample_kernel,
        grid=(1,),
        in_specs=[row(d), _const_spec((1, d)), _const_spec(w_in.shape), vec, vec, vec, vec, vec],
        out_specs=(row(a), row(a), row(a), row(SGU_WIDTH), row(SGU_WIDTH)),
        out_shape=out_shape,
        scratch_shapes=[pltpu.VMEM(w_in.shape, BF16), pltpu.VMEM((d, 2 * a), BF16)],
        compiler_params=pltpu.CompilerParams(dimension_semantics=("arbitrary",),
                                             vmem_limit_bytes=VMEM_LIMIT_WEIGHT_RESIDENT),
        name="inproj_sample",
    )(x, g1.reshape(1, d), w_in, ln_g.reshape(1, -1), ln_b.reshape(1, -1), w00, b0, sog.reshape(1, -1))


def _block_choice(gate_t, own):
    nb = gate_t.shape[0]
    blk = lax.broadcasted_iota(jnp.int32, gate_t.shape, 0)
    g = jnp.where(blk < own, gate_t, NEG_INF)
    picked = jnp.zeros(gate_t.shape, F32)
    for _ in range(MOBA_TOPK):
        top = jnp.max(g, axis=0, keepdims=True)
        first = jnp.min(jnp.where(g == top, blk, nb), axis=0, keepdims=True)
        hit = blk == first
        picked = jnp.where(hit, 1.0, picked)
        g = jnp.where(hit, -jnp.inf, g)
    return jnp.where(blk < own, picked, 0.0)


def _split3(x):
    hi = x.astype(BF16)
    r = x - hi.astype(F32)
    mid = r.astype(BF16)
    return hi, mid, (r - mid.astype(F32)).astype(BF16)


def _attn_prompt_kernel(q_ref, kb_ref, vtb_ref, km_ref, sl_ref, o_ref,
                        kmh_ref, kml_ref, kl_ref, qs_ref, sel_ref, acc_ref, m_ref, l_ref, s_ref, cm_ref, d_ref):
    i = pl.program_id(1)
    blk = MOBA_BLOCK
    nb = kb_ref.shape[1]
    tq = q_ref.shape[2]
    scale2 = LOG2E / math.sqrt(HEAD_DIM)

    @pl.when(i == 0)
    def _prepare():
        km = km_ref[:, 0, :]
        lane_head = lax.broadcasted_iota(jnp.int32, km.shape, 1) // HEAD_DIM
        for h in range(N_HEADS):
            mine = jnp.where(lane_head == h, km, 0.0)
            hi = mine.astype(BF16)
            kmh_ref[h * nb:(h + 1) * nb, :] = hi
            kml_ref[h * nb:(h + 1) * nb, :] = (mine - hi.astype(F32)).astype(BF16)
        lane = lax.broadcasted_iota(jnp.int32, (blk, BIAS_LANES), 1)
        kl = lax.broadcasted_iota(jnp.int32, (blk, BIAS_LANES), 0).astype(F32)
        kl_ref[...] = jnp.where(lane < 3, kl, 0.0).astype(BF16)
        row = lax.broadcasted_iota(jnp.int32, (BIAS_LANES, tq), 0)
        for h in range(N_HEADS):
            hi, mid, lo = (x.astype(F32) for x in _split3(jnp.broadcast_to(sl_ref[h:h + 1, :], (BIAS_LANES, tq))))
            terms = jnp.where(row == 0, hi, jnp.where(row == 1, mid, jnp.where(row == 2, lo, 0.0)))
            qs_ref[h, LANES:, :] = terms.astype(BF16)

    q_t = q_ref[0]
    first_own = Q_BLOCKS * i
    own = first_own + lax.broadcasted_iota(jnp.int32, (1, tq), 1) // blk
    q_hi = q_t.astype(BF16)
    q_lo = (q_t - q_hi.astype(F32)).astype(BF16)
    gate_t = _dot(kmh_ref[...], q_hi) + (_dot(kml_ref[...], q_hi) + _dot(kmh_ref[...], q_lo))
    row = lax.broadcasted_iota(jnp.int32, (LANES, tq), 0)
    for h in range(N_HEADS):
        pair, hh = divmod(h, HEADS_PER_TILE)
        q_pair = q_t[pair * LANES:(pair + 1) * LANES, :]
        mine = (row >= hh * HEAD_DIM) & (row < (hh + 1) * HEAD_DIM)
        qs_ref[h, :LANES, :] = (jnp.where(mine, q_pair, 0.0) * scale2).astype(BF16)

    def scores(j, h, first_query=0):
        pair = h // HEADS_PER_TILE
        keys = jnp.concatenate([kb_ref[0, j, :, pair * LANES:(pair + 1) * LANES], kl_ref[...]], axis=1)
        return _dot(keys, qs_ref[h, :, first_query:])

    ones_rows = jnp.ones((BF16_SUBLANES, blk), BF16)

    def value_product(j, h, p):
        rows = jnp.concatenate([vtb_ref[0, j, h * HEAD_DIM:(h + 1) * HEAD_DIM, :], ones_rows], axis=0)
        pv = _dot(rows, p.astype(BF16))
        return pv[:HEAD_DIM], pv[HEAD_DIM:HEAD_DIM + 1]

    def issue_scores(j, slot, h):
        s = scores(j, h)
        s_ref[slot, h] = s
        cm_ref[slot, h] = jnp.max(s, axis=0, keepdims=True)

    def block_bias(j, h):
        return sl_ref[h:h + 1, :] * ((j - own).astype(F32) * float(blk))

    for d in range(Q_BLOCKS):
        for h in range(N_HEADS):
            d_ref[d, h, :, d * blk:] = scores(first_own + d, h, first_query=d * blk)
    for h in range(N_HEADS):
        issue_scores(0, 0, h)
    for h in range(N_HEADS):
        sel_ref[h] = _block_choice(gate_t[h * nb:(h + 1) * nb, :], own)

    causal = lax.broadcasted_iota(jnp.int32, (blk, blk), 0) <= lax.broadcasted_iota(jnp.int32, (blk, blk), 1)
    for h in range(N_HEADS):
        for a in range(Q_BLOCKS):
            lanes = slice(a * blk, (a + 1) * blk)
            s = jnp.where(causal, d_ref[a, h, :, lanes], NEG_INF)
            m = jnp.max(s, axis=0, keepdims=True)
            acc, l = value_product(first_own + a, h, jnp.exp2(s - m))
            for d in range(a):
                j = first_own + d
                c = block_bias(j, h)[:, lanes]
                chosen = sel_ref[h, pl.ds(j, 1), lanes] > 0.5
                s = d_ref[d, h, :, lanes]
                m_new = jnp.where(chosen, jnp.maximum(m, jnp.max(s, axis=0, keepdims=True) + c), m)
                alpha = jnp.exp2(m - m_new)
                pv, p_sum = value_product(j, h, jnp.exp2(s - jnp.where(chosen, m_new - c, -NEG_INF)))
                m, l, acc = m_new, alpha * l + p_sum, alpha * acc + pv
            m_ref[h, :, lanes] = m
            l_ref[h, :, lanes] = l
            acc_ref[h, :, lanes] = acc

    def finish_block(j, slot, h):
        m, l = m_ref[h], l_ref[h]
        c = block_bias(j, h)
        chosen = sel_ref[h, pl.ds(j, 1), :] > 0.5
        m_new = jnp.where(chosen, jnp.maximum(m, cm_ref[slot, h] + c), m)
        alpha = jnp.exp2(m - m_new)
        shift = jnp.where(chosen, m_new - c, -NEG_INF)
        pv, p_sum = value_product(j, h, jnp.exp2(s_ref[slot, h] - shift))
        m_ref[h] = m_new
        l_ref[h] = alpha * l + p_sum
        acc_ref[h] = alpha * acc_ref[h] + pv

    def trip(t, carry):
        for slot in range(2):
            j = 2 * t + slot
            for h in range(N_HEADS):
                issue_scores(jnp.minimum(j + 1, nb - 1), 1 - slot, h)
                finish_block(j, slot, h)
        return carry

    lax.fori_loop(0, first_own // 2, trip, 0)
    for h in range(N_HEADS):
        o_ref[0, h * HEAD_DIM:(h + 1) * HEAD_DIM, :] = acc_ref[h] / l_ref[h]


def _attn_prompt(q_t, kb, vtb, kmean):
    b, a, t = q_t.shape
    blk = MOBA_BLOCK
    nb = t // blk
    tq = Q_BLOCKS * blk
    assert Q_BLOCKS % 2 == 0 and t % tq == 0 and a == ATTN_WIDTH and kb.shape == (b, nb, blk, a) and vtb.shape == (b, nb, a, blk)
    slopes2 = (_alibi_slopes(N_HEADS) * np.float32(LOG2E)).reshape(N_HEADS, 1)
    slopes2 = jnp.asarray(np.broadcast_to(slopes2, (N_HEADS, tq)).copy())
    once = pl.Buffered(1)
    return pl.pallas_call(
        _attn_prompt_kernel,
        grid=(b, t // tq),
        in_specs=[pl.BlockSpec((1, a, tq), lambda bi, i: (bi, 0, i)),
                  pl.BlockSpec((1, nb, blk, a), lambda bi, i: (bi, 0, 0, 0), pipeline_mode=once),
                  pl.BlockSpec((1, nb, a, blk), lambda bi, i: (bi, 0, 0, 0), pipeline_mode=once),
                  pl.BlockSpec((nb, 1, a), lambda bi, i: (bi, 0, 0)),
                  pl.BlockSpec((N_HEADS, tq), lambda bi, i: (0, 0))],
        out_specs=pl.BlockSpec((1, a, tq), lambda bi, i: (bi, 0, i)),
        out_shape=jax.ShapeDtypeStruct((b, a, t), F32),
        scratch_shapes=[pltpu.VMEM((N_HEADS * nb, a), BF16),
                        pltpu.VMEM((N_HEADS * nb, a), BF16),
                        pltpu.VMEM((blk, BIAS_LANES), BF16),
                        pltpu.VMEM((N_HEADS, LANES + BIAS_LANES, tq), BF16),
                        pltpu.VMEM((N_HEADS, nb, tq), F32),
                        pltpu.VMEM((N_HEADS, HEAD_DIM, tq), F32),
                        pltpu.VMEM((N_HEADS, 1, tq), F32),
                        pltpu.VMEM((N_HEADS, 1, tq), F32),
                        pltpu.VMEM((2, N_HEADS, blk, tq), F32),
                        pltpu.VMEM((2, N_HEADS, 1, tq), F32),
                        pltpu.VMEM((Q_BLOCKS, N_HEADS, blk, tq), F32)],
        compiler_params=pltpu.CompilerParams(dimension_semantics=("arbitrary", "arbitrary"),
                                             vmem_limit_bytes=VMEM_LIMIT_ATTENTION),
        name="attn_prompt",
    )(q_t, kb, vtb, kmean, slopes2)


def _sample_attention(q_t, kn_t, vn_t, slopes, k_tile, v_tile, n_pages, page):
    past = n_pages * page
    pages_per_block = MOBA_BLOCK // page
    nb = past // MOBA_BLOCK
    scale = 1.0 / math.sqrt(HEAD_DIM)

    head_row = lax.broadcasted_iota(jnp.int32, (N_HEADS, page), 0)
    raw = [jnp.zeros((N_HEADS, page), F32) for _ in range(n_pages + 1)]
    for h in range(N_HEADS):
        q_b = jnp.broadcast_to(q_t[:, h:h + 1], (HEAD_DIM, page))
        tiles = [k_tile(pg, h) for pg in range(n_pages)] + [jnp.broadcast_to(kn_t[:, h:h + 1], (HEAD_DIM, page))]
        for pg, tile in enumerate(tiles):
            raw[pg] = jnp.where(head_row == h, jnp.sum(q_b * tile, axis=0, keepdims=True), raw[pg])
    gate = []
    for n in range(nb):
        tot = raw[n * pages_per_block]
        for r in range(1, pages_per_block):
            tot = tot + raw[n * pages_per_block + r]
        gate.append(jnp.sum(tot, axis=1, keepdims=True) * (1.0 / MOBA_BLOCK))
    chosen = []
    for n in range(nb):
        rank = jnp.where(NEG_INF > gate[n], 1.0, 0.0)
        for m in range(nb):
            if m != n:
                ahead = (gate[m] >= gate[n]) if m < n else (gate[m] > gate[n])
                rank = rank + jnp.where(ahead, 1.0, 0.0)
        chosen.append(jnp.where(rank < MOBA_TOPK, 1.0, 0.0))

    lane = lax.broadcasted_iota(jnp.int32, (N_HEADS, page), 1)
    dist0 = (past - lane).astype(F32)
    s_pages = []
    for pg in range(n_pages):
        s = raw[pg] * scale - slopes * (dist0 - float(pg * page))
        ok = jnp.broadcast_to(chosen[pg // pages_per_block], s.shape) > 0.5
        s_pages.append(jnp.where(ok, s, NEG_INF))
    s_pages.append(jnp.where(lane == 0, raw[n_pages] * scale, NEG_INF))
    m = jnp.max(s_pages[0], axis=1, keepdims=True)
    for s in s_pages[1:]:
        m = jnp.maximum(m, jnp.max(s, axis=1, keepdims=True))
    p_pages = [jnp.exp(s - m) for s in s_pages]
    l = jnp.sum(p_pages[0], axis=1, keepdims=True)
    for p in p_pages[1:]:
        l = l + jnp.sum(p, axis=1, keepdims=True)
    outs = []
    for h in range(N_HEADS):
        acc = jnp.broadcast_to(p_pages[n_pages][h:h + 1, :], (HEAD_DIM, page)) * \
            jnp.broadcast_to(vn_t[:, h:h + 1], (HEAD_DIM, page))
        for pg in range(n_pages):
            acc = acc + jnp.broadcast_to(p_pages[pg][h:h + 1, :], (HEAD_DIM, page)) * v_tile(pg, h)
        outs.append(jnp.sum(acc, axis=1, keepdims=True) / l[h:h + 1, :])
    return jnp.concatenate(outs, axis=1)


SEQS_PER_STEP = 2


def _out_proj(x_ref, a_ref, gn_ref, ag_ref, wo_ref, g2_ref):
    a = ATTN_WIDTH
    attn = a_ref[0].T if len(a_ref.shape) == 3 else a_ref[...]
    an = _rmsnorm(attn, ag_ref[...]).astype(BF16)
    x1 = x_ref[...] + (_dot(an, wo_ref[:a, :]) + _dot(gn_ref[...], wo_ref[a:, :]))
    return x1, _rmsnorm(x1, g2_ref[...]).astype(BF16)


def _ffn_half(hf, wg_ref, wu_ref, wd_ref, half):
    w = wg_ref.shape[1] // 2
    cols = slice(half * w, (half + 1) * w)
    gate = _dot(hf, wg_ref[:, cols])
    up = _dot(hf, wu_ref[:, cols])
    half_gate = 0.5 * gate
    act = ((half_gate + half_gate * jnp.tanh(half_gate)) * up).astype(BF16)
    return _dot(act, wd_ref[cols, :])


def _outffn_kernel(x_ref, a_ref, gn_ref, ag_ref, wo_ref, g2_ref, wg_ref, wu_ref, wd_ref, gf_ref, y_ref):
    x1, hf = _out_proj(x_ref, a_ref, gn_ref, ag_ref, wo_ref, g2_ref)
    ff = _ffn_half(hf, wg_ref, wu_ref, wd_ref, 0) + _ffn_half(hf, wg_ref, wu_ref, wd_ref, 1)
    y_ref[...] = _rmsnorm(x1 + ff, gf_ref[...])


def _outffn_attn_sample_kernel(pt_ref, x_ref, a_ref, gn_ref, ag_ref, wo_ref, g2_ref, wg_ref, wu_ref, wd_ref,
                               gf_ref, q_ref, kn_ref, vn_ref, sl_ref, ck_hbm, cv_hbm, y_ref, o_ref,
                               kbuf, vbuf, sem, x1_ref, hf_ref, ff_ref):
    t = pl.program_id(0)
    last_seq = SEQS_PER_STEP * pl.num_programs(0) - 1
    n_pages, page = kbuf.shape[1], kbuf.shape[-1]

    def page_copies(seq, slot):
        copies = []
        for pg in range(n_pages):
            pid = pt_ref[seq, pg]
            copies.append(pltpu.make_async_copy(ck_hbm.at[pid], kbuf.at[slot, pg], sem.at[slot, 0]))
            copies.append(pltpu.make_async_copy(cv_hbm.at[pid], vbuf.at[slot, pg], sem.at[slot, 1]))
        return copies

    def fetch(seq, slot):
        for c in page_copies(seq, slot):
            c.start()

    def wait(seq, slot):
        for c in page_copies(seq, slot):
            c.wait()

    def attend(slot):
        o_ref[slot] = _sample_attention(q_ref[slot], kn_ref[slot], vn_ref[slot], sl_ref[:, :1],
                                        lambda pg, h: kbuf[slot, pg, h], lambda pg, h: vbuf[slot, pg, h],
                                        n_pages, page)

    @pl.when(t == 0)
    def _first():
        fetch(0, 0)

    seq0 = SEQS_PER_STEP * t
    wait(seq0, 0)
    fetch(seq0 + 1, 1)
    x1, hf = _out_proj(x_ref, a_ref, gn_ref, ag_ref, wo_ref, g2_ref)
    x1_ref[...] = x1
    hf_ref[...] = hf
    ff_ref[...] = _ffn_half(hf, wg_ref, wu_ref, wd_ref, 0)
    attend(0)

    wait(seq0 + 1, 1)
    nxt = jnp.minimum(seq0 + 2, last_seq)
    fetch(nxt, 0)
    ff = ff_ref[...] + _ffn_half(hf_ref[...], wg_ref, wu_ref, wd_ref, 1)
    y_ref[...] = _rmsnorm(x1_ref[...] + ff, gf_ref[...])
    attend(1)

    @pl.when(t == pl.num_programs(0) - 1)
    def _drain():
        wait(last_seq, 0)


def _outffn(x, a, gn, ag, w_out, g2, w_gate, w_up, w_down, gf):
    n, d = x.shape
    assert (w_gate.shape[1] // 2) % LANES == 0
    row = lambda w: pl.BlockSpec((n, w), lambda i: (0, 0))
    return pl.pallas_call(
        _outffn_kernel,
        grid=(1,),
        in_specs=[row(d), row(ATTN_WIDTH), row(SGU_WIDTH), _const_spec((1, ATTN_WIDTH)), _const_spec(w_out.shape),
                  _const_spec((1, d)), _const_spec(w_gate.shape), _const_spec(w_up.shape),
                  _const_spec(w_down.shape), _const_spec((1, d))],
        out_specs=row(d),
        out_shape=jax.ShapeDtypeStruct((n, d), F32),
        compiler_params=pltpu.CompilerParams(dimension_semantics=("arbitrary",),
                                             vmem_limit_bytes=VMEM_LIMIT_WEIGHT_RESIDENT),
        name="outffn_sample",
    )(x, a, gn, ag.reshape(1, -1), w_out, g2.reshape(1, -1), w_gate, w_up, w_down, gf.reshape(1, -1))


def _outffn_with_sample_attention(x, a_t, gn, ag, w_out, g2, w_gate, w_up, w_down, gf,
                                  q, k_new, v_new, cache_k, cache_v, page_table):
    n, d = x.shape
    n_seq, a = q.shape
    n_pool, page, h, dh = cache_k.shape
    n_pages = page_table.shape[1]
    assert h * dh == a and MOBA_BLOCK % page == 0 and (n_pages * page) % MOBA_BLOCK == 0
    assert n_seq % SEQS_PER_STEP == 0 and (w_gate.shape[1] // 2) % LANES == 0
    steps = n_seq // SEQS_PER_STEP
    assert n % steps == 0
    tm = n // steps
    per_seq = a_t.shape[2] // tm
    assert tm % LANES == 0 and a_t.shape[2] % tm == 0 and a_t.shape[0] * a_t.shape[2] == n
    slopes = jnp.asarray(np.broadcast_to(_alibi_slopes(N_HEADS).reshape(N_HEADS, 1), (N_HEADS, LANES)).copy())
    ck = jnp.transpose(cache_k, (0, 2, 3, 1))
    cv = jnp.transpose(cache_v, (0, 2, 3, 1))
    tok_t = lambda z: jnp.transpose(z.reshape(n_seq, h, dh), (0, 2, 1))
    row = lambda w: pl.BlockSpec((tm, w), lambda i, pt: (i, 0))
    tok = pl.BlockSpec((SEQS_PER_STEP, dh, h), lambda i, pt: (i, 0, 0))
    hbm = pl.BlockSpec(memory_space=pl.ANY)
    grid_spec = pltpu.PrefetchScalarGridSpec(
        num_scalar_prefetch=1,
        grid=(steps,),
        in_specs=[row(d), pl.BlockSpec((1, ATTN_WIDTH, tm), lambda i, pt: (i // per_seq, 0, i % per_seq)),
                  row(SGU_WIDTH), _const_spec((1, ATTN_WIDTH)), _const_spec(w_out.shape), _const_spec((1, d)),
                  _const_spec(w_gate.shape), _const_spec(w_up.shape), _const_spec(w_down.shape),
                  _const_spec((1, d)),
                  tok, tok, tok, pl.BlockSpec((N_HEADS, LANES), lambda i, pt: (0, 0)), hbm, hbm],
        out_specs=(row(d), tok),
        scratch_shapes=[pltpu.VMEM((SEQS_PER_STEP, n_pages, h, dh, page), F32),
                        pltpu.VMEM((SEQS_PER_STEP, n_pages, h, dh, page), F32),
                        pltpu.SemaphoreType.DMA((SEQS_PER_STEP, 2)),
                        pltpu.VMEM((tm, d), F32),
                        pltpu.VMEM((tm, d), BF16),
                        pltpu.VMEM((tm, d), F32)],
    )
    y, out = pl.pallas_call(
        _outffn_attn_sample_kernel,
        grid_spec=grid_spec,
        out_shape=(jax.ShapeDtypeStruct((n, d), F32), jax.ShapeDtypeStruct((n_seq, dh, h), F32)),
        compiler_params=pltpu.CompilerParams(dimension_semantics=("arbitrary",),
                                             vmem_limit_bytes=VMEM_LIMIT_WEIGHT_RESIDENT),
        name="outffn_prompt_attn_sample",
    )(page_table, x, a_t, gn, ag.reshape(1, -1), w_out, g2.reshape(1, -1), w_gate, w_up, w_down,
      gf.reshape(1, -1), tok_t(q), tok_t(k_new), tok_t(v_new), slopes, ck, cv)
    return y, jnp.transpose(out, (0, 2, 1)).reshape(n_seq, a)


def kernel(x_prompt, x_sample, cache_k, cache_v, page_table, norm1_g, w_in, attn_out_g, sgu_ln_g, sgu_ln_b,
           sgu_w, sgu_b, sgu_out_g, w_out, norm2_g, w_gate, w_up, w_down, final_g):
    depth = w_in.shape[0]
    assert depth == 1, "single-layer stack"
    l = 0
    bsz, seq, d = x_prompt.shape
    dec_b, dec_seq, _ = x_sample.shape
    assert dec_seq == 1
    xs = x_sample.reshape(dec_b, d)

    qp_t, kb, kt, vt, vtb, kmean, gnp, wo, wg, wu, wd = _inproj_prompt(
        x_prompt, norm1_g[l], w_in[l], sgu_ln_g[l], sgu_ln_b[l], sgu_w[l], sgu_b[l], sgu_out_g[l],
        (w_out[l], w_gate[l], w_up[l], w_down[l]), tm=512)
    ap_t = _attn_prompt(qp_t, kb, vtb, kmean)

    qs, ks, vs, gns, vns = _inproj_sample(xs, norm1_g[l], w_in[l], sgu_ln_g[l], sgu_ln_b[l], sgu_w[l], sgu_b[l],
                                          sgu_out_g[l])
    yp, a_s = _outffn_with_sample_attention(
        x_prompt.reshape(bsz * seq, d), ap_t, gnp.reshape(bsz * seq, -1), attn_out_g[l], wo, norm2_g[l],
        wg, wu, wd, final_g, qs, ks, vs, cache_k[l], cache_v[l], page_table)
    ys = _outffn(xs, a_s, gns, attn_out_g[l], wo, norm2_g[l], wg, wu, wd, final_g)

    heads_last = lambda x_t: jnp.transpose(x_t.reshape(bsz, N_HEADS, HEAD_DIM, seq), (0, 3, 1, 2))[None]
    y_prompt = yp.reshape(bsz, seq, d)
    y_sample = ys.reshape(dec_b, dec_seq, d)
    k_prompt = heads_last(kt)
    v_prompt = heads_last(vt)
    k_sample = ks.reshape(depth, dec_b, dec_seq, N_HEADS, HEAD_DIM)
    v_sample = vs.reshape(depth, dec_b, dec_seq, N_HEADS, HEAD_DIM)
    sgu_v_sample = vns.reshape(depth, dec_b, dec_seq, SGU_WIDTH)
    return (y_prompt, y_sample, k_prompt, v_prompt, k_sample, v_sample, sgu_v_sample)
```
